```python
import math
import jax, jax.numpy as jnp
from jax import lax
import numpy as np

D_MODEL = 1024
BATCH = 2
SEQ = 8192
DEPTH = 1

GRID_W = 64
CTX_LEN = 256
N_HEADS = 8
HEAD_DIM = 128
D_GDN = N_HEADS * HEAD_DIM
CONV_K = 5
GDN_CHUNK = 64
SGU_GROUPS = 8
SGU_GROUP_DIM = 128
D_SGU = SGU_GROUPS * SGU_GROUP_DIM
SGU_CHUNK = 128
ROWS_PER_CHUNK = SGU_CHUNK // GRID_W
N_EXPERT_GROUPS = 4
EXPERTS_PER_GROUP = 8
N_EXPERTS = N_EXPERT_GROUPS * EXPERTS_PER_GROUP
TOP_K = 2
D_EXPERT = 256
N_MOD = 6
EPS = 1e-6
COL_BETA = 3 * D_GDN
COL_A = COL_BETA + 2 * N_HEADS
COL_Z = COL_A + 2 * N_HEADS
COL_U = COL_Z + D_GDN
COL_V = COL_U + D_SGU
COL_GATE = COL_V + D_SGU
D_IN = COL_GATE + 2 * D_MODEL

kernel_name = 'hybrid_gdn_sgu_hmoe_dit_block'


def rms_norm(x, gain):
    xf = x.astype(jnp.float32)
    y = xf * lax.rsqrt(jnp.mean(xf * xf, axis=-1, keepdims=True) + EPS)
    return (y * gain.astype(jnp.float32)).astype(x.dtype)


def layer_norm(x, gain, bias):
    xf = x.astype(jnp.float32)
    mu = jnp.mean(xf, axis=-1, keepdims=True)
    var = jnp.mean(jnp.square(xf - mu), axis=-1, keepdims=True)
    y = (xf - mu) * lax.rsqrt(var + EPS)
    return (y * gain.astype(jnp.float32) + bias.astype(jnp.float32)).astype(x.dtype)


def l2_normalize(t):
    return t * lax.rsqrt(jnp.sum(t * t, axis=-1, keepdims=True) + EPS)


def modulate(h, shift, scale):
    return h * (1.0 + scale[:, None, :]) + shift[:, None, :]


def adaln_params(cond, w, b):
    m = jax.nn.silu(cond) @ w + b
    return m.reshape(cond.shape[0], N_MOD, D_MODEL)


def centred_depthwise_conv(x, w):
    C = x.shape[-1]
    pad = CONV_K // 2
    return lax.conv_general_dilated(
        x, w.reshape(CONV_K, 1, C).astype(x.dtype), window_strides=(1,), padding=((pad, pad),),
        dimension_numbers=('NWC', 'WIO', 'NWC'), feature_group_count=C)


def gated_delta_chunked(q, k, v, g, beta, s0):
    B, H, L, dk = q.shape
    n = L // GDN_CHUNK
    C = GDN_CHUNK
    rs = lambda t: t.reshape(B, H, n, C, *t.shape[3:])
    q, k, v, g, beta = rs(q), rs(k), rs(v), rs(g), rs(beta)
    G = jnp.cumsum(g, axis=-1)
    incl = jnp.tril(jnp.ones((C, C), dtype=bool))
    strict = jnp.tril(jnp.ones((C, C), dtype=bool), -1)
    decay = jnp.exp(jnp.where(incl, G[..., :, None] - G[..., None, :], -jnp.inf))
    kb = k * beta[..., None]
    low = jnp.where(strict, jnp.einsum('bhncd,bhnsd->bhncs', kb, k) * decay, 0.0)
    eye = jnp.eye(C, dtype=q.dtype)
    T = lax.linalg.triangular_solve(eye + low, jnp.broadcast_to(eye, low.shape), left_side=True, lower=True)
    u = T @ (v * beta[..., None])
    w = T @ (kb * jnp.exp(G)[..., None])
    qk_intra = jnp.einsum('bhncd,bhnsd->bhncs', q, k) * decay
    q_dec = q * jnp.exp(G)[..., None]
    k_dec = k * jnp.exp(G[..., -1:] - G)[..., None]
    g_last = jnp.exp(G[..., -1])

    def step(s, inp):
        u_c, w_c, qk_c, qd_c, kd_c, gl_c = inp
        v_new = u_c - w_c @ s
        o_c = qd_c @ s + qk_c @ v_new
        s = s * gl_c[..., None, None] + jnp.swapaxes(kd_c, -1, -2) @ v_new
        return s, o_c

    xs = tuple(jnp.moveaxis(t, 2, 0) for t in (u, w, qk_intra, q_dec, k_dec, g_last))
    s_final, o = lax.scan(step, s0, xs)
    o = jnp.moveaxis(o, 0, 2).reshape(B, H, L, v.shape[-1])
    return o, s_final


def gdn_scan(p_qkv, p_beta, p_a, conv_w, a_log, dt_bias, s_init):
    B, L, _ = p_qkv.shape
    qkv = jax.nn.silu(centred_depthwise_conv(p_qkv, conv_w)).astype(jnp.float32)
    heads = lambda t: jnp.swapaxes(t.reshape(B, L, N_HEADS, HEAD_DIM), 1, 2)
    q, k, v = (heads(t) for t in jnp.split(qkv, 3, axis=-1))
    q = l2_normalize(q) * (HEAD_DIM ** -0.5)
    k = l2_normalize(k)
    dirs = lambda t: t.astype(jnp.float32).reshape(B, L, 2, N_HEADS).transpose(2, 0, 3, 1)
    beta = jax.nn.sigmoid(dirs(p_beta))
    g = -jnp.exp(a_log.astype(jnp.float32))[:, None, :, None] * jax.nn.softplus(
        dirs(p_a) + dt_bias.astype(jnp.float32)[:, None, :, None])
    o_f, s_f = gated_delta_chunked(q, k, v, g[0], beta[0], s_init[0])
    flip = lambda t: jnp.flip(t, axis=2)
    o_b, s_b = gated_delta_chunked(flip(q), flip(k), flip(v), flip(g[1]), flip(beta[1]), s_init[1])
    o = jnp.swapaxes(o_f + flip(o_b), 1, 2)
    return o, jnp.stack([s_f, s_b])


def gdn_output(o, p_z, norm_g):
    B, L = o.shape[:2]
    z = p_z.astype(jnp.float32).reshape(B, L, N_HEADS, HEAD_DIM)
    y = rms_norm(o, norm_g) * jax.nn.silu(z)
    return y.reshape(B, L, D_GDN).astype(p_z.dtype)


def sgu_mixer(p_u, p_v, n_chunks, ln_g, ln_b, w_s, b_s):
    B, L, _ = p_u.shape
    u = jax.nn.gelu(p_u)
    v = layer_norm(jax.nn.gelu(p_v), ln_g, ln_b).reshape(B, n_chunks, SGU_CHUNK, SGU_GROUPS, SGU_GROUP_DIM)
    mixed = jnp.einsum('gpq,bnqgc->bnpgc', w_s, v) + b_s.T[:, :, None]
    return u * mixed.reshape(B, L, D_SGU)


def merge_branches(y_gdn, y_sgu, p_gates, w_a, w_b, w_o):
    g_a, g_b = jnp.split(jax.nn.sigmoid(p_gates), 2, axis=-1)
    return (g_a * (y_gdn @ w_a) + g_b * (y_sgu @ w_b)) @ w_o


def hier_moe(h, w_rg, b_rg, w_re, b_re, w1, w3, w2):
    B, L, D = h.shape
    t = h.reshape(B * L, D)
    p_group = jax.nn.softmax((t @ w_rg + b_rg).astype(jnp.float32), axis=-1)
    grp = jnp.argmax(p_group, axis=-1)
    p_g = jnp.take_along_axis(p_group, grp[:, None], axis=-1)
    logits_e = (t @ w_re + b_re).astype(jnp.float32).reshape(-1, N_EXPERT_GROUPS, EXPERTS_PER_GROUP)
    logits_in = jnp.take_along_axis(logits_e, grp[:, None, None], axis=1)[:, 0]
    top_p, top_i = lax.top_k(jax.nn.softmax(logits_in, axis=-1), TOP_K)
    top_p = top_p / jnp.sum(top_p, axis=-1, keepdims=True) * p_g
    expert_id = grp[:, None] * EXPERTS_PER_GROUP + top_i
    gate = jnp.sum(jax.nn.one_hot(expert_id, N_EXPERTS, dtype=jnp.float32) * top_p[..., None], axis=1)
    gate = gate.astype(h.dtype).reshape(-1, N_EXPERT_GROUPS, EXPERTS_PER_GROUP)
    out = jnp.zeros_like(t)
    for gi in range(N_EXPERT_GROUPS):
        sl = slice(gi * EXPERTS_PER_GROUP, (gi + 1) * EXPERTS_PER_GROUP)
        hid = jax.nn.silu(jnp.einsum('td,edf->tef', t, w1[sl])) * jnp.einsum('td,edf->tef', t, w3[sl])
        out = out + jnp.einsum('tef,efd->td', hid * gate[:, gi, :, None], w2[sl])
    return out.reshape(B, L, D)


def setup_inputs(seed: int = 0) -> dict:
    key = jax.random.key(seed)
    ks = jax.random.split(key, 32)
    f32 = jnp.float32
    nrm = lambda k, shape, scale: jax.random.normal(k, shape, f32) * scale
    gain = lambda k, shape: 1.0 + 0.1 * jax.random.normal(k, shape, f32)
    dt = jnp.exp(jax.random.uniform(ks[9], (DEPTH, 2, N_HEADS), f32, math.log(1e-3), math.log(1e-1)))
    return {
        'x': nrm(ks[0], (BATCH, SEQ, D_MODEL), 1.0),
        'c': nrm(ks[1], (BATCH, D_MODEL), 1.0),
        'ctx': nrm(ks[2], (BATCH, CTX_LEN, D_MODEL), 1.0),
        'c_ctx': nrm(ks[3], (D_MODEL,), 1.0),
        'ada_w': nrm(ks[4], (DEPTH, D_MODEL, N_MOD * D_MODEL), 0.5 * D_MODEL ** -0.5),
        'ada_b': nrm(ks[5], (DEPTH, N_MOD * D_MODEL), 0.02),
        'norm_mix_g': gain(ks[6], (DEPTH, D_MODEL)),
        'w_in': nrm(ks[7], (DEPTH, D_MODEL, D_IN), D_MODEL ** -0.5),
        'conv_w': nrm(ks[8], (DEPTH, CONV_K, 3 * D_GDN), CONV_K ** -0.5),
        'a_log': jnp.log(jax.random.uniform(ks[10], (DEPTH, 2, N_HEADS), f32, 1.0, 16.0)),
        'dt_bias': dt + jnp.log(-jnp.expm1(-dt)),
        'gdn_norm_g': gain(ks[11], (DEPTH, HEAD_DIM)),
        'sgu_ln_g': gain(ks[12], (DEPTH, D_SGU)),
        'sgu_ln_b': nrm(ks[13], (DEPTH, D_SGU), 0.02),
        'sgu_w': nrm(ks[14], (DEPTH, SGU_GROUPS, SGU_CHUNK, SGU_CHUNK), SGU_CHUNK ** -0.5),
        'sgu_b': gain(ks[15], (DEPTH, SGU_GROUPS, SGU_CHUNK)),
        'w_branch_a': nrm(ks[16], (DEPTH, D_GDN, D_MODEL), D_GDN ** -0.5),
        'w_branch_b': nrm(ks[17], (DEPTH, D_SGU, D_MODEL), D_SGU ** -0.5),
        'w_out': nrm(ks[18], (DEPTH, D_MODEL, D_MODEL), D_MODEL ** -0.5),
        'norm_ffn_g': gain(ks[19], (DEPTH, D_MODEL)),
        'router_group_w': nrm(ks[20], (DEPTH, D_MODEL, N_EXPERT_GROUPS), D_MODEL ** -0.5),
        'router_group_b': nrm(ks[21], (DEPTH, N_EXPERT_GROUPS), 0.01),
        'router_expert_w': nrm(ks[22], (DEPTH, D_MODEL, N_EXPERTS), D_MODEL ** -0.5),
        'router_expert_b': nrm(ks[23], (DEPTH, N_EXPERTS), 0.01),
        'expert_w1': nrm(ks[24], (DEPTH, N_EXPERTS, D_MODEL, D_EXPERT), D_MODEL ** -0.5),
        'expert_w3': nrm(ks[25], (DEPTH, N_EXPERTS, D_MODEL, D_EXPERT), D_MODEL ** -0.5),
        'expert_w2': nrm(ks[26], (DEPTH, N_EXPERTS, D_EXPERT, D_MODEL), D_EXPERT ** -0.5),
        'final_norm_g': gain(ks[27], (D_MODEL,)),
    }


def reference(x, c, ctx, c_ctx, ada_w, ada_b, norm_mix_g, w_in, conv_w, a_log, dt_bias, gdn_norm_g,
              sgu_ln_g, sgu_ln_b, sgu_w, sgu_b, w_branch_a, w_branch_b, w_out, norm_ffn_g,
              router_group_w, router_group_b, router_expert_w, router_expert_b,
              expert_w1, expert_w3, expert_w2, final_norm_g):
    B = x.shape[0]
    rows = x.shape[1] // GRID_W
    lat_chunks = rows // ROWS_PER_CHUNK
    ctx_chunks = ctx.shape[1] // SGU_CHUNK
    s_zero = jnp.zeros((2, B, N_HEADS, HEAD_DIM, HEAD_DIM), jnp.float32)
    for l in range(DEPTH):
        mod_x = adaln_params(c, ada_w[l], ada_b[l])
        mod_c = adaln_params(c_ctx[None, :], ada_w[l], ada_b[l])
        hc = modulate(rms_norm(ctx, norm_mix_g[l]), mod_c[:, 0], mod_c[:, 1])
        hx = modulate(rms_norm(x, norm_mix_g[l]), mod_x[:, 0], mod_x[:, 1])
        c_qkv, c_beta, c_a = jnp.split(hc @ w_in[l][:, :COL_Z], (COL_BETA, COL_A), axis=-1)
        o_c, s_ctx = gdn_scan(c_qkv, c_beta, c_a, conv_w[l], a_log[l], dt_bias[l], s_zero)
        x_qkv, x_beta, x_a, x_z, x_u, x_v, x_gates = jnp.split(
            hx @ w_in[l], (COL_BETA, COL_A, COL_Z, COL_U, COL_V, COL_GATE), axis=-1)
        o_x, _ = gdn_scan(x_qkv, x_beta, x_a, conv_w[l], a_log[l], dt_bias[l], s_ctx)
        y_gdn = gdn_output(o_x, x_z, gdn_norm_g[l])
        y_sgu = sgu_mixer(x_u, x_v, lat_chunks, sgu_ln_g[l], sgu_ln_b[l], sgu_w[l], sgu_b[l])
        x = x + mod_x[:, 2][:, None, :] * merge_branches(y_gdn, y_sgu, x_gates, w_branch_a[l], w_branch_b[l], w_out[l])
        hx = modulate(rms_norm(x, norm_ffn_g[l]), mod_x[:, 3], mod_x[:, 4])
        x = x + mod_x[:, 5][:, None, :] * hier_moe(hx, router_group_w[l], router_group_b[l], router_expert_w[l],
                                                   router_expert_b[l], expert_w1[l], expert_w3[l], expert_w2[l])
        if l < DEPTH - 1:
            c_z, c_u, c_v, c_gates = jnp.split(hc @ w_in[l][:, COL_Z:], (D_GDN, D_GDN + D_SGU, D_GDN + 2 * D_SGU), axis=-1)
            yc_gdn = gdn_output(o_c, c_z, gdn_norm_g[l])
            yc_sgu = sgu_mixer(c_u, c_v, ctx_chunks, sgu_ln_g[l], sgu_ln_b[l], sgu_w[l], sgu_b[l])
            ctx = ctx + mod_c[:, 2][:, None, :] * merge_branches(yc_gdn, yc_sgu, c_gates, w_branch_a[l], w_branch_b[l], w_out[l])
            hc = modulate(rms_norm(ctx, norm_ffn_g[l]), mod_c[:, 3], mod_c[:, 4])
            ctx = ctx + mod_c[:, 5][:, None, :] * hier_moe(hc, router_group_w[l], router_group_b[l], router_expert_w[l],
                                                           router_expert_b[l], expert_w1[l], expert_w3[l], expert_w2[l])
    return rms_norm(x, final_norm_g)
```

```python
import functools
import math

import jax
import jax.numpy as jnp
from jax import lax
from jax.experimental import pallas as pl
from jax.experimental.pallas import tpu as pltpu

F32 = jnp.float32
BF16 = jnp.bfloat16

D_MODEL = 1024
N_HEADS = 8
HEAD_DIM = 128
D_GDN = N_HEADS * HEAD_DIM
CONV_K = 5
GDN_CHUNK = 64
SGU_GROUPS = 8
SGU_GROUP_DIM = 128
D_SGU = SGU_GROUPS * SGU_GROUP_DIM
SGU_CHUNK = 128
N_EXPERT_GROUPS = 4
EXPERTS_PER_GROUP = 8
N_EXPERTS = N_EXPERT_GROUPS * EXPERTS_PER_GROUP
D_EXPERT = 256
N_MOD = 6
EPS = 1e-6
COL_BETA = 3 * D_GDN
COL_Z = COL_BETA + 4 * N_HEADS

LANES = 128
ROW_TILE = 512
PREP_TILE = 256
MOE_TILE = 1024
MOE_EXPERTS_PER_STEP = 4
VMEM_LIMIT = 48 * 1024 * 1024
NEG_BIG = -1e30

PCOL_Q, PCOL_K, PCOL_V, PCOL_Z, PCOL_U, PCOL_SV, PCOL_GA, PCOL_GB = range(8)


def _mm(a, b):
    return jnp.dot(a.astype(BF16), b.astype(BF16), preferred_element_type=F32)


def _mm_nt(a, b):
    return lax.dot_general(a.astype(BF16), b.astype(BF16), (((1,), (1,)), ((), ())),
                           preferred_element_type=F32)


def _sigmoid(x):
    return 1.0 / (1.0 + jnp.exp(-x))


def _silu(x):
    return x * _sigmoid(x)


def _gelu_tanh(x):
    return 0.5 * x * (1.0 + jnp.tanh(math.sqrt(2.0 / math.pi) * (x + 0.044715 * (x * x * x))))


def _params(*sem):
    return pltpu.CompilerParams(dimension_semantics=sem, vmem_limit_bytes=VMEM_LIMIT)


def _adaln_kernel(c_ref, w_ref, b_ref, o_ref):
    o_ref[...] = _mm(_silu(c_ref[...]), w_ref[...]) + b_ref[...]


def _adaln(cond, w, b):
    n = w.shape[1]
    tn = 1536
    return pl.pallas_call(
        _adaln_kernel,
        grid=(n // tn,),
        in_specs=[pl.BlockSpec((8, D_MODEL), lambda j: (0, 0)),
                  pl.BlockSpec((D_MODEL, tn), lambda j: (0, j)),
                  pl.BlockSpec((1, tn), lambda j: (0, j))],
        out_specs=pl.BlockSpec((8, tn), lambda j: (0, j)),
        out_shape=jax.ShapeDtypeStruct((8, n), F32),
        compiler_params=_params("arbitrary"),
        name="adaln",
    )(cond, w, b.reshape(1, n))


def _inproj_kernel(x_ref, ctx_ref, g_ref, shift_ref, scale_ref, wm_ref, ws_ref, lng_ref, lnb_ref,
                   p_ref, small_ref, h_ref):
    i = pl.program_id(0)
    j = pl.program_id(1)

    def norm_mod(xv):
        y = xv * lax.rsqrt(jnp.mean(xv * xv, axis=-1, keepdims=True) + EPS) * g_ref[...]
        return y * (1.0 + scale_ref[0]) + shift_ref[0]

    @pl.when(j == 0)
    def _():
        @pl.when(i == 0)
        def _():
            h_ref[...] = norm_mod(ctx_ref[...]).astype(BF16)

        @pl.when(i > 0)
        def _():
            h_ref[...] = norm_mod(x_ref[...]).astype(BF16)

        small_ref[...] = jnp.dot(h_ref[...], ws_ref[...], preferred_element_type=F32)

    def proj():
        return jnp.dot(h_ref[...], wm_ref[...], preferred_element_type=F32)

    @pl.when(j <= PCOL_V)
    def _():
        p_ref[...] = proj().astype(BF16)

    @pl.when(j == PCOL_Z)
    def _():
        p_ref[...] = _silu(proj()).astype(BF16)

    @pl.when(j == PCOL_U)
    def _():
        p_ref[...] = _gelu_tanh(proj()).astype(BF16)

    @pl.when(j == PCOL_SV)
    def _():
        a = _gelu_tanh(proj())
        mu = jnp.mean(a, axis=-1, keepdims=True)
        ac = a - mu
        var = jnp.mean(ac * ac, axis=-1, keepdims=True)
        p_ref[...] = (ac * lax.rsqrt(var + EPS) * lng_ref[...] + lnb_ref[...]).astype(BF16)

    @pl.when(j >= PCOL_GA)
    def _():
        p_ref[...] = _sigmoid(proj()).astype(BF16)


def _inproj(x2, ctx2, norm_g, shift3, scale3, w_main, w_small, ln_g, ln_b, tiles_per_batch):
    t_lat = x2.shape[0]
    n_lat = t_lat // ROW_TILE
    n_tiles = n_lat + 1
    n_col = w_main.shape[1] // D_MODEL
    n_batch = shift3.shape[0] - 1
    sel = lambda i: jnp.where(i == 0, n_batch, (jnp.maximum(i, 1) - 1) // tiles_per_batch)
    vec = lambda: pl.BlockSpec((1, D_MODEL), lambda i, j: (0, 0))
    return pl.pallas_call(
        _inproj_kernel,
        grid=(n_tiles, n_col),
        in_specs=[pl.BlockSpec((ROW_TILE, D_MODEL), lambda i, j: (jnp.maximum(i, 1) - 1, 0)),
                  pl.BlockSpec((ROW_TILE, D_MODEL), lambda i, j: (0, 0)),
                  vec(),
                  pl.BlockSpec((1, 1, D_MODEL), lambda i, j: (sel(i), 0, 0)),
                  pl.BlockSpec((1, 1, D_MODEL), lambda i, j: (sel(i), 0, 0)),
                  pl.BlockSpec((D_MODEL, D_MODEL), lambda i, j: (0, j)),
                  pl.BlockSpec((D_MODEL, LANES), lambda i, j: (0, 0)),
                  vec(), vec()],
        out_specs=[pl.BlockSpec((ROW_TILE, D_MODEL), lambda i, j: (i, j)),
                   pl.BlockSpec((ROW_TILE, LANES), lambda i, j: (i, 0))],
        out_shape=[jax.ShapeDtypeStruct((n_tiles * ROW_TILE, n_col * D_MODEL), BF16),
                   jax.ShapeDtypeStruct((n_tiles * ROW_TILE, LANES), F32)],
        scratch_shapes=[pltpu.VMEM((ROW_TILE, D_MODEL), BF16)],
        compiler_params=_params("arbitrary", "arbitrary"),
        name="inproj",
    )(x2, ctx2, norm_g, shift3, scale3, w_main, w_small, ln_g, ln_b)


def _prep_kernel(pm_ref, pp_ref, pn_ref, cw_ref, small_ref, nega_ref, dtb_ref, qkv_ref, gb_ref, ext_ref,
                 *, first_tiles, last_tiles):
    r = pl.program_id(0)
    j = pl.program_id(1)
    tr = pm_ref.shape[0]
    is_first = functools.reduce(jnp.logical_or, [r == t for t in first_tiles])
    is_last = functools.reduce(jnp.logical_or, [r == t for t in last_tiles])
    keep_prev = jnp.where(is_first, 0.0, 1.0)
    keep_next = jnp.where(is_last, 0.0, 1.0)

    ext_ref[8:8 + tr, :] = pm_ref[...].astype(F32)
    ext_ref[0:8, :] = pp_ref[...].astype(F32)[8:16, :] * keep_prev
    ext_ref[8 + tr:16 + tr, :] = pn_ref[...].astype(F32)[0:8, :] * keep_next

    base = 8 - CONV_K // 2
    acc = cw_ref[0:1, :] * ext_ref[base:base + tr, :]
    for k in range(1, CONV_K):
        acc = acc + cw_ref[k:k + 1, :] * ext_ref[base + k:base + k + tr, :]
    y = _silu(acc)

    def head_l2(scale):
        for h in range(N_HEADS):
            yh = y[:, h * HEAD_DIM:(h + 1) * HEAD_DIM]
            inv = lax.rsqrt(jnp.sum(yh * yh, axis=-1, keepdims=True) + EPS) * scale
            qkv_ref[:, h * HEAD_DIM:(h + 1) * HEAD_DIM] = (yh * inv).astype(BF16)

    @pl.when(j == PCOL_Q)
    def _():
        head_l2(HEAD_DIM ** -0.5)

    @pl.when(j == PCOL_K)
    def _():
        head_l2(1.0)

    @pl.when(j == PCOL_V)
    def _():
        qkv_ref[...] = y.astype(BF16)

    @pl.when(j == 0)
    def _():
        s = small_ref[...]
        lane = lax.broadcasted_iota(jnp.int32, s.shape, 1)
        beta = _sigmoid(s)
        z = s + dtb_ref[...]
        softplus = jnp.maximum(z, 0.0) + jnp.log(1.0 + jnp.exp(-jnp.abs(z)))
        g = nega_ref[...] * softplus
        gb_ref[...] = jnp.where(lane < 2 * N_HEADS, beta, jnp.where(lane < 4 * N_HEADS, g, 0.0))


def _prep(p, small, conv_w8, nega, dtb, first_tiles, last_tiles):
    ta = p.shape[0]
    n_tiles = ta // PREP_TILE
    sub = PREP_TILE // 16
    n_sub = ta // 16
    kern = functools.partial(_prep_kernel, first_tiles=first_tiles, last_tiles=last_tiles)
    return pl.pallas_call(
        kern,
        grid=(n_tiles, 3),
        in_specs=[pl.BlockSpec((PREP_TILE, D_MODEL), lambda r, j: (r, j)),
                  pl.BlockSpec((16, D_MODEL), lambda r, j: (jnp.maximum(r * sub - 1, 0), j)),
                  pl.BlockSpec((16, D_MODEL), lambda r, j: (jnp.minimum((r + 1) * sub, n_sub - 1), j)),
                  pl.BlockSpec((8, D_MODEL), lambda r, j: (0, j)),
                  pl.BlockSpec((PREP_TILE, LANES), lambda r, j: (r, 0)),
                  pl.BlockSpec((1, LANES), lambda r, j: (0, 0)),
                  pl.BlockSpec((1, LANES), lambda r, j: (0, 0))],
        out_specs=[pl.BlockSpec((PREP_TILE, D_MODEL), lambda r, j: (r, j)),
                   pl.BlockSpec((PREP_TILE, LANES), lambda r, j: (r, 0))],
        out_shape=[jax.ShapeDtypeStruct((ta, 3 * D_MODEL), BF16),
                   jax.ShapeDtypeStruct((ta, LANES), F32)],
        scratch_shapes=[pltpu.VMEM((PREP_TILE + 16, D_MODEL), F32)],
        compiler_params=_params("arbitrary", "arbitrary"),
        name="prep",
    )(p, p, p, conv_w8, small, nega, dtb)


def _gdn_direction(d, q_ref, k_ref, v_ref, gb_ref, o_ref, s_ref):
    c = GDN_CHUNK
    row = lax.broadcasted_iota(jnp.int32, (c, c), 0)
    col = lax.broadcasted_iota(jnp.int32, (c, c), 1)
    if d == 0:
        incl, strict = row >= col, row > col
    else:
        incl, strict = row <= col, row < col
    eye = jnp.where(row == col, 1.0, 0.0)

    gb = gb_ref[...]
    lane = lax.broadcasted_iota(jnp.int32, gb.shape, 1)
    g_only = jnp.where((lane >= 2 * N_HEADS) & (lane < 4 * N_HEADS), gb, 0.0)
    tri = jnp.where(incl, 1.0, 0.0).astype(BF16)
    g_hi = g_only.astype(BF16)
    g_r1 = g_only - g_hi.astype(F32)
    g_mid = g_r1.astype(BF16)
    g_lo = (g_r1 - g_mid.astype(F32)).astype(BF16)
    dot = lambda a, b: jnp.dot(a, b, preferred_element_type=F32)
    gcum = dot(tri, g_hi) + dot(tri, g_mid) + dot(tri, g_lo)
    gcum_t = gcum.T
    g_end = gcum[c - 1:c, :] if d == 0 else gcum[0:1, :]
    exp_g = jnp.exp(gcum)
    exp_rest = jnp.exp(g_end - gcum)
    exp_end = jnp.exp(g_end)

    for h in range(N_HEADS):
        cb = d * N_HEADS + h
        cg = 2 * N_HEADS + cb
        hs = slice(h * HEAD_DIM, (h + 1) * HEAD_DIM)
        q = q_ref[:, hs].astype(F32)
        k = k_ref[:, hs].astype(F32)
        v = v_ref[:, hs].astype(F32)
        beta = gb[:, cb:cb + 1]
        diff = gcum[:, cg:cg + 1] - gcum_t[cg:cg + 1, :]
        decay = jnp.where(incl, jnp.exp(jnp.minimum(diff, 0.0)), 0.0)
        kb = k * beta
        x = jnp.where(strict, -(_mm_nt(kb, k) * decay), 0.0)
        t = eye + x
        xp = x
        for _ in range(5):
            xp = _mm(xp, xp)
            t = t + _mm(t, xp)
        eg = exp_g[:, cg:cg + 1]
        uw = _mm(t, jnp.concatenate([v * beta, kb * eg], axis=-1))
        u = uw[:, :HEAD_DIM]
        w = uw[:, HEAD_DIM:]
        qk = _mm_nt(q, k) * decay
        s = s_ref[d, h]
        s_bf = s.astype(BF16)
        v_new = u - _mm(w, s_bf)
        o = _mm(q * eg, s_bf) + _mm(qk, v_new)
        k_dec_t = (k * exp_rest[:, cg:cg + 1]).T
        s_ref[d, h] = s * exp_end[:, cg:cg + 1] + _mm(k_dec_t, v_new)
        o_ref[:, hs] = o.astype(BF16)


def _gdn_kernel(qf, kf, vf, gbf, qb, kb, vb, gbb, of_ref, ob_ref, s_ref):
    @pl.when(pl.program_id(1) == 0)
    def _():
        s_ref[...] = jnp.zeros_like(s_ref)

    _gdn_direction(0, qf, kf, vf, gbf, of_ref, s_ref)
    _gdn_direction(1, qb, kb, vb, gbb, ob_ref, s_ref)


def _gdn(qkv, gb, batch, ctx_chunks, lat_chunks):
    ta = qkv.shape[0]
    n_steps = ctx_chunks + lat_chunks
    lat0 = batch * ctx_chunks

    def fwd_blk(b, s):
        return jnp.where(s < ctx_chunks, b * ctx_chunks + s, lat0 + b * lat_chunks + (s - ctx_chunks))

    def bwd_blk(b, s):
        return jnp.where(s < ctx_chunks, b * ctx_chunks + (ctx_chunks - 1 - s),
                         lat0 + b * lat_chunks + (lat_chunks - 1 - (s - ctx_chunks)))

    def specs(blk):
        col = lambda j: pl.BlockSpec((GDN_CHUNK, D_MODEL), lambda b, s: (blk(b, s), j))
        return [col(0), col(1), col(2), pl.BlockSpec((GDN_CHUNK, LANES), lambda b, s: (blk(b, s), 0))]

    out = lambda blk: pl.BlockSpec((GDN_CHUNK, D_MODEL), lambda b, s: (blk(b, s), 0))
    return pl.pallas_call(
        _gdn_kernel,
        grid=(batch, n_steps),
        in_specs=specs(fwd_blk) + specs(bwd_blk),
        out_specs=[out(fwd_blk), out(bwd_blk)],
        out_shape=[jax.ShapeDtypeStruct((ta, D_MODEL), BF16)] * 2,
        scratch_shapes=[pltpu.VMEM((2, N_HEADS, HEAD_DIM, HEAD_DIM), F32)],
        compiler_params=_params("arbitrary", "arbitrary"),
        name="gdn",
    )(qkv, qkv, qkv, gb, qkv, qkv, qkv, gb)


def _sgu_kernel(u_ref, v_ref, w_ref, b_ref, o_ref):
    for ch in range(u_ref.shape[0] // SGU_CHUNK):
        rs = slice(ch * SGU_CHUNK, (ch + 1) * SGU_CHUNK)
        for g in range(SGU_GROUPS):
            cs = slice(g * SGU_GROUP_DIM, (g + 1) * SGU_GROUP_DIM)
            mixed = jnp.dot(w_ref[g], v_ref[rs, cs], preferred_element_type=F32) + b_ref[:, cs]
            o_ref[rs, cs] = (u_ref[rs, cs].astype(F32) * mixed).astype(BF16)


def _sgu(p, w_s, b_full, lat_tile0, n_lat_tiles):
    return pl.pallas_call(
        _sgu_kernel,
        grid=(n_lat_tiles,),
        in_specs=[pl.BlockSpec((ROW_TILE, D_MODEL), lambda i: (i + lat_tile0, PCOL_U)),
                  pl.BlockSpec((ROW_TILE, D_MODEL), lambda i: (i + lat_tile0, PCOL_SV)),
                  pl.BlockSpec((SGU_GROUPS, SGU_CHUNK, SGU_CHUNK), lambda i: (0, 0, 0)),
                  pl.BlockSpec((SGU_CHUNK, D_MODEL), lambda i: (0, 0))],
        out_specs=pl.BlockSpec((ROW_TILE, D_MODEL), lambda i: (i, 0)),
        out_shape=jax.ShapeDtypeStruct((n_lat_tiles * ROW_TILE, D_MODEL), BF16),
        compiler_params=_params("arbitrary"),
        name="sgu",
    )(p, p, w_s, b_full)


def _merge_kernel(of_ref, ob_ref, z_ref, ysgu_ref, ga_ref, gb_ref, x_ref, wa_ref, wb_ref, wo_ref,
                  gng_ref, gate_ref, nfg_ref, shift_ref, scale_ref, wr_ref, br_ref,
                  xmid_ref, h2_ref, rgate_ref, ygdn_ref):
    o = of_ref[...].astype(F32) + ob_ref[...].astype(F32)
    for h in range(N_HEADS):
        hs = slice(h * HEAD_DIM, (h + 1) * HEAD_DIM)
        oh = o[:, hs]
        inv = lax.rsqrt(jnp.mean(oh * oh, axis=-1, keepdims=True) + EPS)
        ygdn_ref[:, hs] = (oh * inv * gng_ref[...] * z_ref[:, hs].astype(F32)).astype(BF16)
    ya = jnp.dot(ygdn_ref[...], wa_ref[...], preferred_element_type=F32)
    yb = jnp.dot(ysgu_ref[...], wb_ref[...], preferred_element_type=F32)
    merged = ga_ref[...].astype(F32) * ya + gb_ref[...].astype(F32) * yb
    mix = jnp.dot(merged.astype(BF16), wo_ref[...], preferred_element_type=F32)
    xm = x_ref[...] + gate_ref[0] * mix
    xmid_ref[...] = xm
    hn = xm * lax.rsqrt(jnp.mean(xm * xm, axis=-1, keepdims=True) + EPS) * nfg_ref[...]
    h2 = hn * (1.0 + scale_ref[0]) + shift_ref[0]
    h2_ref[...] = h2.astype(BF16)

    h_hi = h2.astype(BF16)
    h_lo = (h2 - h_hi.astype(F32)).astype(BF16)
    wr = wr_ref[...]
    w_hi = wr.astype(BF16)
    w_lo = (wr - w_hi.astype(F32)).astype(BF16)
    dot = lambda a, b: jnp.dot(a, b, preferred_element_type=F32)
    lg = dot(h_hi, w_hi) + dot(h_lo, w_hi) + dot(h_hi, w_lo) + br_ref[...]

    lane = lax.broadcasted_iota(jnp.int32, lg.shape, 1).astype(F32)
    far = float(LANES)
    gl = jnp.where(lane < N_EXPERT_GROUPS, lg, NEG_BIG)
    gmax = jnp.max(gl, axis=-1, keepdims=True)
    p_g = 1.0 / jnp.sum(jnp.exp(gl - gmax), axis=-1, keepdims=True)
    grp = jnp.min(jnp.where(gl == gmax, lane, far), axis=-1, keepdims=True)
    lo = N_EXPERT_GROUPS + EXPERTS_PER_GROUP * grp
    in_grp = jnp.where(lane >= lo, jnp.where(lane < lo + EXPERTS_PER_GROUP, 1.0, 0.0), 0.0)
    el = jnp.where(in_grp > 0.0, lg, NEG_BIG)
    m1 = jnp.max(el, axis=-1, keepdims=True)
    i1 = jnp.min(jnp.where(el == m1, lane, far), axis=-1, keepdims=True)
    el2 = jnp.where(lane == i1, NEG_BIG, el)
    m2 = jnp.max(el2, axis=-1, keepdims=True)
    i2 = jnp.min(jnp.where(el2 == m2, lane, far), axis=-1, keepdims=True)
    t = jnp.exp(m2 - m1)
    w1 = p_g / (1.0 + t)
    w2 = w1 * t
    rgate_ref[...] = jnp.where(lane == i1, w1, 0.0) + jnp.where(lane == i2, w2, 0.0)


def _merge(o_f, o_b, p, y_sgu, x2, w_a, w_b, w_o, gng, gate3, nfg, shift3, scale3, w_r, b_r,
           lat_tile0, tiles_per_batch):
    t_lat = x2.shape[0]
    n_tiles = t_lat // ROW_TILE
    lat = lambda c: pl.BlockSpec((ROW_TILE, D_MODEL), lambda i: (i + lat_tile0, c))
    own = lambda: pl.BlockSpec((ROW_TILE, D_MODEL), lambda i: (i, 0))
    mat = lambda: pl.BlockSpec((D_MODEL, D_MODEL), lambda i: (0, 0))
    vec = lambda: pl.BlockSpec((1, D_MODEL), lambda i: (0, 0))
    per_b = lambda: pl.BlockSpec((1, 1, D_MODEL), lambda i: (i // tiles_per_batch, 0, 0))
    return pl.pallas_call(
        _merge_kernel,
        grid=(n_tiles,),
        in_specs=[lat(0), lat(0), lat(PCOL_Z), own(), lat(PCOL_GA), lat(PCOL_GB), own(),
                  mat(), mat(), mat(), pl.BlockSpec((1, HEAD_DIM), lambda i: (0, 0)),
                  per_b(), vec(), per_b(), per_b(),
                  pl.BlockSpec((D_MODEL, LANES), lambda i: (0, 0)),
                  pl.BlockSpec((1, LANES), lambda i: (0, 0))],
        out_specs=[own(), own(), pl.BlockSpec((ROW_TILE, LANES), lambda i: (i, 0))],
        out_shape=[jax.ShapeDtypeStruct((t_lat, D_MODEL), F32),
                   jax.ShapeDtypeStruct((t_lat, D_MODEL), BF16),
                   jax.ShapeDtypeStruct((t_lat, LANES), F32)],
        scratch_shapes=[pltpu.VMEM((ROW_TILE, D_MODEL), BF16)],
        compiler_params=_params("arbitrary"),
        name="merge",
    )(o_f, o_b, p, y_sgu, p, p, x2, w_a, w_b, w_o, gng, gate3, nfg, shift3, scale3, w_r, b_r)


def _moe_kernel(h_ref, rg_ref, w1_ref, w3_ref, w2_ref, xmid_ref, gate_ref, fng_ref, o_ref, acc_ref):
    q = pl.program_id(1)

    @pl.when(q == 0)
    def _():
        acc_ref[...] = jnp.zeros_like(acc_ref)

    h = h_ref[...]
    rg = rg_ref[...]
    lane = lax.broadcasted_iota(jnp.int32, rg.shape, 1)
    for e in range(MOE_EXPERTS_PER_STEP):
        col = N_EXPERT_GROUPS + q * MOE_EXPERTS_PER_STEP + e
        gate = jnp.sum(jnp.where(lane == col, rg, 0.0), axis=-1, keepdims=True)
        a = jnp.dot(h, w1_ref[e], preferred_element_type=F32)
        b = jnp.dot(h, w3_ref[e], preferred_element_type=F32)
        hid = _silu(a) * b * gate
        acc_ref[...] += jnp.dot(hid.astype(BF16), w2_ref[e], preferred_element_type=F32)

    @pl.when(q == pl.num_programs(1) - 1)
    def _():
        xo = xmid_ref[...] + gate_ref[0] * acc_ref[...]
        o_ref[...] = xo * lax.rsqrt(jnp.mean(xo * xo, axis=-1, keepdims=True) + EPS) * fng_ref[...]


def _moe(h2, rgate, w1, w3, w2, xmid, gate3, fng, tiles_per_batch):
    t_lat = h2.shape[0]
    n_tiles = t_lat // MOE_TILE
    eps = MOE_EXPERTS_PER_STEP
    row = lambda w: pl.BlockSpec((MOE_TILE, w), lambda i, q: (i, 0))
    return pl.pallas_call(
        _moe_kernel,
        grid=(n_tiles, N_EXPERTS // eps),
        in_specs=[row(D_MODEL), row(LANES),
                  pl.BlockSpec((eps, D_MODEL, D_EXPERT), lambda i, q: (q, 0, 0)),
                  pl.BlockSpec((eps, D_MODEL, D_EXPERT), lambda i, q: (q, 0, 0)),
                  pl.BlockSpec((eps, D_EXPERT, D_MODEL), lambda i, q: (q, 0, 0)),
                  row(D_MODEL),
                  pl.BlockSpec((1, 1, D_MODEL), lambda i, q: (i // tiles_per_batch, 0, 0)),
                  pl.BlockSpec((1, D_MODEL), lambda i, q: (0, 0))],
        out_specs=row(D_MODEL),
        out_shape=jax.ShapeDtypeStruct((t_lat, D_MODEL), F32),
        scratch_shapes=[pltpu.VMEM((MOE_TILE, D_MODEL), F32)],
        compiler_params=_params("arbitrary", "arbitrary"),
        name="moe",
    )(h2, rgate, w1, w3, w2, xmid, gate3, fng)


def kernel(x, c, ctx, c_ctx, ada_w, ada_b, norm_mix_g, w_in, conv_w, a_log, dt_bias, gdn_norm_g, sgu_ln_g, sgu_ln_b, sgu_w, sgu_b, w_branch_a, w_branch_b, w_out, norm_ffn_g, router_group_w, router_group_b, router_expert_w, router_expert_b, expert_w1, expert_w3, expert_w2, final_norm_g):
    batch, seq, d = x.shape
    ctx_len = ctx.shape[1]
    assert d == D_MODEL and ada_w.shape[0] == 1, "single-layer block with D_MODEL channels"
    assert batch * ctx_len == ROW_TILE, "context rows of all samples form one row tile"
    assert seq % MOE_TILE == 0 and ctx_len % PREP_TILE == 0 and batch + 1 <= 8
    t_lat = batch * seq
    row = lambda v: v.reshape(1, -1).astype(F32)

    cond = jnp.zeros((8, d), F32).at[:batch].set(c).at[batch].set(c_ctx)
    mod = _adaln(cond, ada_w[0], ada_b[0]).reshape(8, N_MOD, d)
    mod_row = lambda k: mod[:batch + 1, k].reshape(batch + 1, 1, d)

    w_l = w_in[0]
    w_main = jnp.concatenate([w_l[:, :COL_BETA], w_l[:, COL_Z:]], axis=1).astype(BF16)
    w_small = jnp.zeros((d, LANES), BF16).at[:, :4 * N_HEADS].set(w_l[:, COL_BETA:COL_Z].astype(BF16))
    x2 = x.reshape(t_lat, d)
    tiles_per_batch = seq // ROW_TILE
    p, small = _inproj(x2, ctx.reshape(batch * ctx_len, d), row(norm_mix_g), mod_row(0), mod_row(1),
                       w_main, w_small, row(sgu_ln_g), row(sgu_ln_b), tiles_per_batch)

    ctx_t, lat_t = ctx_len // PREP_TILE, seq // PREP_TILE
    starts = [b * ctx_t for b in range(batch)] + [batch * ctx_t + b * lat_t for b in range(batch)]
    ends = [(b + 1) * ctx_t - 1 for b in range(batch)] + [batch * ctx_t + (b + 1) * lat_t - 1 for b in range(batch)]
    conv_w8 = jnp.zeros((8, 3 * D_GDN), F32).at[:CONV_K].set(conv_w[0])
    pad_lanes = lambda v: jnp.zeros((1, LANES), F32).at[0, 2 * N_HEADS:4 * N_HEADS].set(v.reshape(-1))
    qkv, gb = _prep(p, small, conv_w8, pad_lanes(-jnp.exp(a_log[0])), pad_lanes(dt_bias[0]),
                    tuple(starts), tuple(ends))

    o_f, o_b = _gdn(qkv, gb, batch, ctx_len // GDN_CHUNK, seq // GDN_CHUNK)

    b_full = jnp.repeat(sgu_b[0].T, SGU_GROUP_DIM, axis=1).astype(F32)
    y_sgu = _sgu(p, sgu_w[0].astype(BF16), b_full, 1, t_lat // ROW_TILE)

    w_r = jnp.zeros((d, LANES), F32).at[:, :N_EXPERT_GROUPS].set(router_group_w[0]) \
        .at[:, N_EXPERT_GROUPS:N_EXPERT_GROUPS + N_EXPERTS].set(router_expert_w[0])
    b_r = jnp.zeros((1, LANES), F32).at[0, :N_EXPERT_GROUPS].set(router_group_b[0]) \
        .at[0, N_EXPERT_GROUPS:N_EXPERT_GROUPS + N_EXPERTS].set(router_expert_b[0])
    gng = gdn_norm_g[0].reshape(1, HEAD_DIM).astype(F32)
    xmid, h2, rgate = _merge(o_f, o_b, p, y_sgu, x2, w_branch_a[0].astype(BF16), w_branch_b[0].astype(BF16),
                             w_out[0].astype(BF16), gng, mod_row(2)[:batch], row(norm_ffn_g),
                             mod_row(3)[:batch], mod_row(4)[:batch], w_r, b_r, 1, tiles_per_batch)

    out = _moe(h2, rgate, expert_w1[0].astype(BF16), expert_w3[0].astype(BF16), expert_w2[0].astype(BF16),
               xmid, mod_row(5)[:batch], row(final_norm_g), seq // MOE_TILE)
    return out.reshape(batch, seq, d)
```

```python
import functools
import math

import jax
import jax.numpy as jnp
from jax import lax
from jax.experimental import pallas as pl
from jax.experimental.pallas import tpu as pltpu

F32 = jnp.float32
BF16 = jnp.bfloat16

D_MODEL = 1024
N_HEADS = 8
HEAD_DIM = 128
D_GDN = N_HEADS * HEAD_DIM
CONV_K = 5
GDN_CHUNK = 64
SGU_GROUPS = 8
SGU_GROUP_DIM = 128
D_SGU = SGU_GROUPS * SGU_GROUP_DIM
SGU_CHUNK = 128
N_EXPERT_GROUPS = 4
EXPERTS_PER_GROUP = 8
N_EXPERTS = N_EXPERT_GROUPS * EXPERTS_PER_GROUP
D_EXPERT = 256
N_MOD = 6
EPS = 1e-6
COL_BETA = 3 * D_GDN
COL_Z = COL_BETA + 4 * N_HEADS

LANES = 128
ROW_TILE = 512
PREP_TILE = 256
MOE_TILE = 1024
MOE_EXPERTS_PER_STEP = 4
VMEM_LIMIT = 48 * 1024 * 1024
NEG_BIG = -1e30

PCOL_Q, PCOL_K, PCOL_V, PCOL_Z, PCOL_U, PCOL_SV, PCOL_GA, PCOL_GB = range(8)


def _mm(a, b):
    return jnp.dot(a.astype(BF16), b.astype(BF16), preferred_element_type=F32)


def _mm_nt(a, b):
    return lax.dot_general(a.astype(BF16), b.astype(BF16), (((1,), (1,)), ((), ())),
                           preferred_element_type=F32)


def _sigmoid(x):
    return 1.0 / (1.0 + jnp.exp(-x))


def _silu(x):
    return x * _sigmoid(x)


def _gelu_tanh(x):
    return 0.5 * x * (1.0 + jnp.tanh(math.sqrt(2.0 / math.pi) * (x + 0.044715 * (x * x * x))))


def _params(*sem):
    return pltpu.CompilerParams(dimension_semantics=sem, vmem_limit_bytes=VMEM_LIMIT)


def _adaln_kernel(c_ref, w_ref, b_ref, o_ref):
    o_ref[...] = _mm(_silu(c_ref[...]), w_ref[...]) + b_ref[...]


def _adaln(cond, w, b):
    n = w.shape[1]
    tn = 1536
    return pl.pallas_call(
        _adaln_kernel,
        grid=(n // tn,),
        in_specs=[pl.BlockSpec((8, D_MODEL), lambda j: (0, 0)),
                  pl.BlockSpec((D_MODEL, tn), lambda j: (0, j)),
                  pl.BlockSpec((1, tn), lambda j: (0, j))],
        out_specs=pl.BlockSpec((8, tn), lambda j: (0, j)),
        out_shape=jax.ShapeDtypeStruct((8, n), F32),
        compiler_params=_params("arbitrary"),
        name="adaln",
    )(cond, w, b.reshape(1, n))


def _inproj_kernel(x_ref, ctx_ref, g_ref, shift_ref, scale_ref, wm_ref, ws_ref, lng_ref, lnb_ref,
                   p_ref, small_ref, h_ref):
    i = pl.program_id(0)
    j = pl.program_id(1)

    def norm_mod(xv):
        y = xv * lax.rsqrt(jnp.mean(xv * xv, axis=-1, keepdims=True) + EPS) * g_ref[...]
        return y * (1.0 + scale_ref[0]) + shift_ref[0]

    @pl.when(j == 0)
    def _():
        @pl.when(i == 0)
        def _():
            h_ref[...] = norm_mod(ctx_ref[...]).astype(BF16)

        @pl.when(i > 0)
        def _():
            h_ref[...] = norm_mod(x_ref[...]).astype(BF16)

        small_ref[...] = jnp.dot(h_ref[...], ws_ref[...], preferred_element_type=F32)

    def proj():
        return jnp.dot(h_ref[...], wm_ref[...], preferred_element_type=F32)

    @pl.when(j <= PCOL_V)
    def _():
        p_ref[...] = proj().astype(BF16)

    @pl.when(j == PCOL_Z)
    def _():
        p_ref[...] = _silu(proj()).astype(BF16)

    @pl.when(j == PCOL_U)
    def _():
        p_ref[...] = _gelu_tanh(proj()).astype(BF16)

    @pl.when(j == PCOL_SV)
    def _():
        a = _gelu_tanh(proj())
        mu = jnp.mean(a, axis=-1, keepdims=True)
        ac = a - mu
        var = jnp.mean(ac * ac, axis=-1, keepdims=True)
        p_ref[...] = (ac * lax.rsqrt(var + EPS) * lng_ref[...] + lnb_ref[...]).astype(BF16)

    @pl.when(j >= PCOL_GA)
    def _():
        p_ref[...] = _sigmoid(proj()).astype(BF16)


def _inproj(x2, ctx2, norm_g, shift3, scale3, w_main, w_small, ln_g, ln_b, tiles_per_batch):
    t_lat = x2.shape[0]
    n_lat = t_lat // ROW_TILE
    n_tiles = n_lat + 1
    n_col = w_main.shape[1] // D_MODEL
    n_batch = shift3.shape[0] - 1
    sel = lambda i: jnp.where(i == 0, n_batch, (jnp.maximum(i, 1) - 1) // tiles_per_batch)
    vec = lambda: pl.BlockSpec((1, D_MODEL), lambda i, j: (0, 0))
    return pl.pallas_call(
        _inproj_kernel,
        grid=(n_tiles, n_col),
        in_specs=[pl.BlockSpec((ROW_TILE, D_MODEL), lambda i, j: (jnp.maximum(i, 1) - 1, 0)),
                  pl.BlockSpec((ROW_TILE, D_MODEL), lambda i, j: (0, 0)),
                  vec(),
                  pl.BlockSpec((1, 1, D_MODEL), lambda i, j: (sel(i), 0, 0)),
                  pl.BlockSpec((1, 1, D_MODEL), lambda i, j: (sel(i), 0, 0)),
                  pl.BlockSpec((D_MODEL, D_MODEL), lambda i, j: (0, j)),
                  pl.BlockSpec((D_MODEL, LANES), lambda i, j: (0, 0)),
                  vec(), vec()],
        out_specs=[pl.BlockSpec((ROW_TILE, D_MODEL), lambda i, j: (i, j)),
                   pl.BlockSpec((ROW_TILE, LANES), lambda i, j: (i, 0))],
        out_shape=[jax.ShapeDtypeStruct((n_tiles * ROW_TILE, n_col * D_MODEL), BF16),
                   jax.ShapeDtypeStruct((n_tiles * ROW_TILE, LANES), F32)],
        scratch_shapes=[pltpu.VMEM((ROW_TILE, D_MODEL), BF16)],
        compiler_params=_params("arbitrary", "arbitrary"),
        name="inproj",
    )(x2, ctx2, norm_g, shift3, scale3, w_main, w_small, ln_g, ln_b)


def _prep_kernel(pm_ref, pp_ref, pn_ref, cw_ref, small_ref, nega_ref, dtb_ref, qkv_ref, gb_ref, ext_ref,
                 *, first_tiles, last_tiles):
    r = pl.program_id(0)
    j = pl.program_id(1)
    tr = pm_ref.shape[0]
    is_first = functools.reduce(jnp.logical_or, [r == t for t in first_tiles])
    is_last = functools.reduce(jnp.logical_or, [r == t for t in last_tiles])
    keep_prev = jnp.where(is_first, 0.0, 1.0)
    keep_next = jnp.where(is_last, 0.0, 1.0)

    ext_ref[8:8 + tr, :] = pm_ref[...].astype(F32)
    ext_ref[0:8, :] = pp_ref[...].astype(F32)[8:16, :] * keep_prev
    ext_ref[8 + tr:16 + tr, :] = pn_ref[...].astype(F32)[0:8, :] * keep_next

    base = 8 - CONV_K // 2
    acc = cw_ref[0:1, :] * ext_ref[base:base + tr, :]
    for k in range(1, CONV_K):
        acc = acc + cw_ref[k:k + 1, :] * ext_ref[base + k:base + k + tr, :]
    y = _silu(acc)

    def head_l2(scale):
        for h in range(N_HEADS):
            yh = y[:, h * HEAD_DIM:(h + 1) * HEAD_DIM]
            inv = lax.rsqrt(jnp.sum(yh * yh, axis=-1, keepdims=True) + EPS) * scale
            qkv_ref[:, h * HEAD_DIM:(h + 1) * HEAD_DIM] = (yh * inv).astype(BF16)

    @pl.when(j == PCOL_Q)
    def _():
        head_l2(HEAD_DIM ** -0.5)

    @pl.when(j == PCOL_K)
    def _():
        head_l2(1.0)

    @pl.when(j == PCOL_V)
    def _():
        qkv_ref[...] = y.astype(BF16)

    @pl.when(j == 0)
    def _():
        s = small_ref[...]
        lane = lax.broadcasted_iota(jnp.int32, s.shape, 1)
        beta = _sigmoid(s)
        z = s + dtb_ref[...]
        softplus = jnp.maximum(z, 0.0) + jnp.log(1.0 + jnp.exp(-jnp.abs(z)))
        g = nega_ref[...] * softplus
        gb_ref[...] = jnp.where(lane < 2 * N_HEADS, beta, jnp.where(lane < 4 * N_HEADS, g, 0.0))


def _prep(p, small, conv_w8, nega, dtb, first_tiles, last_tiles):
    ta = p.shape[0]
    n_tiles = ta // PREP_TILE
    sub = PREP_TILE // 16
    n_sub = ta // 16
    kern = functools.partial(_prep_kernel, first_tiles=first_tiles, last_tiles=last_tiles)
    return pl.pallas_call(
        kern,
        grid=(n_tiles, 3),
        in_specs=[pl.BlockSpec((PREP_TILE, D_MODEL), lambda r, j: (r, j)),
                  pl.BlockSpec((16, D_MODEL), lambda r, j: (jnp.maximum(r * sub - 1, 0), j)),
                  pl.BlockSpec((16, D_MODEL), lambda r, j: (jnp.minimum((r + 1) * sub, n_sub - 1), j)),
                  pl.BlockSpec((8, D_MODEL), lambda r, j: (0, j)),
                  pl.BlockSpec((PREP_TILE, LANES), lambda r, j: (r, 0)),
                  pl.BlockSpec((1, LANES), lambda r, j: (0, 0)),
                  pl.BlockSpec((1, LANES), lambda r, j: (0, 0))],
        out_specs=[pl.BlockSpec((PREP_TILE, D_MODEL), lambda r, j: (r, j)),
                   pl.BlockSpec((PREP_TILE, LANES), lambda r, j: (r, 0))],
        out_shape=[jax.ShapeDtypeStruct((ta, 3 * D_MODEL), BF16),
                   jax.ShapeDtypeStruct((ta, LANES), F32)],
        scratch_shapes=[pltpu.VMEM((PREP_TILE + 16, D_MODEL), F32)],
        compiler_params=_params("arbitrary", "arbitrary"),
        name="prep",
    )(p, p, p, conv_w8, small, nega, dtb)


def _gdn_decays(d, gb):
    c = GDN_CHUNK
    row = lax.broadcasted_iota(jnp.int32, (c, c), 0)
    col = lax.broadcasted_iota(jnp.int32, (c, c), 1)
    incl = row >= col if d == 0 else row <= col
    lane = lax.broadcasted_iota(jnp.int32, gb.shape, 1)
    g_only = jnp.where(lane >= 2 * N_HEADS, jnp.where(lane < 4 * N_HEADS, gb, 0.0), 0.0)
    tri = jnp.where(incl, 1.0, 0.0).astype(BF16)
    g_hi = g_only.astype(BF16)
    g_r1 = g_only - g_hi.astype(F32)
    g_mid = g_r1.astype(BF16)
    g_lo = (g_r1 - g_mid.astype(F32)).astype(BF16)
    dot = lambda a, b: jnp.dot(a, b, preferred_element_type=F32)
    gcum = dot(tri, g_hi) + dot(tri, g_mid) + dot(tri, g_lo)
    g_end = gcum[c - 1:c, :] if d == 0 else gcum[0:1, :]
    return gcum, gcum.T, jnp.exp(gcum), jnp.exp(g_end - gcum), jnp.exp(g_end)


def _gdn_kernel(qf, kf, vf, gbf, qb, kb, vb, gbb, of_ref, ob_ref, s_ref):
    @pl.when(pl.program_id(1) == 0)
    def _():
        s_ref[...] = jnp.zeros_like(s_ref)

    c = GDN_CHUNK
    row = lax.broadcasted_iota(jnp.int32, (c, c), 0)
    col = lax.broadcasted_iota(jnp.int32, (c, c), 1)
    eye = jnp.where(row == col, 1.0, 0.0)
    masks = ((row >= col, row > col), (row <= col, row < col))
    refs = ((qf, kf, vf, gbf, of_ref), (qb, kb, vb, gbb, ob_ref))
    gbv = [refs[d][3][...] for d in range(2)]
    dec = [_gdn_decays(d, gbv[d]) for d in range(2)]
    chains = [(d, h) for d in range(2) for h in range(N_HEADS)]

    st = []
    for d, h in chains:
        q_ref, k_ref, v_ref, _, _ = refs[d]
        hs = slice(h * HEAD_DIM, (h + 1) * HEAD_DIM)
        q, k, v = q_ref[:, hs], k_ref[:, hs], v_ref[:, hs]
        both = _mm_nt(jnp.concatenate([q, k], axis=0), k)
        st.append(dict(q=q.astype(F32), k=k.astype(F32), v=v.astype(F32), qk=both[:c], kk=both[c:]))

    for (d, h), e in zip(chains, st):
        cb = d * N_HEADS + h
        cg = 2 * N_HEADS + cb
        gcum, gcum_t, exp_g, exp_rest, exp_end = dec[d]
        incl, strict = masks[d]
        e["beta"] = gbv[d][:, cb:cb + 1]
        diff = gcum[:, cg:cg + 1] - gcum_t[cg:cg + 1, :]
        decay = jnp.where(incl, jnp.exp(jnp.minimum(diff, 0.0)), 0.0)
        e["y"] = jnp.where(strict, -(e["beta"] * e["kk"] * decay), 0.0)
        e["qk"] = e["qk"] * decay
        e["eg"] = exp_g[:, cg:cg + 1]
        e["er"] = exp_rest[:, cg:cg + 1]
        e["ee"] = exp_end[:, cg:cg + 1]

    for e in st:
        e["t"] = eye + e["y"]
        e["y"] = _mm(e["y"], e["y"])
    for _ in range(4):
        for e in st:
            out = _mm(jnp.concatenate([e["y"], e["t"]], axis=0), e["y"])
            e["y"] = out[:c]
            e["t"] = e["t"] + out[c:]
    for e in st:
        e["t"] = e["t"] + _mm(e["t"], e["y"])

    for e in st:
        kb_ = e["k"] * e["beta"]
        uw = _mm(e["t"], jnp.concatenate([e["v"] * e["beta"], kb_ * e["eg"]], axis=-1))
        e["u"], e["w"] = uw[:, :HEAD_DIM], uw[:, HEAD_DIM:]

    for (d, h), e in zip(chains, st):
        e["s"] = s_ref[d, h]
        ws = _mm(jnp.concatenate([e["w"], e["q"] * e["eg"]], axis=0), e["s"])
        e["v_new"] = e["u"] - ws[:c]
        e["o"] = ws[c:]

    for (d, h), e in zip(chains, st):
        k_dec_t = (e["k"] * e["er"]).T
        out = _mm(jnp.concatenate([e["qk"], k_dec_t], axis=0), e["v_new"])
        refs[d][4][:, h * HEAD_DIM:(h + 1) * HEAD_DIM] = (e["o"] + out[:c]).astype(BF16)
        s_ref[d, h] = e["s"] * e["ee"] + out[c:]


def _gdn(qkv, gb, batch, ctx_chunks, lat_chunks):
    ta = qkv.shape[0]
    n_steps = ctx_chunks + lat_chunks
    lat0 = batch * ctx_chunks

    def fwd_blk(b, s):
        return jnp.where(s < ctx_chunks, b * ctx_chunks + s, lat0 + b * lat_chunks + (s - ctx_chunks))

    def bwd_blk(b, s):
        return jnp.where(s < ctx_chunks, b * ctx_chunks + (ctx_chunks - 1 - s),
                         lat0 + b * lat_chunks + (lat_chunks - 1 - (s - ctx_chunks)))

    def specs(blk):
        col = lambda j: pl.BlockSpec((GDN_CHUNK, D_MODEL), lambda b, s: (blk(b, s), j))
        return [col(0), col(1), col(2), pl.BlockSpec((GDN_CHUNK, LANES), lambda b, s: (blk(b, s), 0))]

    out = lambda blk: pl.BlockSpec((GDN_CHUNK, D_MODEL), lambda b, s: (blk(b, s), 0))
    return pl.pallas_call(
        _gdn_kernel,
        grid=(batch, n_steps),
        in_specs=specs(fwd_blk) + specs(bwd_blk),
        out_specs=[out(fwd_blk), out(bwd_blk)],
        out_shape=[jax.ShapeDtypeStruct((ta, D_MODEL), BF16)] * 2,
        scratch_shapes=[pltpu.VMEM((2, N_HEADS, HEAD_DIM, HEAD_DIM), F32)],
        compiler_params=_params("arbitrary", "arbitrary"),
        name="gdn",
    )(qkv, qkv, qkv, gb, qkv, qkv, qkv, gb)


def _sgu_kernel(u_ref, v_ref, w_ref, b_ref, o_ref):
    for ch in range(u_ref.shape[0] // SGU_CHUNK):
        rs = slice(ch * SGU_CHUNK, (ch + 1) * SGU_CHUNK)
        for g in range(SGU_GROUPS):
            cs = slice(g * SGU_GROUP_DIM, (g + 1) * SGU_GROUP_DIM)
            mixed = jnp.dot(w_ref[g], v_ref[rs, cs], preferred_element_type=F32) + b_ref[:, cs]
            o_ref[rs, cs] = (u_ref[rs, cs].astype(F32) * mixed).astype(BF16)


def _sgu(p, w_s, b_full, lat_tile0, n_lat_tiles):
    return pl.pallas_call(
        _sgu_kernel,
        grid=(n_lat_tiles,),
        in_specs=[pl.BlockSpec((ROW_TILE, D_MODEL), lambda i: (i + lat_tile0, PCOL_U)),
                  pl.BlockSpec((ROW_TILE, D_MODEL), lambda i: (i + lat_tile0, PCOL_SV)),
                  pl.BlockSpec((SGU_GROUPS, SGU_CHUNK, SGU_CHUNK), lambda i: (0, 0, 0)),
                  pl.BlockSpec((SGU_CHUNK, D_MODEL), lambda i: (0, 0))],
        out_specs=pl.BlockSpec((ROW_TILE, D_MODEL), lambda i: (i, 0)),
        out_shape=jax.ShapeDtypeStruct((n_lat_tiles * ROW_TILE, D_MODEL), BF16),
        compiler_params=_params("arbitrary"),
        name="sgu",
    )(p, p, w_s, b_full)


def _merge_kernel(of_ref, ob_ref, z_ref, ysgu_ref, ga_ref, gb_ref, x_ref, wa_ref, wb_ref, wo_ref,
                  gng_ref, gate_ref, nfg_ref, shift_ref, scale_ref, wr_ref, br_ref,
                  xmid_ref, h2_ref, rgate_ref, ygdn_ref):
    o = of_ref[...].astype(F32) + ob_ref[...].astype(F32)
    for h in range(N_HEADS):
        hs = slice(h * HEAD_DIM, (h + 1) * HEAD_DIM)
        oh = o[:, hs]
        inv = lax.rsqrt(jnp.mean(oh * oh, axis=-1, keepdims=True) + EPS)
        ygdn_ref[:, hs] = (oh * inv * gng_ref[...] * z_ref[:, hs].astype(F32)).astype(BF16)
    ya = jnp.dot(ygdn_ref[...], wa_ref[...], preferred_element_type=F32)
    yb = jnp.dot(ysgu_ref[...], wb_ref[...], preferred_element_type=F32)
    merged = ga_ref[...].astype(F32) * ya + gb_ref[...].astype(F32) * yb
    mix = jnp.dot(merged.astype(BF16), wo_ref[...], preferred_element_type=F32)
    xm = x_ref[...] + gate_ref[0] * mix
    xmid_ref[...] = xm
    hn = xm * lax.rsqrt(jnp.mean(xm * xm, axis=-1, keepdims=True) + EPS) * nfg_ref[...]
    h2 = hn * (1.0 + scale_ref[0]) + shift_ref[0]
    h2_ref[...] = h2.astype(BF16)

    h_hi = h2.astype(BF16)
    h_lo = (h2 - h_hi.astype(F32)).astype(BF16)
    wr = wr_ref[...]
    w_hi = wr.astype(BF16)
    w_lo = (wr - w_hi.astype(F32)).astype(BF16)
    dot = lambda a, b: jnp.dot(a, b, preferred_element_type=F32)
    lg = dot(h_hi, w_hi) + dot(h_lo, w_hi) + dot(h_hi, w_lo) + br_ref[...]

    lane = lax.broadcasted_iota(jnp.int32, lg.shape, 1).astype(F32)
    far = float(LANES)
    gl = jnp.where(lane < N_EXPERT_GROUPS, lg, NEG_BIG)
    gmax = jnp.max(gl, axis=-1, keepdims=True)
    p_g = 1.0 / jnp.sum(jnp.exp(gl - gmax), axis=-1, keepdims=True)
    grp = jnp.min(jnp.where(gl == gmax, lane, far), axis=-1, keepdims=True)
    lo = N_EXPERT_GROUPS + EXPERTS_PER_GROUP * grp
    in_grp = jnp.where(lane >= lo, jnp.where(lane < lo + EXPERTS_PER_GROUP, 1.0, 0.0), 0.0)
    el = jnp.where(in_grp > 0.0, lg, NEG_BIG)
    m1 = jnp.max(el, axis=-1, keepdims=True)
    i1 = jnp.min(jnp.where(el == m1, lane, far), axis=-1, keepdims=True)
    el2 = jnp.where(lane == i1, NEG_BIG, el)
    m2 = jnp.max(el2, axis=-1, keepdims=True)
    i2 = jnp.min(jnp.where(el2 == m2, lane, far), axis=-1, keepdims=True)
    t = jnp.exp(m2 - m1)
    w1 = p_g / (1.0 + t)
    w2 = w1 * t
    rgate_ref[...] = jnp.where(lane == i1, w1, 0.0) + jnp.where(lane == i2, w2, 0.0)


def _merge(o_f, o_b, p, y_sgu, x2, w_a, w_b, w_o, gng, gate3, nfg, shift3, scale3, w_r, b_r,
           lat_tile0, tiles_per_batch):
    t_lat = x2.shape[0]
    n_tiles = t_lat // ROW_TILE
    lat = lambda c: pl.BlockSpec((ROW_TILE, D_MODEL), lambda i: (i + lat_tile0, c))
    own = lambda: pl.BlockSpec((ROW_TILE, D_MODEL), lambda i: (i, 0))
    mat = lambda: pl.BlockSpec((D_MODEL, D_MODEL), lambda i: (0, 0))
    vec = lambda: pl.BlockSpec((1, D_MODEL), lambda i: (0, 0))
    per_b = lambda: pl.BlockSpec((1, 1, D_MODEL), lambda i: (i // tiles_per_batch, 0, 0))
    return pl.pallas_call(
        _merge_kernel,
        grid=(n_tiles,),
        in_specs=[lat(0), lat(0), lat(PCOL_Z), own(), lat(PCOL_GA), lat(PCOL_GB), own(),
                  mat(), mat(), mat(), pl.BlockSpec((1, HEAD_DIM), lambda i: (0, 0)),
                  per_b(), vec(), per_b(), per_b(),
                  pl.BlockSpec((D_MODEL, LANES), lambda i: (0, 0)),
                  pl.BlockSpec((1, LANES), lambda i: (0, 0))],
        out_specs=[own(), own(), pl.BlockSpec((ROW_TILE, LANES), lambda i: (i, 0))],
        out_shape=[jax.ShapeDtypeStruct((t_lat, D_MODEL), F32),
                   jax.ShapeDtypeStruct((t_lat, D_MODEL), BF16),
                   jax.ShapeDtypeStruct((t_lat, LANES), F32)],
        scratch_shapes=[pltpu.VMEM((ROW_TILE, D_MODEL), BF16)],
        compiler_params=_params("arbitrary"),
        name="merge",
    )(o_f, o_b, p, y_sgu, p, p, x2, w_a, w_b, w_o, gng, gate3, nfg, shift3, scale3, w_r, b_r)


def _moe_kernel(h_ref, rg_ref, w1_ref, w3_ref, w2_ref, xmid_ref, gate_ref, fng_ref, o_ref, acc_ref):
    q = pl.program_id(1)

    @pl.when(q == 0)
    def _():
        acc_ref[...] = jnp.zeros_like(acc_ref)

    h = h_ref[...]
    rg = rg_ref[...]
    lane = lax.broadcasted_iota(jnp.int32, rg.shape, 1)
    for e in range(MOE_EXPERTS_PER_STEP):
        col = N_EXPERT_GROUPS + q * MOE_EXPERTS_PER_STEP + e
        gate = jnp.sum(jnp.where(lane == col, rg, 0.0), axis=-1, keepdims=True)
        a = jnp.dot(h, w1_ref[e], preferred_element_type=F32)
        b = jnp.dot(h, w3_ref[e], preferred_element_type=F32)
        hid = _silu(a) * b * gate
        acc_ref[...] += jnp.dot(hid.astype(BF16), w2_ref[e], preferred_element_type=F32)

    @pl.when(q == pl.num_programs(1) - 1)
    def _():
        xo = xmid_ref[...] + gate_ref[0] * acc_ref[...]
        o_ref[...] = xo * lax.rsqrt(jnp.mean(xo * xo, axis=-1, keepdims=True) + EPS) * fng_ref[...]


def _moe(h2, rgate, w1, w3, w2, xmid, gate3, fng, tiles_per_batch):
    t_lat = h2.shape[0]
    n_tiles = t_lat // MOE_TILE
    eps = MOE_EXPERTS_PER_STEP
    row = lambda w: pl.BlockSpec((MOE_TILE, w), lambda i, q: (i, 0))
    return pl.pallas_call(
        _moe_kernel,
        grid=(n_tiles, N_EXPERTS // eps),
        in_specs=[row(D_MODEL), row(LANES),
                  pl.BlockSpec((eps, D_MODEL, D_EXPERT), lambda i, q: (q, 0, 0)),
                  pl.BlockSpec((eps, D_MODEL, D_EXPERT), lambda i, q: (q, 0, 0)),
                  pl.BlockSpec((eps, D_EXPERT, D_MODEL), lambda i, q: (q, 0, 0)),
                  row(D_MODEL),
                  pl.BlockSpec((1, 1, D_MODEL), lambda i, q: (i // tiles_per_batch, 0, 0)),
                  pl.BlockSpec((1, D_MODEL), lambda i, q: (0, 0))],
        out_specs=row(D_MODEL),
        out_shape=jax.ShapeDtypeStruct((t_lat, D_MODEL), F32),
        scratch_shapes=[pltpu.VMEM((MOE_TILE, D_MODEL), F32)],
        compiler_params=_params("arbitrary", "arbitrary"),
        name="moe",
    )(h2, rgate, w1, w3, w2, xmid, gate3, fng)


def kernel(x, c, ctx, c_ctx, ada_w, ada_b, norm_mix_g, w_in, conv_w, a_log, dt_bias, gdn_norm_g, sgu_ln_g, sgu_ln_b, sgu_w, sgu_b, w_branch_a, w_branch_b, w_out, norm_ffn_g, router_group_w, router_group_b, router_expert_w, router_expert_b, expert_w1, expert_w3, expert_w2, final_norm_g):
    batch, seq, d = x.shape
    ctx_len = ctx.shape[1]
    assert d == D_MODEL and ada_w.shape[0] == 1, "single-layer block with D_MODEL channels"
    assert batch * ctx_len == ROW_TILE, "context rows of all samples form one row tile"
    assert seq % MOE_TILE == 0 and ctx_len % PREP_TILE == 0 and batch + 1 <= 8
    t_lat = batch * seq
    row = lambda v: v.reshape(1, -1).astype(F32)

    cond = jnp.zeros((8, d), F32).at[:batch].set(c).at[batch].set(c_ctx)
    mod = _adaln(cond, ada_w[0], ada_b[0]).reshape(8, N_MOD, d)
    mod_row = lambda k: mod[:batch + 1, k].reshape(batch + 1, 1, d)

    w_l = w_in[0]
    w_main = jnp.concatenate([w_l[:, :COL_BETA], w_l[:, COL_Z:]], axis=1).astype(BF16)
    w_small = jnp.zeros((d, LANES), BF16).at[:, :4 * N_HEADS].set(w_l[:, COL_BETA:COL_Z].astype(BF16))
    x2 = x.reshape(t_lat, d)
    tiles_per_batch = seq // ROW_TILE
    p, small = _inproj(x2, ctx.reshape(batch * ctx_len, d), row(norm_mix_g), mod_row(0), mod_row(1),
                       w_main, w_small, row(sgu_ln_g), row(sgu_ln_b), tiles_per_batch)

    ctx_t, lat_t = ctx_len // PREP_TILE, seq // PREP_TILE
    starts = [b * ctx_t for b in range(batch)] + [batch * ctx_t + b * lat_t for b in range(batch)]
    ends = [(b + 1) * ctx_t - 1 for b in range(batch)] + [batch * ctx_t + (b + 1) * lat_t - 1 for b in range(batch)]
    conv_w8 = jnp.zeros((8, 3 * D_GDN), F32).at[:CONV_K].set(conv_w[0])
    pad_lanes = lambda v: jnp.zeros((1, LANES), F32).at[0, 2 * N_HEADS:4 * N_HEADS].set(v.reshape(-1))
    qkv, gb = _prep(p, small, conv_w8, pad_lanes(-jnp.exp(a_log[0])), pad_lanes(dt_bias[0]),
                    tuple(starts), tuple(ends))

    o_f, o_b = _gdn(qkv, gb, batch, ctx_len // GDN_CHUNK, seq // GDN_CHUNK)

    b_full = jnp.repeat(sgu_b[0].T, SGU_GROUP_DIM, axis=1).astype(F32)
    y_sgu = _sgu(p, sgu_w[0].astype(BF16), b_full, 1, t_lat // ROW_TILE)

    w_r = jnp.zeros((d, LANES), F32).at[:, :N_EXPERT_GROUPS].set(router_group_w[0]) \
        .at[:, N_EXPERT_GROUPS:N_EXPERT_GROUPS + N_EXPERTS].set(router_expert_w[0])
    b_r = jnp.zeros((1, LANES), F32).at[0, :N_EXPERT_GROUPS].set(router_group_b[0]) \
        .at[0, N_EXPERT_GROUPS:N_EXPERT_GROUPS + N_EXPERTS].set(router_expert_b[0])
    gng = gdn_norm_g[0].reshape(1, HEAD_DIM).astype(F32)
    xmid, h2, rgate = _merge(o_f, o_b, p, y_sgu, x2, w_branch_a[0].astype(BF16), w_branch_b[0].astype(BF16),
                             w_out[0].astype(BF16), gng, mod_row(2)[:batch], row(norm_ffn_g),
                             mod_row(3)[:batch], mod_row(4)[:batch], w_r, b_r, 1, tiles_per_batch)

    out = _moe(h2, rgate, expert_w1[0].astype(BF16), expert_w3[0].astype(BF16), expert_w2[0].astype(BF16),
               xmid, mod_row(5)[:batch], row(final_norm_g), seq // MOE_TILE)
    return out.reshape(batch, seq, d)
```

```python
import functools
import math

import jax
import jax.numpy as jnp
from jax import lax
from jax.experimental import pallas as pl
from jax.experimental.pallas import tpu as pltpu

F32 = jnp.float32
BF16 = jnp.bfloat16

D_MODEL = 1024
N_HEADS = 8
HEAD_DIM = 128
D_GDN = N_HEADS * HEAD_DIM
CONV_K = 5
GDN_CHUNK = 64
SGU_GROUPS = 8
SGU_GROUP_DIM = 128
D_SGU = SGU_GROUPS * SGU_GROUP_DIM
SGU_CHUNK = 128
N_EXPERT_GROUPS = 4
EXPERTS_PER_GROUP = 8
N_EXPERTS = N_EXPERT_GROUPS * EXPERTS_PER_GROUP
D_EXPERT = 256
N_MOD = 6
EPS = 1e-6
COL_BETA = 3 * D_GDN
COL_Z = COL_BETA + 4 * N_HEADS

LANES = 128
ROW_TILE = 512
PREP_TILE = 256
MOE_TILE = 1024
MOE_EXPERTS_PER_STEP = 4
MOE_CAP = 320
VMEM_LIMIT = 48 * 1024 * 1024
MOE_VMEM_LIMIT = 56 * 1024 * 1024
NEG_BIG = -1e30

PCOL_Q, PCOL_K, PCOL_V, PCOL_Z, PCOL_U, PCOL_SV, PCOL_GA, PCOL_GB = range(8)


def _mm(a, b):
    return jnp.dot(a.astype(BF16), b.astype(BF16), preferred_element_type=F32)


def _mm_nt(a, b):
    return lax.dot_general(a.astype(BF16), b.astype(BF16), (((1,), (1,)), ((), ())),
                           preferred_element_type=F32)


def _sigmoid(x):
    return 1.0 / (1.0 + jnp.exp(-x))


def _silu(x):
    return x * _sigmoid(x)


def _gelu_tanh(x):
    return 0.5 * x * (1.0 + jnp.tanh(math.sqrt(2.0 / math.pi) * (x + 0.044715 * (x * x * x))))


def _params(*sem):
    return pltpu.CompilerParams(dimension_semantics=sem, vmem_limit_bytes=VMEM_LIMIT)


def _adaln_kernel(c_ref, w_ref, b_ref, o_ref):
    o_ref[...] = _mm(_silu(c_ref[...]), w_ref[...]) + b_ref[...]


def _adaln(cond, w, b):
    n = w.shape[1]
    tn = 1536
    return pl.pallas_call(
        _adaln_kernel,
        grid=(n // tn,),
        in_specs=[pl.BlockSpec((8, D_MODEL), lambda j: (0, 0)),
                  pl.BlockSpec((D_MODEL, tn), lambda j: (0, j)),
                  pl.BlockSpec((1, tn), lambda j: (0, j))],
        out_specs=pl.BlockSpec((8, tn), lambda j: (0, j)),
        out_shape=jax.ShapeDtypeStruct((8, n), F32),
        compiler_params=_params("arbitrary"),
        name="adaln",
    )(cond, w, b.reshape(1, n))


def _inproj_kernel(x_ref, ctx_ref, g_ref, shift_ref, scale_ref, wm_ref, ws_ref, lng_ref, lnb_ref,
                   p_ref, small_ref, h_ref):
    i = pl.program_id(0)
    j = pl.program_id(1)

    def norm_mod(xv):
        y = xv * lax.rsqrt(jnp.mean(xv * xv, axis=-1, keepdims=True) + EPS) * g_ref[...]
        return y * (1.0 + scale_ref[0]) + shift_ref[0]

    @pl.when(j == 0)
    def _():
        @pl.when(i == 0)
        def _():
            h_ref[...] = norm_mod(ctx_ref[...]).astype(BF16)

        @pl.when(i > 0)
        def _():
            h_ref[...] = norm_mod(x_ref[...]).astype(BF16)

        small_ref[...] = jnp.dot(h_ref[...], ws_ref[...], preferred_element_type=F32)

    def raw(a):
        return a

    def gelu_ln(a):
        a = _gelu_tanh(a)
        mu = jnp.mean(a, axis=-1, keepdims=True)
        ac = a - mu
        var = jnp.mean(ac * ac, axis=-1, keepdims=True)
        return ac * lax.rsqrt(var + EPS) * lng_ref[...] + lnb_ref[...]

    epilogues = (raw, raw, raw, _silu, _gelu_tanh, gelu_ln, _sigmoid, _sigmoid)
    for step in range(len(epilogues) // 2):
        @pl.when(j == step)
        def _(step=step):
            for half in range(2):
                cs = slice(half * D_MODEL, (half + 1) * D_MODEL)
                a = jnp.dot(h_ref[...], wm_ref[:, cs], preferred_element_type=F32)
                p_ref[:, cs] = epilogues[2 * step + half](a).astype(BF16)


def _inproj(x2, ctx2, norm_g, shift3, scale3, w_main, w_small, ln_g, ln_b, tiles_per_batch):
    t_lat = x2.shape[0]
    n_lat = t_lat // ROW_TILE
    n_tiles = n_lat + 1
    tn = 2 * D_MODEL
    n_col = w_main.shape[1] // tn
    n_batch = shift3.shape[0] - 1
    sel = lambda i: jnp.where(i == 0, n_batch, (jnp.maximum(i, 1) - 1) // tiles_per_batch)
    vec = lambda: pl.BlockSpec((1, D_MODEL), lambda i, j: (0, 0))
    return pl.pallas_call(
        _inproj_kernel,
        grid=(n_tiles, n_col),
        in_specs=[pl.BlockSpec((ROW_TILE, D_MODEL), lambda i, j: (jnp.maximum(i, 1) - 1, 0)),
                  pl.BlockSpec((ROW_TILE, D_MODEL), lambda i, j: (0, 0)),
                  vec(),
                  pl.BlockSpec((1, 1, D_MODEL), lambda i, j: (sel(i), 0, 0)),
                  pl.BlockSpec((1, 1, D_MODEL), lambda i, j: (sel(i), 0, 0)),
                  pl.BlockSpec((D_MODEL, tn), lambda i, j: (0, j)),
                  pl.BlockSpec((D_MODEL, LANES), lambda i, j: (0, 0)),
                  vec(), vec()],
        out_specs=[pl.BlockSpec((ROW_TILE, tn), lambda i, j: (i, j)),
                   pl.BlockSpec((ROW_TILE, LANES), lambda i, j: (i, 0))],
        out_shape=[jax.ShapeDtypeStruct((n_tiles * ROW_TILE, n_col * tn), BF16),
                   jax.ShapeDtypeStruct((n_tiles * ROW_TILE, LANES), F32)],
        scratch_shapes=[pltpu.VMEM((ROW_TILE, D_MODEL), BF16)],
        compiler_params=_params("arbitrary", "arbitrary"),
        name="inproj",
    )(x2, ctx2, norm_g, shift3, scale3, w_main, w_small, ln_g, ln_b)


def _prep_kernel(pm_ref, pp_ref, pn_ref, cw_ref, small_ref, nega_ref, dtb_ref, qkv_ref, gb_ref, ext_ref,
                 *, first_tiles, last_tiles):
    r = pl.program_id(0)
    j = pl.program_id(1)
    tr = pm_ref.shape[0]
    is_first = functools.reduce(jnp.logical_or, [r == t for t in first_tiles])
    is_last = functools.reduce(jnp.logical_or, [r == t for t in last_tiles])
    keep_prev = jnp.where(is_first, 0.0, 1.0)
    keep_next = jnp.where(is_last, 0.0, 1.0)

    ext_ref[8:8 + tr, :] = pm_ref[...].astype(F32)
    ext_ref[0:8, :] = pp_ref[...].astype(F32)[8:16, :] * keep_prev
    ext_ref[8 + tr:16 + tr, :] = pn_ref[...].astype(F32)[0:8, :] * keep_next

    base = 8 - CONV_K // 2
    acc = cw_ref[0:1, :] * ext_ref[base:base + tr, :]
    for k in range(1, CONV_K):
        acc = acc + cw_ref[k:k + 1, :] * ext_ref[base + k:base + k + tr, :]
    y = _silu(acc)

    def head_l2(scale):
        for h in range(N_HEADS):
            yh = y[:, h * HEAD_DIM:(h + 1) * HEAD_DIM]
            inv = lax.rsqrt(jnp.sum(yh * yh, axis=-1, keepdims=True) + EPS) * scale
            qkv_ref[:, h * HEAD_DIM:(h + 1) * HEAD_DIM] = (yh * inv).astype(BF16)

    @pl.when(j == PCOL_Q)
    def _():
        head_l2(HEAD_DIM ** -0.5)

    @pl.when(j == PCOL_K)
    def _():
        head_l2(1.0)

    @pl.when(j == PCOL_V)
    def _():
        qkv_ref[...] = y.astype(BF16)

    @pl.when(j == 0)
    def _():
        s = small_ref[...]
        lane = lax.broadcasted_iota(jnp.int32, s.shape, 1)
        beta = _sigmoid(s)
        z = s + dtb_ref[...]
        softplus = jnp.maximum(z, 0.0) + jnp.log(1.0 + jnp.exp(-jnp.abs(z)))
        g = nega_ref[...] * softplus
        gb_ref[...] = jnp.where(lane < 2 * N_HEADS, beta, jnp.where(lane < 4 * N_HEADS, g, 0.0))


def _prep(p, small, conv_w8, nega, dtb, first_tiles, last_tiles):
    ta = p.shape[0]
    n_tiles = ta // PREP_TILE
    sub = PREP_TILE // 16
    n_sub = ta // 16
    kern = functools.partial(_prep_kernel, first_tiles=first_tiles, last_tiles=last_tiles)
    return pl.pallas_call(
        kern,
        grid=(n_tiles, 3),
        in_specs=[pl.BlockSpec((PREP_TILE, D_MODEL), lambda r, j: (r, j)),
                  pl.BlockSpec((16, D_MODEL), lambda r, j: (jnp.maximum(r * sub - 1, 0), j)),
                  pl.BlockSpec((16, D_MODEL), lambda r, j: (jnp.minimum((r + 1) * sub, n_sub - 1), j)),
                  pl.BlockSpec((8, D_MODEL), lambda r, j: (0, j)),
                  pl.BlockSpec((PREP_TILE, LANES), lambda r, j: (r, 0)),
                  pl.BlockSpec((1, LANES), lambda r, j: (0, 0)),
                  pl.BlockSpec((1, LANES), lambda r, j: (0, 0))],
        out_specs=[pl.BlockSpec((PREP_TILE, D_MODEL), lambda r, j: (r, j)),
                   pl.BlockSpec((PREP_TILE, LANES), lambda r, j: (r, 0))],
        out_shape=[jax.ShapeDtypeStruct((ta, 3 * D_MODEL), BF16),
                   jax.ShapeDtypeStruct((ta, LANES), F32)],
        scratch_shapes=[pltpu.VMEM((PREP_TILE + 16, D_MODEL), F32)],
        compiler_params=_params("arbitrary", "arbitrary"),
        name="prep",
    )(p, p, p, conv_w8, small, nega, dtb)


def _gdn_decays(d, gb):
    c = GDN_CHUNK
    row = lax.broadcasted_iota(jnp.int32, (c, c), 0)
    col = lax.broadcasted_iota(jnp.int32, (c, c), 1)
    incl = row >= col if d == 0 else row <= col
    lane = lax.broadcasted_iota(jnp.int32, gb.shape, 1)
    g_only = jnp.where(lane >= 2 * N_HEADS, jnp.where(lane < 4 * N_HEADS, gb, 0.0), 0.0)
    tri = jnp.where(incl, 1.0, 0.0).astype(BF16)
    g_hi = g_only.astype(BF16)
    g_r1 = g_only - g_hi.astype(F32)
    g_mid = g_r1.astype(BF16)
    g_lo = (g_r1 - g_mid.astype(F32)).astype(BF16)
    dot = lambda a, b: jnp.dot(a, b, preferred_element_type=F32)
    gcum = dot(tri, g_hi) + dot(tri, g_mid) + dot(tri, g_lo)
    g_end = gcum[c - 1:c, :] if d == 0 else gcum[0:1, :]
    return gcum, gcum.T, jnp.exp(gcum), jnp.exp(g_end - gcum), jnp.exp(g_end)


def _gdn_kernel(qf, kf, vf, gbf, qb, kb, vb, gbb, of_ref, ob_ref, s_ref):
    @pl.when(pl.program_id(1) == 0)
    def _():
        s_ref[...] = jnp.zeros_like(s_ref)

    c = GDN_CHUNK
    row = lax.broadcasted_iota(jnp.int32, (c, c), 0)
    col = lax.broadcasted_iota(jnp.int32, (c, c), 1)
    eye = jnp.where(row == col, 1.0, 0.0)
    masks = ((row >= col, row > col), (row <= col, row < col))
    refs = ((qf, kf, vf, gbf, of_ref), (qb, kb, vb, gbb, ob_ref))
    gbv = [refs[d][3][...] for d in range(2)]
    dec = [_gdn_decays(d, gbv[d]) for d in range(2)]
    chains = [(d, h) for d in range(2) for h in range(N_HEADS)]

    st = []
    for d, h in chains:
        q_ref, k_ref, v_ref, _, _ = refs[d]
        hs = slice(h * HEAD_DIM, (h + 1) * HEAD_DIM)
        q, k, v = q_ref[:, hs], k_ref[:, hs], v_ref[:, hs]
        both = _mm_nt(jnp.concatenate([q, k], axis=0), k)
        st.append(dict(q=q.astype(F32), k=k.astype(F32), v=v.astype(F32), qk=both[:c], kk=both[c:]))

    for (d, h), e in zip(chains, st):
        cb = d * N_HEADS + h
        cg = 2 * N_HEADS + cb
        gcum, gcum_t, exp_g, exp_rest, exp_end = dec[d]
        incl, strict = masks[d]
        e["beta"] = gbv[d][:, cb:cb + 1]
        diff = gcum[:, cg:cg + 1] - gcum_t[cg:cg + 1, :]
        decay = jnp.where(incl, jnp.exp(jnp.minimum(diff, 0.0)), 0.0)
        e["y"] = jnp.where(strict, -(e["beta"] * e["kk"] * decay), 0.0)
        e["qk"] = e["qk"] * decay
        e["eg"] = exp_g[:, cg:cg + 1]
        e["er"] = exp_rest[:, cg:cg + 1]
        e["ee"] = exp_end[:, cg:cg + 1]

    levels = range(GDN_CHUNK.bit_length() - 1)
    sels = []
    for d in range(2):
        inner, outer = (col, row) if d == 0 else (row, col)
        sels.append([((outer >> lvl) & 1 == 1) & ((inner >> lvl) == (outer >> lvl) - 1) for lvl in levels])
    for (d, h), e in zip(chains, st):
        e["t"] = eye + jnp.where(sels[d][0], e["y"], 0.0)
    for lvl in levels[1:]:
        for (d, h), e in zip(chains, st):
            e["yd"] = _mm(jnp.where(sels[d][lvl], e["y"], 0.0), e["t"])
        for e in st:
            e["t"] = e["t"] + _mm(e["t"], e["yd"])

    for e in st:
        kb_ = e["k"] * e["beta"]
        uw = _mm(e["t"], jnp.concatenate([e["v"] * e["beta"], kb_ * e["eg"]], axis=-1))
        e["u"], e["w"] = uw[:, :HEAD_DIM], uw[:, HEAD_DIM:]

    for (d, h), e in zip(chains, st):
        e["s"] = s_ref[d, h]
        ws = _mm(jnp.concatenate([e["w"], e["q"] * e["eg"]], axis=0), e["s"])
        e["v_new"] = e["u"] - ws[:c]
        e["o"] = ws[c:]

    for (d, h), e in zip(chains, st):
        k_dec_t = (e["k"] * e["er"]).T
        out = _mm(jnp.concatenate([e["qk"], k_dec_t], axis=0), e["v_new"])
        refs[d][4][:, h * HEAD_DIM:(h + 1) * HEAD_DIM] = (e["o"] + out[:c]).astype(BF16)
        s_ref[d, h] = e["s"] * e["ee"] + out[c:]


def _gdn(qkv, gb, batch, ctx_chunks, lat_chunks):
    ta = qkv.shape[0]
    n_steps = ctx_chunks + lat_chunks
    lat0 = batch * ctx_chunks

    def fwd_blk(b, s):
        return jnp.where(s < ctx_chunks, b * ctx_chunks + s, lat0 + b * lat_chunks + (s - ctx_chunks))

    def bwd_blk(b, s):
        return jnp.where(s < ctx_chunks, b * ctx_chunks + (ctx_chunks - 1 - s),
                         lat0 + b * lat_chunks + (lat_chunks - 1 - (s - ctx_chunks)))

    def specs(blk):
        col = lambda j: pl.BlockSpec((GDN_CHUNK, D_MODEL), lambda b, s: (blk(b, s), j))
        return [col(0), col(1), col(2), pl.BlockSpec((GDN_CHUNK, LANES), lambda b, s: (blk(b, s), 0))]

    out = lambda blk: pl.BlockSpec((GDN_CHUNK, D_MODEL), lambda b, s: (blk(b, s), 0))
    return pl.pallas_call(
        _gdn_kernel,
        grid=(batch, n_steps),
        in_specs=specs(fwd_blk) + specs(bwd_blk),
        out_specs=[out(fwd_blk), out(bwd_blk)],
        out_shape=[jax.ShapeDtypeStruct((ta, D_MODEL), BF16)] * 2,
        scratch_shapes=[pltpu.VMEM((2, N_HEADS, HEAD_DIM, HEAD_DIM), F32)],
        compiler_params=_params("arbitrary", "arbitrary"),
        name="gdn",
    )(qkv, qkv, qkv, gb, qkv, qkv, qkv, gb)


def _sgu_kernel(u_ref, v_ref, w_ref, b_ref, o_ref):
    for ch in range(u_ref.shape[0] // SGU_CHUNK):
        rs = slice(ch * SGU_CHUNK, (ch + 1) * SGU_CHUNK)
        for g in range(SGU_GROUPS):
            cs = slice(g * SGU_GROUP_DIM, (g + 1) * SGU_GROUP_DIM)
            mixed = jnp.dot(w_ref[g], v_ref[rs, cs], preferred_element_type=F32) + b_ref[:, cs]
            o_ref[rs, cs] = (u_ref[rs, cs].astype(F32) * mixed).astype(BF16)


def _sgu(p, w_s, b_full, lat_tile0, n_lat_tiles):
    return pl.pallas_call(
        _sgu_kernel,
        grid=(n_lat_tiles,),
        in_specs=[pl.BlockSpec((ROW_TILE, D_MODEL), lambda i: (i + lat_tile0, PCOL_U)),
                  pl.BlockSpec((ROW_TILE, D_MODEL), lambda i: (i + lat_tile0, PCOL_SV)),
                  pl.BlockSpec((SGU_GROUPS, SGU_CHUNK, SGU_CHUNK), lambda i: (0, 0, 0)),
                  pl.BlockSpec((SGU_CHUNK, D_MODEL), lambda i: (0, 0))],
        out_specs=pl.BlockSpec((ROW_TILE, D_MODEL), lambda i: (i, 0)),
        out_shape=jax.ShapeDtypeStruct((n_lat_tiles * ROW_TILE, D_MODEL), BF16),
        compiler_params=_params("arbitrary"),
        name="sgu",
    )(p, p, w_s, b_full)


def _merge_kernel(of_ref, ob_ref, z_ref, ysgu_ref, ga_ref, gb_ref, x_ref, wa_ref, wb_ref, wo_ref,
                  gng_ref, gate_ref, nfg_ref, shift_ref, scale_ref, wr_ref, br_ref,
                  xmid_ref, h2_ref, rgate_ref, ygdn_ref):
    o = of_ref[...].astype(F32) + ob_ref[...].astype(F32)
    for h in range(N_HEADS):
        hs = slice(h * HEAD_DIM, (h + 1) * HEAD_DIM)
        oh = o[:, hs]
        inv = lax.rsqrt(jnp.mean(oh * oh, axis=-1, keepdims=True) + EPS)
        ygdn_ref[:, hs] = (oh * inv * gng_ref[...] * z_ref[:, hs].astype(F32)).astype(BF16)
    ya = jnp.dot(ygdn_ref[...], wa_ref[...], preferred_element_type=F32)
    yb = jnp.dot(ysgu_ref[...], wb_ref[...], preferred_element_type=F32)
    merged = ga_ref[...].astype(F32) * ya + gb_ref[...].astype(F32) * yb
    mix = jnp.dot(merged.astype(BF16), wo_ref[...], preferred_element_type=F32)
    xm = x_ref[...] + gate_ref[0] * mix
    xmid_ref[...] = xm
    hn = xm * lax.rsqrt(jnp.mean(xm * xm, axis=-1, keepdims=True) + EPS) * nfg_ref[...]
    h2 = hn * (1.0 + scale_ref[0]) + shift_ref[0]
    h2_ref[...] = h2.astype(BF16)

    h_hi = h2.astype(BF16)
    h_lo = (h2 - h_hi.astype(F32)).astype(BF16)
    wr = wr_ref[...]
    w_hi = wr.astype(BF16)
    w_lo = (wr - w_hi.astype(F32)).astype(BF16)
    dot = lambda a, b: jnp.dot(a, b, preferred_element_type=F32)
    lg = dot(h_hi, w_hi) + dot(h_lo, w_hi) + dot(h_hi, w_lo) + br_ref[...]

    lane = lax.broadcasted_iota(jnp.int32, lg.shape, 1).astype(F32)
    far = float(LANES)
    gl = jnp.where(lane < N_EXPERT_GROUPS, lg, NEG_BIG)
    gmax = jnp.max(gl, axis=-1, keepdims=True)
    p_g = 1.0 / jnp.sum(jnp.exp(gl - gmax), axis=-1, keepdims=True)
    grp = jnp.min(jnp.where(gl == gmax, lane, far), axis=-1, keepdims=True)
    lo = N_EXPERT_GROUPS + EXPERTS_PER_GROUP * grp
    in_grp = jnp.where(lane >= lo, jnp.where(lane < lo + EXPERTS_PER_GROUP, 1.0, 0.0), 0.0)
    el = jnp.where(in_grp > 0.0, lg, NEG_BIG)
    m1 = jnp.max(el, axis=-1, keepdims=True)
    i1 = jnp.min(jnp.where(el == m1, lane, far), axis=-1, keepdims=True)
    el2 = jnp.where(lane == i1, NEG_BIG, el)
    m2 = jnp.max(el2, axis=-1, keepdims=True)
    i2 = jnp.min(jnp.where(el2 == m2, lane, far), axis=-1, keepdims=True)
    t = jnp.exp(m2 - m1)
    w1 = p_g / (1.0 + t)
    w2 = w1 * t
    rgate_ref[...] = jnp.where(lane == 0.0, grp, jnp.where(lane == i1, w1, 0.0) + jnp.where(lane == i2, w2, 0.0))


def _merge(o_f, o_b, p, y_sgu, x2, w_a, w_b, w_o, gng, gate3, nfg, shift3, scale3, w_r, b_r,
           lat_tile0, tiles_per_batch):
    t_lat = x2.shape[0]
    n_tiles = t_lat // ROW_TILE
    lat = lambda c: pl.BlockSpec((ROW_TILE, D_MODEL), lambda i: (i + lat_tile0, c))
    own = lambda: pl.BlockSpec((ROW_TILE, D_MODEL), lambda i: (i, 0))
    mat = lambda: pl.BlockSpec((D_MODEL, D_MODEL), lambda i: (0, 0))
    vec = lambda: pl.BlockSpec((1, D_MODEL), lambda i: (0, 0))
    per_b = lambda: pl.BlockSpec((1, 1, D_MODEL), lambda i: (i // tiles_per_batch, 0, 0))
    return pl.pallas_call(
        _merge_kernel,
        grid=(n_tiles,),
        in_specs=[lat(0), lat(0), lat(PCOL_Z), own(), lat(PCOL_GA), lat(PCOL_GB), own(),
                  mat(), mat(), mat(), pl.BlockSpec((1, HEAD_DIM), lambda i: (0, 0)),
                  per_b(), vec(), per_b(), per_b(),
                  pl.BlockSpec((D_MODEL, LANES), lambda i: (0, 0)),
                  pl.BlockSpec((1, LANES), lambda i: (0, 0))],
        out_specs=[own(), own(), pl.BlockSpec((ROW_TILE, LANES), lambda i: (i, 0))],
        out_shape=[jax.ShapeDtypeStruct((t_lat, D_MODEL), F32),
                   jax.ShapeDtypeStruct((t_lat, D_MODEL), BF16),
                   jax.ShapeDtypeStruct((t_lat, LANES), F32)],
        scratch_shapes=[pltpu.VMEM((ROW_TILE, D_MODEL), BF16)],
        compiler_params=_params("arbitrary"),
        name="merge",
    )(o_f, o_b, p, y_sgu, p, p, x2, w_a, w_b, w_o, gng, gate3, nfg, shift3, scale3, w_r, b_r)


def _split3(x):
    hi = x.astype(BF16)
    r1 = x - hi.astype(F32)
    mid = r1.astype(BF16)
    return hi, mid, (r1 - mid.astype(F32)).astype(BF16)


def _moe_kernel(h_ref, rg_ref, w1_ref, w3_ref, w2_ref, xmid_ref, gate_ref, fng_ref, o_ref,
                before_ref, posr_ref, posc_ref, xc_ref, yc_ref, gc_ref, cnt_ref):
    i, g, hf = pl.program_id(0), pl.program_id(1), pl.program_id(2)
    tm = h_ref.shape[0]
    dot = lambda a, b: jnp.dot(a, b, preferred_element_type=F32)

    @pl.when((i == 0) & (g == 0) & (hf == 0))
    def _():
        r = lax.broadcasted_iota(jnp.int32, (tm, tm), 0)
        c = lax.broadcasted_iota(jnp.int32, (tm, tm), 1)
        before_ref[...] = jnp.where(r < c, 1.0, 0.0).astype(BF16)

    @pl.when((g == 0) & (hf == 0))
    def _():
        rg = rg_ref[...]
        lane = lax.broadcasted_iota(jnp.int32, rg.shape, 1).astype(F32)
        m_cols = jnp.where(rg[:, 0:1] == lane, 1.0, 0.0)
        sub = lax.broadcasted_iota(jnp.int32, (8, tm), 0).astype(F32)
        m_rows = jnp.where(rg.T[0:1, :] == sub, 1.0, 0.0)
        rank_r = dot(m_rows.astype(BF16), before_ref[...])
        posr_ref[...] = jnp.where(m_rows > 0.0, rank_r, -1.0)
        after_c = dot(before_ref[...], m_cols.astype(BF16))
        total_c = jnp.sum(m_cols, axis=0, keepdims=True)
        posc_ref[...] = jnp.where(m_cols > 0.0, total_c - after_c - 1.0, -1.0)
        total_r = jnp.sum(m_rows, axis=1, keepdims=True)
        for gg in range(N_EXPERT_GROUPS):
            cnt_ref[gg] = total_r[gg, 0].astype(jnp.int32)
        o_ref[...] = jnp.zeros_like(o_ref)

    cap = MOE_CAP
    n_blk = (cnt_ref[g] + (cap - 1)) // cap
    lane128 = lax.broadcasted_iota(jnp.int32, (tm, LANES), 1)
    gf = g.astype(F32)

    def block(blk, carry):
        r0 = pl.multiple_of(blk * cap, 16)
        rows = pl.ds(r0, cap)
        r0f = (blk * cap).astype(F32)

        @pl.when(hf == 0)
        def _():
            slot = lax.broadcasted_iota(jnp.int32, (cap, tm), 0).astype(F32) + r0f
            sel = jnp.where(posr_ref[pl.ds(g, 1), :] == slot, 1.0, 0.0).astype(BF16)
            xc_ref[rows, :] = dot(sel, h_ref[...]).astype(BF16)
            g_hi, g_mid, g_lo = _split3(rg_ref[...])
            gc_ref[rows, :] = dot(sel, g_hi) + dot(sel, g_mid) + dot(sel, g_lo)

        x = xc_ref[rows, :]
        gc = gc_ref[rows, :]
        lane = lax.broadcasted_iota(jnp.int32, gc.shape, 1)
        y = None
        for e in range(MOE_EXPERTS_PER_STEP):
            col = N_EXPERT_GROUPS + g * EXPERTS_PER_GROUP + hf * MOE_EXPERTS_PER_STEP + e
            gate = jnp.sum(jnp.where(lane == col, gc, 0.0), axis=-1, keepdims=True)
            hid = _silu(dot(x, w1_ref[e])) * dot(x, w3_ref[e]) * gate
            ye = dot(hid.astype(BF16), w2_ref[e])
            y = ye if y is None else y + ye

        @pl.when(hf == 0)
        def _():
            yc_ref[rows, :] = y

        @pl.when(hf == 1)
        def _():
            rank_c = jnp.sum(jnp.where(lane128 == g, posc_ref[...], 0.0), axis=-1, keepdims=True)
            slot = lax.broadcasted_iota(jnp.int32, (tm, cap), 1).astype(F32) + r0f
            sel_t = jnp.where(rank_c == slot, 1.0, 0.0).astype(BF16)
            o_ref[...] += dot(sel_t, (yc_ref[rows, :] + y).astype(BF16))

        return carry

    lax.fori_loop(0, n_blk, block, 0)

    @pl.when((g == pl.num_programs(1) - 1) & (hf == 1))
    def _():
        xo = xmid_ref[...] + gate_ref[0] * o_ref[...]
        o_ref[...] = xo * lax.rsqrt(jnp.mean(xo * xo, axis=-1, keepdims=True) + EPS) * fng_ref[...]


def _moe(h2, rgate, w1, w3, w2, xmid, gate3, fng, tiles_per_batch):
    t_lat = h2.shape[0]
    n_tiles = t_lat // MOE_TILE
    eps = MOE_EXPERTS_PER_STEP
    halves = EXPERTS_PER_GROUP // eps
    max_rows = (MOE_TILE // MOE_CAP + 1) * MOE_CAP
    row = lambda w: pl.BlockSpec((MOE_TILE, w), lambda i, g, hf: (i, 0))
    wspec = lambda a, b: pl.BlockSpec((eps, a, b), lambda i, g, hf: (g * halves + hf, 0, 0))
    return pl.pallas_call(
        _moe_kernel,
        grid=(n_tiles, N_EXPERT_GROUPS, halves),
        in_specs=[row(D_MODEL), row(LANES),
                  wspec(D_MODEL, D_EXPERT), wspec(D_MODEL, D_EXPERT), wspec(D_EXPERT, D_MODEL),
                  row(D_MODEL),
                  pl.BlockSpec((1, 1, D_MODEL), lambda i, g, hf: (i // tiles_per_batch, 0, 0)),
                  pl.BlockSpec((1, D_MODEL), lambda i, g, hf: (0, 0))],
        out_specs=row(D_MODEL),
        out_shape=jax.ShapeDtypeStruct((t_lat, D_MODEL), F32),
        scratch_shapes=[pltpu.VMEM((MOE_TILE, MOE_TILE), BF16),
                        pltpu.VMEM((8, MOE_TILE), F32),
                        pltpu.VMEM((MOE_TILE, LANES), F32),
                        pltpu.VMEM((max_rows, D_MODEL), BF16),
                        pltpu.VMEM((max_rows, D_MODEL), F32),
                        pltpu.VMEM((max_rows, LANES), F32),
                        pltpu.SMEM((N_EXPERT_GROUPS,), jnp.int32)],
        compiler_params=pltpu.CompilerParams(dimension_semantics=("arbitrary",) * 3,
                                             vmem_limit_bytes=MOE_VMEM_LIMIT),
        name="moe",
    )(h2, rgate, w1, w3, w2, xmid, gate3, fng)


def kernel(x, c, ctx, c_ctx, ada_w, ada_b, norm_mix_g, w_in, conv_w, a_log, dt_bias, gdn_norm_g, sgu_ln_g, sgu_ln_b, sgu_w, sgu_b, w_branch_a, w_branch_b, w_out, norm_ffn_g, router_group_w, router_group_b, router_expert_w, router_expert_b, expert_w1, expert_w3, expert_w2, final_norm_g):
    batch, seq, d = x.shape
    ctx_len = ctx.shape[1]
    assert d == D_MODEL and ada_w.shape[0] == 1, "single-layer block with D_MODEL channels"
    assert batch * ctx_len == ROW_TILE, "context rows of all samples form one row tile"
    assert seq % MOE_TILE == 0 and ctx_len % PREP_TILE == 0 and batch + 1 <= 8
    t_lat = batch * seq
    row = lambda v: v.reshape(1, -1).astype(F32)

    cond = jnp.zeros((8, d), F32).at[:batch].set(c).at[batch].set(c_ctx)
    mod = _adaln(cond, ada_w[0], ada_b[0]).reshape(8, N_MOD, d)
    mod_row = lambda k: mod[:batch + 1, k].reshape(batch + 1, 1, d)

    w_l = w_in[0]
    w_main = jnp.concatenate([w_l[:, :COL_BETA], w_l[:, COL_Z:]], axis=1).astype(BF16)
    w_small = jnp.zeros((d, LANES), BF16).at[:, :4 * N_HEADS].set(w_l[:, COL_BETA:COL_Z].astype(BF16))
    x2 = x.reshape(t_lat, d)
    tiles_per_batch = seq // ROW_TILE
    p, small = _inproj(x2, ctx.reshape(batch * ctx_len, d), row(norm_mix_g), mod_row(0), mod_row(1),
                       w_main, w_small, row(sgu_ln_g), row(sgu_ln_b), tiles_per_batch)

    ctx_t, lat_t = ctx_len // PREP_TILE, seq // PREP_TILE
    starts = [b * ctx_t for b in range(batch)] + [batch * ctx_t + b * lat_t for b in range(batch)]
    ends = [(b + 1) * ctx_t - 1 for b in range(batch)] + [batch * ctx_t + (b + 1) * lat_t - 1 for b in range(batch)]
    conv_w8 = jnp.zeros((8, 3 * D_GDN), F32).at[:CONV_K].set(conv_w[0])
    pad_lanes = lambda v: jnp.zeros((1, LANES), F32).at[0, 2 * N_HEADS:4 * N_HEADS].set(v.reshape(-1))
    qkv, gb = _prep(p, small, conv_w8, pad_lanes(-jnp.exp(a_log[0])), pad_lanes(dt_bias[0]),
                    tuple(starts), tuple(ends))

    o_f, o_b = _gdn(qkv, gb, batch, ctx_len // GDN_CHUNK, seq // GDN_CHUNK)

    b_full = jnp.repeat(sgu_b[0].T, SGU_GROUP_DIM, axis=1).astype(F32)
    y_sgu = _sgu(p, sgu_w[0].astype(BF16), b_full, 1, t_lat // ROW_TILE)

    w_r = jnp.zeros((d, LANES), F32).at[:, :N_EXPERT_GROUPS].set(router_group_w[0]) \
        .at[:, N_EXPERT_GROUPS:N_EXPERT_GROUPS + N_EXPERTS].set(router_expert_w[0])
    b_r = jnp.zeros((1, LANES), F32).at[0, :N_EXPERT_GROUPS].set(router_group_b[0]) \
        .at[0, N_EXPERT_GROUPS:N_EXPERT_GROUPS + N_EXPERTS].set(router_expert_b[0])
    gng = gdn_norm_g[0].reshape(1, HEAD_DIM).astype(F32)
    xmid, h2, rgate = _merge(o_f, o_b, p, y_sgu, x2, w_branch_a[0].astype(BF16), w_branch_b[0].astype(BF16),
                             w_out[0].astype(BF16), gng, mod_row(2)[:batch], row(norm_ffn_g),
                             mod_row(3)[:batch], mod_row(4)[:batch], w_r, b_r, 1, tiles_per_batch)

    out = _moe(h2, rgate, expert_w1[0].astype(BF16), expert_w3[0].astype(BF16), expert_w2[0].astype(BF16),
               xmid, mod_row(5)[:batch], row(final_norm_g), seq // MOE_TILE)
    return out.reshape(batch, seq, d)
```

```python
import functools
import math

import jax
import jax.numpy as jnp
from jax import lax
from jax.experimental import pallas as pl
from jax.experimental.pallas import tpu as pltpu

F32 = jnp.float32
BF16 = jnp.bfloat16

D_MODEL = 1024
N_HEADS = 8
HEAD_DIM = 128
D_GDN = N_HEADS * HEAD_DIM
CONV_K = 5
GDN_CHUNK = 64
SGU_GROUPS = 8
SGU_GROUP_DIM = 128
D_SGU = SGU_GROUPS * SGU_GROUP_DIM
SGU_CHUNK = 128
N_EXPERT_GROUPS = 4
EXPERTS_PER_GROUP = 8
N_EXPERTS = N_EXPERT_GROUPS * EXPERTS_PER_GROUP
D_EXPERT = 256
N_MOD = 6
EPS = 1e-6
COL_BETA = 3 * D_GDN
COL_Z = COL_BETA + 4 * N_HEADS

LANES = 128
ROW_TILE = 512
PREP_TILE = 256
MOE_TILE = 1024
MOE_EXPERTS_PER_STEP = 4
MOE_BLOCK = 128
MOE_BLOCK_MULTIPLES = (4, 2, 1)
VMEM_LIMIT = 48 * 1024 * 1024
MOE_VMEM_LIMIT = 56 * 1024 * 1024
NEG_BIG = -1e30

PCOL_Q, PCOL_K, PCOL_V, PCOL_Z, PCOL_U, PCOL_SV, PCOL_GA, PCOL_GB = range(8)


def _mm(a, b):
    return jnp.dot(a.astype(BF16), b.astype(BF16), preferred_element_type=F32)


def _mm_nt(a, b):
    return lax.dot_general(a.astype(BF16), b.astype(BF16), (((1,), (1,)), ((), ())),
                           preferred_element_type=F32)


def _sigmoid(x):
    return 1.0 / (1.0 + jnp.exp(-x))


def _silu(x):
    return x * _sigmoid(x)


def _gelu_tanh(x):
    return 0.5 * x * (1.0 + jnp.tanh(math.sqrt(2.0 / math.pi) * (x + 0.044715 * (x * x * x))))


def _params(*sem):
    return pltpu.CompilerParams(dimension_semantics=sem, vmem_limit_bytes=VMEM_LIMIT)


def _adaln_kernel(c_ref, w_ref, b_ref, o_ref):
    o_ref[...] = _mm(_silu(c_ref[...]), w_ref[...]) + b_ref[...]


def _adaln(cond, w, b):
    n = w.shape[1]
    tn = 1536
    return pl.pallas_call(
        _adaln_kernel,
        grid=(n // tn,),
        in_specs=[pl.BlockSpec((8, D_MODEL), lambda j: (0, 0)),
                  pl.BlockSpec((D_MODEL, tn), lambda j: (0, j)),
                  pl.BlockSpec((1, tn), lambda j: (0, j))],
        out_specs=pl.BlockSpec((8, tn), lambda j: (0, j)),
        out_shape=jax.ShapeDtypeStruct((8, n), F32),
        compiler_params=_params("arbitrary"),
        name="adaln",
    )(cond, w, b.reshape(1, n))


def _inproj_kernel(x_ref, ctx_ref, g_ref, shift_ref, scale_ref, wm_ref, ws_ref, lng_ref, lnb_ref,
                   p_ref, small_ref, h_ref):
    i = pl.program_id(0)
    j = pl.program_id(1)

    def norm_mod(xv):
        y = xv * lax.rsqrt(jnp.mean(xv * xv, axis=-1, keepdims=True) + EPS) * g_ref[...]
        return y * (1.0 + scale_ref[0]) + shift_ref[0]

    @pl.when(j == 0)
    def _():
        @pl.when(i == 0)
        def _():
            h_ref[...] = norm_mod(ctx_ref[...]).astype(BF16)

        @pl.when(i > 0)
        def _():
            h_ref[...] = norm_mod(x_ref[...]).astype(BF16)

        small_ref[...] = jnp.dot(h_ref[...], ws_ref[...], preferred_element_type=F32)

    def raw(a):
        return a

    def gelu_ln(a):
        a = _gelu_tanh(a)
        mu = jnp.mean(a, axis=-1, keepdims=True)
        ac = a - mu
        var = jnp.mean(ac * ac, axis=-1, keepdims=True)
        return ac * lax.rsqrt(var + EPS) * lng_ref[...] + lnb_ref[...]

    epilogues = (raw, raw, raw, _silu, _gelu_tanh, gelu_ln, _sigmoid, _sigmoid)
    for step in range(len(epilogues) // 2):
        @pl.when(j == step)
        def _(step=step):
            for half in range(2):
                cs = slice(half * D_MODEL, (half + 1) * D_MODEL)
                a = jnp.dot(h_ref[...], wm_ref[:, cs], preferred_element_type=F32)
                p_ref[:, cs] = epilogues[2 * step + half](a).astype(BF16)


def _inproj(x2, ctx2, norm_g, shift3, scale3, w_main, w_small, ln_g, ln_b, tiles_per_batch):
    t_lat = x2.shape[0]
    n_lat = t_lat // ROW_TILE
    n_tiles = n_lat + 1
    tn = 2 * D_MODEL
    n_col = w_main.shape[1] // tn
    n_batch = shift3.shape[0] - 1
    sel = lambda i: jnp.where(i == 0, n_batch, (jnp.maximum(i, 1) - 1) // tiles_per_batch)
    vec = lambda: pl.BlockSpec((1, D_MODEL), lambda i, j: (0, 0))
    return pl.pallas_call(
        _inproj_kernel,
        grid=(n_tiles, n_col),
        in_specs=[pl.BlockSpec((ROW_TILE, D_MODEL), lambda i, j: (jnp.maximum(i, 1) - 1, 0)),
                  pl.BlockSpec((ROW_TILE, D_MODEL), lambda i, j: (0, 0)),
                  vec(),
                  pl.BlockSpec((1, 1, D_MODEL), lambda i, j: (sel(i), 0, 0)),
                  pl.BlockSpec((1, 1, D_MODEL), lambda i, j: (sel(i), 0, 0)),
                  pl.BlockSpec((D_MODEL, tn), lambda i, j: (0, j)),
                  pl.BlockSpec((D_MODEL, LANES), lambda i, j: (0, 0)),
                  vec(), vec()],
        out_specs=[pl.BlockSpec((ROW_TILE, tn), lambda i, j: (i, j)),
                   pl.BlockSpec((ROW_TILE, LANES), lambda i, j: (i, 0))],
        out_shape=[jax.ShapeDtypeStruct((n_tiles * ROW_TILE, n_col * tn), BF16),
                   jax.ShapeDtypeStruct((n_tiles * ROW_TILE, LANES), F32)],
        scratch_shapes=[pltpu.VMEM((ROW_TILE, D_MODEL), BF16)],
        compiler_params=_params("arbitrary", "arbitrary"),
        name="inproj",
    )(x2, ctx2, norm_g, shift3, scale3, w_main, w_small, ln_g, ln_b)


def _prep_kernel(pm_ref, pp_ref, pn_ref, cw_ref, small_ref, nega_ref, dtb_ref, qkv_ref, gb_ref, shift_ref,
                 *, first_tiles, last_tiles):
    r = pl.program_id(0)
    j = pl.program_id(1)
    tr = pm_ref.shape[0]
    is_first = functools.reduce(jnp.logical_or, [r == t for t in first_tiles])
    is_last = functools.reduce(jnp.logical_or, [r == t for t in last_tiles])
    keep_prev = jnp.where(is_first, 0.0, 1.0)
    keep_next = jnp.where(is_last, 0.0, 1.0)

    half = CONV_K // 2
    offsets = [o for o in range(-half, half + 1) if o != 0]

    @pl.when((r == 0) & (j == 0))
    def _():
        i0 = lax.broadcasted_iota(jnp.int32, (tr, tr), 0)
        i1 = lax.broadcasted_iota(jnp.int32, (tr, tr), 1)
        for m, o in enumerate(offsets):
            shift_ref[m * tr:(m + 1) * tr, :] = jnp.where(i1 == i0 + o, 1.0, 0.0).astype(BF16)

    x = pm_ref[...]
    shifted = jnp.dot(shift_ref[...], x, preferred_element_type=F32)
    acc = cw_ref[half:half + 1, :] * x.astype(F32)
    for m, o in enumerate(offsets):
        acc = acc + cw_ref[half + o:half + o + 1, :] * shifted[m * tr:(m + 1) * tr, :]

    prev = pp_ref[...].astype(F32)[8:16, :] * keep_prev
    nxt = pn_ref[...].astype(F32)[0:8, :] * keep_next
    sub = lax.broadcasted_iota(jnp.int32, prev.shape, 0)
    top = jnp.zeros_like(prev)
    bot = jnp.zeros_like(prev)
    for o in range(1, half + 1):
        top = top + cw_ref[half - o:half - o + 1, :] * jnp.where(sub < o, pltpu.roll(prev, o, 0), 0.0)
        bot = bot + cw_ref[half + o:half + o + 1, :] * jnp.where(sub >= 8 - o, pltpu.roll(nxt, 8 - o, 0), 0.0)
    acc = jnp.concatenate([acc[0:8] + top, acc[8:tr - 8], acc[tr - 8:tr] + bot], axis=0)
    y = _silu(acc)

    def head_l2(scale):
        for h in range(N_HEADS):
            yh = y[:, h * HEAD_DIM:(h + 1) * HEAD_DIM]
            inv = lax.rsqrt(jnp.sum(yh * yh, axis=-1, keepdims=True) + EPS) * scale
            qkv_ref[:, h * HEAD_DIM:(h + 1) * HEAD_DIM] = (yh * inv).astype(BF16)

    @pl.when(j == PCOL_Q)
    def _():
        head_l2(HEAD_DIM ** -0.5)

    @pl.when(j == PCOL_K)
    def _():
        head_l2(1.0)

    @pl.when(j == PCOL_V)
    def _():
        qkv_ref[...] = y.astype(BF16)

    @pl.when(j == 0)
    def _():
        s = small_ref[...]
        lane = lax.broadcasted_iota(jnp.int32, s.shape, 1)
        beta = _sigmoid(s)
        z = s + dtb_ref[...]
        softplus = jnp.maximum(z, 0.0) + jnp.log(1.0 + jnp.exp(-jnp.abs(z)))
        g = nega_ref[...] * softplus
        gb_ref[...] = jnp.where(lane < 2 * N_HEADS, beta, jnp.where(lane < 4 * N_HEADS, g, 0.0))


def _prep(p, small, conv_w8, nega, dtb, first_tiles, last_tiles):
    ta = p.shape[0]
    n_tiles = ta // PREP_TILE
    sub = PREP_TILE // 16
    n_sub = ta // 16
    kern = functools.partial(_prep_kernel, first_tiles=first_tiles, last_tiles=last_tiles)
    return pl.pallas_call(
        kern,
        grid=(n_tiles, 3),
        in_specs=[pl.BlockSpec((PREP_TILE, D_MODEL), lambda r, j: (r, j)),
                  pl.BlockSpec((16, D_MODEL), lambda r, j: (jnp.maximum(r * sub - 1, 0), j)),
                  pl.BlockSpec((16, D_MODEL), lambda r, j: (jnp.minimum((r + 1) * sub, n_sub - 1), j)),
                  pl.BlockSpec((8, D_MODEL), lambda r, j: (0, j)),
                  pl.BlockSpec((PREP_TILE, LANES), lambda r, j: (r, 0)),
                  pl.BlockSpec((1, LANES), lambda r, j: (0, 0)),
                  pl.BlockSpec((1, LANES), lambda r, j: (0, 0))],
        out_specs=[pl.BlockSpec((PREP_TILE, D_MODEL), lambda r, j: (r, j)),
                   pl.BlockSpec((PREP_TILE, LANES), lambda r, j: (r, 0))],
        out_shape=[jax.ShapeDtypeStruct((ta, 3 * D_MODEL), BF16),
                   jax.ShapeDtypeStruct((ta, LANES), F32)],
        scratch_shapes=[pltpu.VMEM(((CONV_K - 1) * PREP_TILE, PREP_TILE), BF16)],
        compiler_params=_params("arbitrary", "arbitrary"),
        name="prep",
    )(p, p, p, conv_w8, small, nega, dtb)


def _gdn_decays(d, gb):
    c = GDN_CHUNK
    row = lax.broadcasted_iota(jnp.int32, (c, c), 0)
    col = lax.broadcasted_iota(jnp.int32, (c, c), 1)
    incl = row >= col if d == 0 else row <= col
    lane = lax.broadcasted_iota(jnp.int32, gb.shape, 1)
    g_only = jnp.where(lane >= 2 * N_HEADS, jnp.where(lane < 4 * N_HEADS, gb, 0.0), 0.0)
    tri = jnp.where(incl, 1.0, 0.0).astype(BF16)
    g_hi = g_only.astype(BF16)
    g_r1 = g_only - g_hi.astype(F32)
    g_mid = g_r1.astype(BF16)
    g_lo = (g_r1 - g_mid.astype(F32)).astype(BF16)
    dot = lambda a, b: jnp.dot(a, b, preferred_element_type=F32)
    gcum = dot(tri, g_hi) + dot(tri, g_mid) + dot(tri, g_lo)
    g_end = gcum[c - 1:c, :] if d == 0 else gcum[0:1, :]
    return gcum, gcum.T, jnp.exp(gcum), jnp.exp(g_end - gcum), jnp.exp(g_end)


def _gdn_kernel(qf, kf, vf, gbf, qb, kb, vb, gbb, of_ref, ob_ref, s_ref):
    @pl.when(pl.program_id(1) == 0)
    def _():
        s_ref[...] = jnp.zeros_like(s_ref)

    c = GDN_CHUNK
    row = lax.broadcasted_iota(jnp.int32, (c, c), 0)
    col = lax.broadcasted_iota(jnp.int32, (c, c), 1)
    eye = jnp.where(row == col, 1.0, 0.0)
    masks = ((row >= col, row > col), (row <= col, row < col))
    refs = ((qf, kf, vf, gbf, of_ref), (qb, kb, vb, gbb, ob_ref))
    gbv = [refs[d][3][...] for d in range(2)]
    dec = [_gdn_decays(d, gbv[d]) for d in range(2)]
    chains = [(d, h) for d in range(2) for h in range(N_HEADS)]

    st = []
    for d, h in chains:
        q_ref, k_ref, v_ref, _, _ = refs[d]
        hs = slice(h * HEAD_DIM, (h + 1) * HEAD_DIM)
        q, k, v = q_ref[:, hs], k_ref[:, hs], v_ref[:, hs]
        both = _mm_nt(jnp.concatenate([q, k], axis=0), k)
        st.append(dict(q=q.astype(F32), k=k.astype(F32), v=v.astype(F32), qk=both[:c], kk=both[c:]))

    for (d, h), e in zip(chains, st):
        cb = d * N_HEADS + h
        cg = 2 * N_HEADS + cb
        gcum, gcum_t, exp_g, exp_rest, exp_end = dec[d]
        incl, strict = masks[d]
        e["beta"] = gbv[d][:, cb:cb + 1]
        diff = gcum[:, cg:cg + 1] - gcum_t[cg:cg + 1, :]
        decay = jnp.where(incl, jnp.exp(jnp.minimum(diff, 0.0)), 0.0)
        e["y"] = jnp.where(strict, -(e["beta"] * e["kk"] * decay), 0.0)
        e["qk"] = e["qk"] * decay
        e["eg"] = exp_g[:, cg:cg + 1]
        e["er"] = exp_rest[:, cg:cg + 1]
        e["ee"] = exp_end[:, cg:cg + 1]

    levels = range(GDN_CHUNK.bit_length() - 1)
    sels = []
    for d in range(2):
        inner, outer = (col, row) if d == 0 else (row, col)
        sels.append([((outer >> lvl) & 1 == 1) & ((inner >> lvl) == (outer >> lvl) - 1) for lvl in levels])
    for (d, h), e in zip(chains, st):
        e["t"] = eye + jnp.where(sels[d][0], e["y"], 0.0)
    for lvl in levels[1:]:
        for (d, h), e in zip(chains, st):
            e["yd"] = _mm(jnp.where(sels[d][lvl], e["y"], 0.0), e["t"])
        for e in st:
            e["t"] = e["t"] + _mm(e["t"], e["yd"])

    for e in st:
        kb_ = e["k"] * e["beta"]
        uw = _mm(e["t"], jnp.concatenate([e["v"] * e["beta"], kb_ * e["eg"]], axis=-1))
        e["u"], e["w"] = uw[:, :HEAD_DIM], uw[:, HEAD_DIM:]

    for (d, h), e in zip(chains, st):
        e["s"] = s_ref[d, h]
        ws = _mm(jnp.concatenate([e["w"], e["q"] * e["eg"]], axis=0), e["s"])
        e["v_new"] = e["u"] - ws[:c]
        e["o"] = ws[c:]

    for (d, h), e in zip(chains, st):
        k_dec_t = (e["k"] * e["er"]).T
        out = _mm(jnp.concatenate([e["qk"], k_dec_t], axis=0), e["v_new"])
        refs[d][4][:, h * HEAD_DIM:(h + 1) * HEAD_DIM] = (e["o"] + out[:c]).astype(BF16)
        s_ref[d, h] = e["s"] * e["ee"] + out[c:]


def _gdn(qkv, gb, batch, ctx_chunks, lat_chunks):
    ta = qkv.shape[0]
    n_steps = ctx_chunks + lat_chunks
    lat0 = batch * ctx_chunks

    def fwd_blk(b, s):
        return jnp.where(s < ctx_chunks, b * ctx_chunks + s, lat0 + b * lat_chunks + (s - ctx_chunks))

    def bwd_blk(b, s):
        return jnp.where(s < ctx_chunks, b * ctx_chunks + (ctx_chunks - 1 - s),
                         lat0 + b * lat_chunks + (lat_chunks - 1 - (s - ctx_chunks)))

    def specs(blk):
        col = lambda j: pl.BlockSpec((GDN_CHUNK, D_MODEL), lambda b, s: (blk(b, s), j))
        return [col(0), col(1), col(2), pl.BlockSpec((GDN_CHUNK, LANES), lambda b, s: (blk(b, s), 0))]

    out = lambda blk: pl.BlockSpec((GDN_CHUNK, D_MODEL), lambda b, s: (blk(b, s), 0))
    return pl.pallas_call(
        _gdn_kernel,
        grid=(batch, n_steps),
        in_specs=specs(fwd_blk) + specs(bwd_blk),
        out_specs=[out(fwd_blk), out(bwd_blk)],
        out_shape=[jax.ShapeDtypeStruct((ta, D_MODEL), BF16)] * 2,
        scratch_shapes=[pltpu.VMEM((2, N_HEADS, HEAD_DIM, HEAD_DIM), F32)],
        compiler_params=_params("arbitrary", "arbitrary"),
        name="gdn",
    )(qkv, qkv, qkv, gb, qkv, qkv, qkv, gb)


def _sgu_kernel(u_ref, v_ref, w_ref, b_ref, o_ref):
    for ch in range(u_ref.shape[0] // SGU_CHUNK):
        rs = slice(ch * SGU_CHUNK, (ch + 1) * SGU_CHUNK)
        for g in range(SGU_GROUPS):
            cs = slice(g * SGU_GROUP_DIM, (g + 1) * SGU_GROUP_DIM)
            mixed = jnp.dot(w_ref[g], v_ref[rs, cs], preferred_element_type=F32) + b_ref[:, cs]
            o_ref[rs, cs] = (u_ref[rs, cs].astype(F32) * mixed).astype(BF16)


def _sgu(p, w_s, b_full, lat_tile0, n_lat_tiles):
    return pl.pallas_call(
        _sgu_kernel,
        grid=(n_lat_tiles,),
        in_specs=[pl.BlockSpec((ROW_TILE, D_MODEL), lambda i: (i + lat_tile0, PCOL_U)),
                  pl.BlockSpec((ROW_TILE, D_MODEL), lambda i: (i + lat_tile0, PCOL_SV)),
                  pl.BlockSpec((SGU_GROUPS, SGU_CHUNK, SGU_CHUNK), lambda i: (0, 0, 0)),
                  pl.BlockSpec((SGU_CHUNK, D_MODEL), lambda i: (0, 0))],
        out_specs=pl.BlockSpec((ROW_TILE, D_MODEL), lambda i: (i, 0)),
        out_shape=jax.ShapeDtypeStruct((n_lat_tiles * ROW_TILE, D_MODEL), BF16),
        compiler_params=_params("arbitrary"),
        name="sgu",
    )(p, p, w_s, b_full)


def _merge_kernel(of_ref, ob_ref, z_ref, ysgu_ref, ga_ref, gb_ref, x_ref, wa_ref, wb_ref, wo_ref,
                  gng_ref, gate_ref, nfg_ref, shift_ref, scale_ref, wr_ref, br_ref,
                  xmid_ref, h2_ref, rgate_ref, ygdn_ref):
    o = of_ref[...].astype(F32) + ob_ref[...].astype(F32)
    for h in range(N_HEADS):
        hs = slice(h * HEAD_DIM, (h + 1) * HEAD_DIM)
        oh = o[:, hs]
        inv = lax.rsqrt(jnp.mean(oh * oh, axis=-1, keepdims=True) + EPS)
        ygdn_ref[:, hs] = (oh * inv * gng_ref[...] * z_ref[:, hs].astype(F32)).astype(BF16)
    ya = jnp.dot(ygdn_ref[...], wa_ref[...], preferred_element_type=F32)
    yb = jnp.dot(ysgu_ref[...], wb_ref[...], preferred_element_type=F32)
    merged = ga_ref[...].astype(F32) * ya + gb_ref[...].astype(F32) * yb
    mix = jnp.dot(merged.astype(BF16), wo_ref[...], preferred_element_type=F32)
    xm = x_ref[...] + gate_ref[0] * mix
    xmid_ref[...] = xm
    hn = xm * lax.rsqrt(jnp.mean(xm * xm, axis=-1, keepdims=True) + EPS) * nfg_ref[...]
    h2 = hn * (1.0 + scale_ref[0]) + shift_ref[0]
    h2_ref[...] = h2.astype(BF16)

    h_hi = h2.astype(BF16)
    h_lo = (h2 - h_hi.astype(F32)).astype(BF16)
    wr = wr_ref[...]
    w_hi = wr.astype(BF16)
    w_lo = (wr - w_hi.astype(F32)).astype(BF16)
    dot = lambda a, b: jnp.dot(a, b, preferred_element_type=F32)
    lg = dot(h_hi, w_hi) + dot(h_lo, w_hi) + dot(h_hi, w_lo) + br_ref[...]

    lane = lax.broadcasted_iota(jnp.int32, lg.shape, 1).astype(F32)
    far = float(LANES)
    gl = jnp.where(lane < N_EXPERT_GROUPS, lg, NEG_BIG)
    gmax = jnp.max(gl, axis=-1, keepdims=True)
    p_g = 1.0 / jnp.sum(jnp.exp(gl - gmax), axis=-1, keepdims=True)
    grp = jnp.min(jnp.where(gl == gmax, lane, far), axis=-1, keepdims=True)
    lo = N_EXPERT_GROUPS + EXPERTS_PER_GROUP * grp
    in_grp = jnp.where(lane >= lo, jnp.where(lane < lo + EXPERTS_PER_GROUP, 1.0, 0.0), 0.0)
    el = jnp.where(in_grp > 0.0, lg, NEG_BIG)
    m1 = jnp.max(el, axis=-1, keepdims=True)
    i1 = jnp.min(jnp.where(el == m1, lane, far), axis=-1, keepdims=True)
    el2 = jnp.where(lane == i1, NEG_BIG, el)
    m2 = jnp.max(el2, axis=-1, keepdims=True)
    i2 = jnp.min(jnp.where(el2 == m2, lane, far), axis=-1, keepdims=True)
    t = jnp.exp(m2 - m1)
    w1 = p_g / (1.0 + t)
    w2 = w1 * t
    rgate_ref[...] = jnp.where(lane == 0.0, grp, jnp.where(lane == i1, w1, 0.0) + jnp.where(lane == i2, w2, 0.0))


def _merge(o_f, o_b, p, y_sgu, x2, w_a, w_b, w_o, gng, gate3, nfg, shift3, scale3, w_r, b_r,
           lat_tile0, tiles_per_batch):
    t_lat = x2.shape[0]
    n_tiles = t_lat // ROW_TILE
    lat = lambda c: pl.BlockSpec((ROW_TILE, D_MODEL), lambda i: (i + lat_tile0, c))
    own = lambda: pl.BlockSpec((ROW_TILE, D_MODEL), lambda i: (i, 0))
    mat = lambda: pl.BlockSpec((D_MODEL, D_MODEL), lambda i: (0, 0))
    vec = lambda: pl.BlockSpec((1, D_MODEL), lambda i: (0, 0))
    per_b = lambda: pl.BlockSpec((1, 1, D_MODEL), lambda i: (i // tiles_per_batch, 0, 0))
    return pl.pallas_call(
        _merge_kernel,
        grid=(n_tiles,),
        in_specs=[lat(0), lat(0), lat(PCOL_Z), own(), lat(PCOL_GA), lat(PCOL_GB), own(),
                  mat(), mat(), mat(), pl.BlockSpec((1, HEAD_DIM), lambda i: (0, 0)),
                  per_b(), vec(), per_b(), per_b(),
                  pl.BlockSpec((D_MODEL, LANES), lambda i: (0, 0)),
                  pl.BlockSpec((1, LANES), lambda i: (0, 0))],
        out_specs=[own(), own(), pl.BlockSpec((ROW_TILE, LANES), lambda i: (i, 0))],
        out_shape=[jax.ShapeDtypeStruct((t_lat, D_MODEL), F32),
                   jax.ShapeDtypeStruct((t_lat, D_MODEL), BF16),
                   jax.ShapeDtypeStruct((t_lat, LANES), F32)],
        scratch_shapes=[pltpu.VMEM((ROW_TILE, D_MODEL), BF16)],
        compiler_params=_params("arbitrary"),
        name="merge",
    )(o_f, o_b, p, y_sgu, p, p, x2, w_a, w_b, w_o, gng, gate3, nfg, shift3, scale3, w_r, b_r)


def _split3(x):
    hi = x.astype(BF16)
    r1 = x - hi.astype(F32)
    mid = r1.astype(BF16)
    return hi, mid, (r1 - mid.astype(F32)).astype(BF16)


def _moe_kernel(h_ref, rg_ref, w1_ref, w3_ref, w2_ref, xmid_ref, gate_ref, fng_ref, o_ref,
                before_ref, destc_ref, xs_ref, ys_ref, gs_ref, start_ref):
    i, g, hf = pl.program_id(0), pl.program_id(1), pl.program_id(2)
    tm = h_ref.shape[0]
    dot = lambda a, b: jnp.dot(a, b, preferred_element_type=F32)

    @pl.when((i == 0) & (g == 0) & (hf == 0))
    def _():
        r = lax.broadcasted_iota(jnp.int32, (tm, tm), 0)
        c = lax.broadcasted_iota(jnp.int32, (tm, tm), 1)
        before_ref[...] = jnp.where(r < c, 1.0, 0.0).astype(BF16)

    @pl.when((g == 0) & (hf == 0))
    def _():
        rg = rg_ref[...]
        sub = lax.broadcasted_iota(jnp.int32, (8, tm), 0).astype(F32)
        m_rows = jnp.where(rg.T[0:1, :] == sub, 1.0, 0.0)
        total_r = jnp.sum(m_rows, axis=1, keepdims=True)
        start_r = jnp.zeros_like(total_r)
        for gg in range(N_EXPERT_GROUPS - 1):
            start_r = start_r + jnp.where(sub[:, 0:1] > gg, total_r[gg:gg + 1, :], 0.0)
        rank_r = dot(m_rows.astype(BF16), before_ref[...])
        dest_r = jnp.sum(jnp.where(m_rows > 0.0, rank_r + start_r, 0.0), axis=0, keepdims=True)
        destc_ref[...] = jnp.broadcast_to(dest_r, (LANES, tm)).T
        acc = jnp.int32(0)
        for gg in range(N_EXPERT_GROUPS):
            start_ref[gg] = acc
            acc = acc + total_r[gg, 0].astype(jnp.int32)
        start_ref[N_EXPERT_GROUPS] = acc
        slot = lax.broadcasted_iota(jnp.int32, (tm, tm), 0).astype(F32)
        perm = jnp.where(dest_r == slot, 1.0, 0.0).astype(BF16)
        xs_ref[...] = dot(perm, h_ref[...]).astype(BF16)
        g_hi, g_mid, _ = _split3(rg)
        gs_ref[...] = dot(perm, g_hi) + dot(perm, g_mid)
        ys_ref[...] = jnp.zeros_like(ys_ref)

    bs = MOE_BLOCK
    first = start_ref[g] // bs
    n_unit = (start_ref[g + 1] + (bs - 1)) // bs - first

    def experts(row0, size):
        rows = pl.ds(pl.multiple_of(row0, bs), size)
        x = xs_ref[rows, :]
        gs = gs_ref[rows, :]
        lane = lax.broadcasted_iota(jnp.int32, gs.shape, 1)
        y = ys_ref[rows, :]
        for e in range(MOE_EXPERTS_PER_STEP):
            col = N_EXPERT_GROUPS + g * EXPERTS_PER_GROUP + hf * MOE_EXPERTS_PER_STEP + e
            gate = jnp.sum(jnp.where(lane == col, gs, 0.0), axis=-1, keepdims=True)
            hid = _silu(dot(x, w1_ref[e])) * dot(x, w3_ref[e]) * gate
            y = y + dot(hid.astype(BF16), w2_ref[e])
        ys_ref[rows, :] = y

    big = MOE_BLOCK_MULTIPLES[0]

    def big_block(k, carry):
        experts((first + k * big) * bs, big * bs)
        return carry

    lax.fori_loop(0, n_unit // big, big_block, 0)
    done = first + (n_unit // big) * big
    for m in MOE_BLOCK_MULTIPLES[1:]:
        take = (n_unit & m) != 0

        @pl.when(take)
        def _(done=done, m=m):
            experts(done * bs, m * bs)

        done = done + jnp.where(take, m, 0)

    @pl.when((g == pl.num_programs(1) - 1) & (hf == pl.num_programs(2) - 1))
    def _():
        slot = lax.broadcasted_iota(jnp.int32, (tm, tm), 1).astype(F32)
        perm_t = jnp.where(destc_ref[:, 0:1] == slot, 1.0, 0.0).astype(BF16)
        xo = xmid_ref[...] + gate_ref[0] * dot(perm_t, ys_ref[...].astype(BF16))
        o_ref[...] = xo * lax.rsqrt(jnp.mean(xo * xo, axis=-1, keepdims=True) + EPS) * fng_ref[...]


def _moe(h2, rgate, w1, w3, w2, xmid, gate3, fng, tiles_per_batch):
    t_lat = h2.shape[0]
    n_tiles = t_lat // MOE_TILE
    eps = MOE_EXPERTS_PER_STEP
    halves = EXPERTS_PER_GROUP // eps
    row = lambda w: pl.BlockSpec((MOE_TILE, w), lambda i, g, hf: (i, 0))
    wspec = lambda a, b: pl.BlockSpec((eps, a, b), lambda i, g, hf: (g * halves + hf, 0, 0))
    return pl.pallas_call(
        _moe_kernel,
        grid=(n_tiles, N_EXPERT_GROUPS, halves),
        in_specs=[row(D_MODEL), row(LANES),
                  wspec(D_MODEL, D_EXPERT), wspec(D_MODEL, D_EXPERT), wspec(D_EXPERT, D_MODEL),
                  row(D_MODEL),
                  pl.BlockSpec((1, 1, D_MODEL), lambda i, g, hf: (i // tiles_per_batch, 0, 0)),
                  pl.BlockSpec((1, D_MODEL), lambda i, g, hf: (0, 0))],
        out_specs=row(D_MODEL),
        out_shape=jax.ShapeDtypeStruct((t_lat, D_MODEL), F32),
        scratch_shapes=[pltpu.VMEM((MOE_TILE, MOE_TILE), BF16),
                        pltpu.VMEM((MOE_TILE, LANES), F32),
                        pltpu.VMEM((MOE_TILE, D_MODEL), BF16),
                        pltpu.VMEM((MOE_TILE, D_MODEL), F32),
                        pltpu.VMEM((MOE_TILE, LANES), F32),
                        pltpu.SMEM((N_EXPERT_GROUPS + 1,), jnp.int32)],
        compiler_params=pltpu.CompilerParams(dimension_semantics=("arbitrary",) * 3,
                                             vmem_limit_bytes=MOE_VMEM_LIMIT),
        name="moe",
    )(h2, rgate, w1, w3, w2, xmid, gate3, fng)


def kernel(x, c, ctx, c_ctx, ada_w, ada_b, norm_mix_g, w_in, conv_w, a_log, dt_bias, gdn_norm_g, sgu_ln_g, sgu_ln_b, sgu_w, sgu_b, w_branch_a, w_branch_b, w_out, norm_ffn_g, router_group_w, router_group_b, router_expert_w, router_expert_b, expert_w1, expert_w3, expert_w2, final_norm_g):
    batch, seq, d = x.shape
    ctx_len = ctx.shape[1]
    assert d == D_MODEL and ada_w.shape[0] == 1, "single-layer block with D_MODEL channels"
    assert batch * ctx_len == ROW_TILE, "context rows of all samples form one row tile"
    assert seq % MOE_TILE == 0 and ctx_len % PREP_TILE == 0 and batch + 1 <= 8
    t_lat = batch * seq
    row = lambda v: v.reshape(1, -1).astype(F32)

    cond = jnp.zeros((8, d), F32).at[:batch].set(c).at[batch].set(c_ctx)
    mod = _adaln(cond, ada_w[0], ada_b[0]).reshape(8, N_MOD, d)
    mod_row = lambda k: mod[:batch + 1, k].reshape(batch + 1, 1, d)

    w_l = w_in[0]
    w_main = jnp.concatenate([w_l[:, :COL_BETA], w_l[:, COL_Z:]], axis=1).astype(BF16)
    w_small = jnp.zeros((d, LANES), BF16).at[:, :4 * N_HEADS].set(w_l[:, COL_BETA:COL_Z].astype(BF16))
    x2 = x.reshape(t_lat, d)
    tiles_per_batch = seq // ROW_TILE
    p, small = _inproj(x2, ctx.reshape(batch * ctx_len, d), row(norm_mix_g), mod_row(0), mod_row(1),
                       w_main, w_small, row(sgu_ln_g), row(sgu_ln_b), tiles_per_batch)

    ctx_t, lat_t = ctx_len // PREP_TILE, seq // PREP_TILE
    starts = [b * ctx_t for b in range(batch)] + [batch * ctx_t + b * lat_t for b in range(batch)]
    ends = [(b + 1) * ctx_t - 1 for b in range(batch)] + [batch * ctx_t + (b + 1) * lat_t - 1 for b in range(batch)]
    conv_w8 = jnp.zeros((8, 3 * D_GDN), F32).at[:CONV_K].set(conv_w[0])
    pad_lanes = lambda v: jnp.zeros((1, LANES), F32).at[0, 2 * N_HEADS:4 * N_HEADS].set(v.reshape(-1))
    qkv, gb = _prep(p, small, conv_w8, pad_lanes(-jnp.exp(a_log[0])), pad_lanes(dt_bias[0]),
                    tuple(starts), tuple(ends))

    o_f, o_b = _gdn(qkv, gb, batch, ctx_len // GDN_CHUNK, seq // GDN_CHUNK)

    b_full = jnp.repeat(sgu_b[0].T, SGU_GROUP_DIM, axis=1).astype(F32)
    y_sgu = _sgu(p, sgu_w[0].astype(BF16), b_full, 1, t_lat // ROW_TILE)

    w_r = jnp.zeros((d, LANES), F32).at[:, :N_EXPERT_GROUPS].set(router_group_w[0]) \
        .at[:, N_EXPERT_GROUPS:N_EXPERT_GROUPS + N_EXPERTS].set(router_expert_w[0])
    b_r = jnp.zeros((1, LANES), F32).at[0, :N_EXPERT_GROUPS].set(router_group_b[0]) \
        .at[0, N_EXPERT_GROUPS:N_EXPERT_GROUPS + N_EXPERTS].set(router_expert_b[0])
    gng = gdn_norm_g[0].reshape(1, HEAD_DIM).astype(F32)
    xmid, h2, rgate = _merge(o_f, o_b, p, y_sgu, x2, w_branch_a[0].astype(BF16), w_branch_b[0].astype(BF16),
                             w_out[0].astype(BF16), gng, mod_row(2)[:batch], row(norm_ffn_g),
                             mod_row(3)[:batch], mod_row(4)[:batch], w_r, b_r, 1, tiles_per_batch)

    out = _moe(h2, rgate, expert_w1[0].astype(BF16), expert_w3[0].astype(BF16), expert_w2[0].astype(BF16),
               xmid, mod_row(5)[:batch], row(final_norm_g), seq // MOE_TILE)
    return out.reshape(batch, seq, d)
```

```python
import functools
import math

import jax
import jax.numpy as jnp
from jax import lax
from jax.experimental import pallas as pl
from jax.experimental.pallas import tpu as pltpu

F32 = jnp.float32
BF16 = jnp.bfloat16

D_MODEL = 1024
N_HEADS = 8
HEAD_DIM = 128
D_GDN = N_HEADS * HEAD_DIM
CONV_K = 5
GDN_CHUNK = 64
GDN_STEP_CHUNKS = 2
SGU_GROUPS = 8
SGU_GROUP_DIM = 128
D_SGU = SGU_GROUPS * SGU_GROUP_DIM
SGU_CHUNK = 128
N_EXPERT_GROUPS = 4
EXPERTS_PER_GROUP = 8
N_EXPERTS = N_EXPERT_GROUPS * EXPERTS_PER_GROUP
D_EXPERT = 256
N_MOD = 6
EPS = 1e-6
COL_BETA = 3 * D_GDN
COL_Z = COL_BETA + 4 * N_HEADS

LANES = 128
ROW_TILE = 512
PREP_TILE = 256
MOE_TILE = 1024
MOE_EXPERTS_PER_STEP = 4
MOE_BLOCK = 128
MOE_BLOCK_MULTIPLES = (4, 2, 1)
VMEM_LIMIT = 48 * 1024 * 1024
MOE_VMEM_LIMIT = 56 * 1024 * 1024
NEG_BIG = -1e30

PCOL_Q, PCOL_K, PCOL_V, PCOL_Z, PCOL_U, PCOL_SV, PCOL_GA, PCOL_GB = range(8)


def _mm(a, b):
    return jnp.dot(a.astype(BF16), b.astype(BF16), preferred_element_type=F32)


def _mm_nt(a, b):
    return lax.dot_general(a.astype(BF16), b.astype(BF16), (((1,), (1,)), ((), ())),
                           preferred_element_type=F32)


def _sigmoid(x):
    return 1.0 / (1.0 + jnp.exp(-x))


def _silu(x):
    return x * _sigmoid(x)


def _gelu_tanh(x):
    return 0.5 * x * (1.0 + jnp.tanh(math.sqrt(2.0 / math.pi) * (x + 0.044715 * (x * x * x))))


def _params(*sem):
    return pltpu.CompilerParams(dimension_semantics=sem, vmem_limit_bytes=VMEM_LIMIT)


def _adaln_kernel(c_ref, w_ref, b_ref, o_ref):
    o_ref[...] = _mm(_silu(c_ref[...]), w_ref[...]) + b_ref[...]


def _adaln(cond, w, b):
    n = w.shape[1]
    tn = 1536
    return pl.pallas_call(
        _adaln_kernel,
        grid=(n // tn,),
        in_specs=[pl.BlockSpec((8, D_MODEL), lambda j: (0, 0)),
                  pl.BlockSpec((D_MODEL, tn), lambda j: (0, j)),
                  pl.BlockSpec((1, tn), lambda j: (0, j))],
        out_specs=pl.BlockSpec((8, tn), lambda j: (0, j)),
        out_shape=jax.ShapeDtypeStruct((8, n), F32),
        compiler_params=_params("arbitrary"),
        name="adaln",
    )(cond, w, b.reshape(1, n))


def _inproj_kernel(x_ref, ctx_ref, g_ref, shift_ref, scale_ref, wm_ref, ws_ref, lng_ref, lnb_ref,
                   p_ref, small_ref, h_ref):
    i = pl.program_id(0)
    j = pl.program_id(1)

    def norm_mod(xv):
        y = xv * lax.rsqrt(jnp.mean(xv * xv, axis=-1, keepdims=True) + EPS) * g_ref[...]
        return y * (1.0 + scale_ref[0]) + shift_ref[0]

    @pl.when(j == 0)
    def _():
        @pl.when(i == 0)
        def _():
            h_ref[...] = norm_mod(ctx_ref[...]).astype(BF16)

        @pl.when(i > 0)
        def _():
            h_ref[...] = norm_mod(x_ref[...]).astype(BF16)

        small_ref[...] = jnp.dot(h_ref[...], ws_ref[...], preferred_element_type=F32)

    def raw(a):
        return a

    def gelu_ln(a):
        a = _gelu_tanh(a)
        mu = jnp.mean(a, axis=-1, keepdims=True)
        ac = a - mu
        var = jnp.mean(ac * ac, axis=-1, keepdims=True)
        return ac * lax.rsqrt(var + EPS) * lng_ref[...] + lnb_ref[...]

    epilogues = (raw, raw, raw, _silu, _gelu_tanh, gelu_ln, _sigmoid, _sigmoid)
    for step in range(len(epilogues) // 2):
        @pl.when(j == step)
        def _(step=step):
            for half in range(2):
                cs = slice(half * D_MODEL, (half + 1) * D_MODEL)
                a = jnp.dot(h_ref[...], wm_ref[:, cs], preferred_element_type=F32)
                p_ref[:, cs] = epilogues[2 * step + half](a).astype(BF16)


def _inproj(x2, ctx2, norm_g, shift3, scale3, w_main, w_small, ln_g, ln_b, tiles_per_batch):
    t_lat = x2.shape[0]
    n_lat = t_lat // ROW_TILE
    n_tiles = n_lat + 1
    tn = 2 * D_MODEL
    n_col = w_main.shape[1] // tn
    n_batch = shift3.shape[0] - 1
    sel = lambda i: jnp.where(i == 0, n_batch, (jnp.maximum(i, 1) - 1) // tiles_per_batch)
    vec = lambda: pl.BlockSpec((1, D_MODEL), lambda i, j: (0, 0))
    return pl.pallas_call(
        _inproj_kernel,
        grid=(n_tiles, n_col),
        in_specs=[pl.BlockSpec((ROW_TILE, D_MODEL), lambda i, j: (jnp.maximum(i, 1) - 1, 0)),
                  pl.BlockSpec((ROW_TILE, D_MODEL), lambda i, j: (0, 0)),
                  vec(),
                  pl.BlockSpec((1, 1, D_MODEL), lambda i, j: (sel(i), 0, 0)),
                  pl.BlockSpec((1, 1, D_MODEL), lambda i, j: (sel(i), 0, 0)),
                  pl.BlockSpec((D_MODEL, tn), lambda i, j: (0, j)),
                  pl.BlockSpec((D_MODEL, LANES), lambda i, j: (0, 0)),
                  vec(), vec()],
        out_specs=[pl.BlockSpec((ROW_TILE, tn), lambda i, j: (i, j)),
                   pl.BlockSpec((ROW_TILE, LANES), lambda i, j: (i, 0))],
        out_shape=[jax.ShapeDtypeStruct((n_tiles * ROW_TILE, n_col * tn), BF16),
                   jax.ShapeDtypeStruct((n_tiles * ROW_TILE, LANES), F32)],
        scratch_shapes=[pltpu.VMEM((ROW_TILE, D_MODEL), BF16)],
        compiler_params=_params("arbitrary", "arbitrary"),
        name="inproj",
    )(x2, ctx2, norm_g, shift3, scale3, w_main, w_small, ln_g, ln_b)


def _prep_kernel(pm_ref, pp_ref, pn_ref, cw_ref, small_ref, nega_ref, dtb_ref, qkv_ref, gb_ref, shift_ref,
                 *, first_tiles, last_tiles):
    r = pl.program_id(0)
    j = pl.program_id(1)
    tr = pm_ref.shape[0]
    is_first = functools.reduce(jnp.logical_or, [r == t for t in first_tiles])
    is_last = functools.reduce(jnp.logical_or, [r == t for t in last_tiles])
    keep_prev = jnp.where(is_first, 0.0, 1.0)
    keep_next = jnp.where(is_last, 0.0, 1.0)

    half = CONV_K // 2
    offsets = [o for o in range(-half, half + 1) if o != 0]

    @pl.when((r == 0) & (j == 0))
    def _():
        i0 = lax.broadcasted_iota(jnp.int32, (tr, tr), 0)
        i1 = lax.broadcasted_iota(jnp.int32, (tr, tr), 1)
        for m, o in enumerate(offsets):
            shift_ref[m * tr:(m + 1) * tr, :] = jnp.where(i1 == i0 + o, 1.0, 0.0).astype(BF16)

    x = pm_ref[...]
    shifted = jnp.dot(shift_ref[...], x, preferred_element_type=F32)
    acc = cw_ref[half:half + 1, :] * x.astype(F32)
    for m, o in enumerate(offsets):
        acc = acc + cw_ref[half + o:half + o + 1, :] * shifted[m * tr:(m + 1) * tr, :]

    prev = pp_ref[...].astype(F32)[8:16, :] * keep_prev
    nxt = pn_ref[...].astype(F32)[0:8, :] * keep_next
    sub = lax.broadcasted_iota(jnp.int32, prev.shape, 0)
    top = jnp.zeros_like(prev)
    bot = jnp.zeros_like(prev)
    for o in range(1, half + 1):
        top = top + cw_ref[half - o:half - o + 1, :] * jnp.where(sub < o, pltpu.roll(prev, o, 0), 0.0)
        bot = bot + cw_ref[half + o:half + o + 1, :] * jnp.where(sub >= 8 - o, pltpu.roll(nxt, 8 - o, 0), 0.0)
    acc = jnp.concatenate([acc[0:8] + top, acc[8:tr - 8], acc[tr - 8:tr] + bot], axis=0)
    y = _silu(acc)

    def head_l2(scale):
        for h in range(N_HEADS):
            yh = y[:, h * HEAD_DIM:(h + 1) * HEAD_DIM]
            inv = lax.rsqrt(jnp.sum(yh * yh, axis=-1, keepdims=True) + EPS) * scale
            qkv_ref[:, h * HEAD_DIM:(h + 1) * HEAD_DIM] = (yh * inv).astype(BF16)

    @pl.when(j == PCOL_Q)
    def _():
        head_l2(HEAD_DIM ** -0.5)

    @pl.when(j == PCOL_K)
    def _():
        head_l2(1.0)

    @pl.when(j == PCOL_V)
    def _():
        qkv_ref[...] = y.astype(BF16)

    @pl.when(j == 0)
    def _():
        s = small_ref[...]
        lane = lax.broadcasted_iota(jnp.int32, s.shape, 1)
        beta = _sigmoid(s)
        z = s + dtb_ref[...]
        softplus = jnp.maximum(z, 0.0) + jnp.log(1.0 + jnp.exp(-jnp.abs(z)))
        g = nega_ref[...] * softplus
        gb_ref[...] = jnp.where(lane < 2 * N_HEADS, beta, jnp.where(lane < 4 * N_HEADS, g, 0.0))


def _prep(p, small, conv_w8, nega, dtb, first_tiles, last_tiles):
    ta = p.shape[0]
    n_tiles = ta // PREP_TILE
    sub = PREP_TILE // 16
    n_sub = ta // 16
    kern = functools.partial(_prep_kernel, first_tiles=first_tiles, last_tiles=last_tiles)
    return pl.pallas_call(
        kern,
        grid=(n_tiles, 3),
        in_specs=[pl.BlockSpec((PREP_TILE, D_MODEL), lambda r, j: (r, j)),
                  pl.BlockSpec((16, D_MODEL), lambda r, j: (jnp.maximum(r * sub - 1, 0), j)),
                  pl.BlockSpec((16, D_MODEL), lambda r, j: (jnp.minimum((r + 1) * sub, n_sub - 1), j)),
                  pl.BlockSpec((8, D_MODEL), lambda r, j: (0, j)),
                  pl.BlockSpec((PREP_TILE, LANES), lambda r, j: (r, 0)),
                  pl.BlockSpec((1, LANES), lambda r, j: (0, 0)),
                  pl.BlockSpec((1, LANES), lambda r, j: (0, 0))],
        out_specs=[pl.BlockSpec((PREP_TILE, D_MODEL), lambda r, j: (r, j)),
                   pl.BlockSpec((PREP_TILE, LANES), lambda r, j: (r, 0))],
        out_shape=[jax.ShapeDtypeStruct((ta, 3 * D_MODEL), BF16),
                   jax.ShapeDtypeStruct((ta, LANES), F32)],
        scratch_shapes=[pltpu.VMEM(((CONV_K - 1) * PREP_TILE, PREP_TILE), BF16)],
        compiler_params=_params("arbitrary", "arbitrary"),
        name="prep",
    )(p, p, p, conv_w8, small, nega, dtb)


def _gdn_decays(d, gb):
    c = GDN_CHUNK
    row = lax.broadcasted_iota(jnp.int32, (c, c), 0)
    col = lax.broadcasted_iota(jnp.int32, (c, c), 1)
    incl = row >= col if d == 0 else row <= col
    lane = lax.broadcasted_iota(jnp.int32, gb.shape, 1)
    g_only = jnp.where(lane >= 2 * N_HEADS, jnp.where(lane < 4 * N_HEADS, gb, 0.0), 0.0)
    tri = jnp.where(incl, 1.0, 0.0).astype(BF16)
    g_hi = g_only.astype(BF16)
    g_r1 = g_only - g_hi.astype(F32)
    g_mid = g_r1.astype(BF16)
    g_lo = (g_r1 - g_mid.astype(F32)).astype(BF16)
    dot = lambda a, b: jnp.dot(a, b, preferred_element_type=F32)
    gcum = dot(tri, g_hi) + dot(tri, g_mid) + dot(tri, g_lo)
    g_end = gcum[c - 1:c, :] if d == 0 else gcum[0:1, :]
    return gcum, gcum.T, jnp.exp(gcum), jnp.exp(g_end - gcum), jnp.exp(g_end)


def _gdn_kernel(qf, kf, vf, gbf, qb, kb, vb, gbb, of_ref, ob_ref, s_ref):
    @pl.when(pl.program_id(1) == 0)
    def _():
        s_ref[...] = jnp.zeros_like(s_ref)

    c = GDN_CHUNK
    n_sub = qf.shape[0] // c
    row = lax.broadcasted_iota(jnp.int32, (c, c), 0)
    col = lax.broadcasted_iota(jnp.int32, (c, c), 1)
    eye = jnp.where(row == col, 1.0, 0.0)
    masks = ((row >= col, row > col), (row <= col, row < col))
    refs = ((qf, kf, vf, gbf, of_ref), (qb, kb, vb, gbb, ob_ref))

    def rows(d, sub):
        k = sub if d == 0 else n_sub - 1 - sub
        return slice(k * c, (k + 1) * c)

    scans = [(d, sub) for sub in range(n_sub) for d in range(2)]
    gbv = {ds: refs[ds[0]][3][rows(*ds), :] for ds in scans}
    dec = {ds: _gdn_decays(ds[0], gbv[ds]) for ds in scans}
    chains = [(d, sub, h) for d, sub in scans for h in range(N_HEADS)]

    st = []
    for d, sub, h in chains:
        q_ref, k_ref, v_ref, _, _ = refs[d]
        rs, hs = rows(d, sub), slice(h * HEAD_DIM, (h + 1) * HEAD_DIM)
        q, k, v = q_ref[rs, hs], k_ref[rs, hs], v_ref[rs, hs]
        both = _mm_nt(jnp.concatenate([q, k], axis=0), k)
        st.append(dict(q=q.astype(F32), k=k.astype(F32), v=v.astype(F32), qk=both[:c], kk=both[c:]))

    for (d, sub, h), e in zip(chains, st):
        cb = d * N_HEADS + h
        cg = 2 * N_HEADS + cb
        gcum, gcum_t, exp_g, exp_rest, exp_end = dec[d, sub]
        incl, strict = masks[d]
        e["beta"] = gbv[d, sub][:, cb:cb + 1]
        diff = gcum[:, cg:cg + 1] - gcum_t[cg:cg + 1, :]
        decay = jnp.where(incl, jnp.exp(jnp.minimum(diff, 0.0)), 0.0)
        e["y"] = jnp.where(strict, -(e["beta"] * e["kk"] * decay), 0.0)
        e["qk"] = e["qk"] * decay
        e["eg"] = exp_g[:, cg:cg + 1]
        e["er"] = exp_rest[:, cg:cg + 1]
        e["ee"] = exp_end[:, cg:cg + 1]

    levels = range(GDN_CHUNK.bit_length() - 1)
    sels = []
    for d in range(2):
        inner, outer = (col, row) if d == 0 else (row, col)
        sels.append([((outer >> lvl) & 1 == 1) & ((inner >> lvl) == (outer >> lvl) - 1) for lvl in levels])
    for (d, sub, h), e in zip(chains, st):
        e["t"] = eye + jnp.where(sels[d][0], e["y"], 0.0)
    def active_rows(d, b):
        return [(2 * p + 1 - d) * b for p in range(c // (2 * b))]

    def pick(x, starts, b):
        return jnp.concatenate([x[s:s + b] for s in starts], axis=0)

    def place(pieces, starts, b, base):
        out = []
        for blk in range(c // b):
            cur = None if base is None else base[blk * b:(blk + 1) * b]
            if blk * b in starts:
                piece = pieces[starts.index(blk * b) * b:(starts.index(blk * b) + 1) * b]
                cur = piece if cur is None else cur + piece
            out.append(jnp.zeros((b, c), F32) if cur is None else cur)
        return jnp.concatenate(out, axis=0)

    for lvl in levels[1:]:
        b = 1 << lvl
        if b % 8 == 0:
            for (d, sub, h), e in zip(chains, st):
                act = active_rows(d, b)
                yd = _mm(pick(jnp.where(sels[d][lvl], e["y"], 0.0), act, b), e["t"])
                e["yd"] = place(yd, act, b, None)
            for (d, sub, h), e in zip(chains, st):
                act = active_rows(d, b)
                e["t"] = place(_mm(pick(e["t"], act, b), e["yd"]), act, b, e["t"])
        else:
            for (d, sub, h), e in zip(chains, st):
                e["yd"] = _mm(jnp.where(sels[d][lvl], e["y"], 0.0), e["t"])
            for e in st:
                e["t"] = e["t"] + _mm(e["t"], e["yd"])

    for e in st:
        kb_ = e["k"] * e["beta"]
        uw = _mm(e["t"], jnp.concatenate([e["v"] * e["beta"], kb_ * e["eg"]], axis=-1))
        e["u"], e["w"] = uw[:, :HEAD_DIM], uw[:, HEAD_DIM:]

    state = {(d, h): s_ref[d, h] for d in range(2) for h in range(N_HEADS)}
    for step in range(n_sub):
        now = [(key, e) for key, e in zip(chains, st) if key[1] == step]
        for (d, sub, h), e in now:
            ws = _mm(jnp.concatenate([e["w"], e["q"] * e["eg"]], axis=0), state[d, h])
            e["v_new"] = e["u"] - ws[:c]
            e["o"] = ws[c:]
        for (d, sub, h), e in now:
            k_dec_t = (e["k"] * e["er"]).T
            out = _mm(jnp.concatenate([e["qk"], k_dec_t], axis=0), e["v_new"])
            refs[d][4][rows(d, sub), h * HEAD_DIM:(h + 1) * HEAD_DIM] = (e["o"] + out[:c]).astype(BF16)
            state[d, h] = state[d, h] * e["ee"] + out[c:]
    for (d, h), s in state.items():
        s_ref[d, h] = s


def _gdn(qkv, gb, batch, ctx_blocks, lat_blocks):
    ta = qkv.shape[0]
    rows = GDN_CHUNK * GDN_STEP_CHUNKS
    n_steps = ctx_blocks + lat_blocks
    lat0 = batch * ctx_blocks

    def fwd_blk(b, s):
        return jnp.where(s < ctx_blocks, b * ctx_blocks + s, lat0 + b * lat_blocks + (s - ctx_blocks))

    def bwd_blk(b, s):
        return jnp.where(s < ctx_blocks, b * ctx_blocks + (ctx_blocks - 1 - s),
                         lat0 + b * lat_blocks + (lat_blocks - 1 - (s - ctx_blocks)))

    def specs(blk):
        col = lambda j: pl.BlockSpec((rows, D_MODEL), lambda b, s: (blk(b, s), j))
        return [col(0), col(1), col(2), pl.BlockSpec((rows, LANES), lambda b, s: (blk(b, s), 0))]

    out = lambda blk: pl.BlockSpec((rows, D_MODEL), lambda b, s: (blk(b, s), 0))
    return pl.pallas_call(
        _gdn_kernel,
        grid=(batch, n_steps),
        in_specs=specs(fwd_blk) + specs(bwd_blk),
        out_specs=[out(fwd_blk), out(bwd_blk)],
        out_shape=[jax.ShapeDtypeStruct((ta, D_MODEL), BF16)] * 2,
        scratch_shapes=[pltpu.VMEM((2, N_HEADS, HEAD_DIM, HEAD_DIM), F32)],
        compiler_params=_params("arbitrary", "arbitrary"),
        name="gdn",
    )(qkv, qkv, qkv, gb, qkv, qkv, qkv, gb)


def _sgu_kernel(u_ref, v_ref, w_ref, b_ref, o_ref):
    for ch in range(u_ref.shape[0] // SGU_CHUNK):
        rs = slice(ch * SGU_CHUNK, (ch + 1) * SGU_CHUNK)
        for g in range(SGU_GROUPS):
            cs = slice(g * SGU_GROUP_DIM, (g + 1) * SGU_GROUP_DIM)
            mixed = jnp.dot(w_ref[g], v_ref[rs, cs], preferred_element_type=F32) + b_ref[:, cs]
            o_ref[rs, cs] = (u_ref[rs, cs].astype(F32) * mixed).astype(BF16)


def _sgu(p, w_s, b_full, lat_tile0, n_lat_tiles):
    return pl.pallas_call(
        _sgu_kernel,
        grid=(n_lat_tiles,),
        in_specs=[pl.BlockSpec((ROW_TILE, D_MODEL), lambda i: (i + lat_tile0, PCOL_U)),
                  pl.BlockSpec((ROW_TILE, D_MODEL), lambda i: (i + lat_tile0, PCOL_SV)),
                  pl.BlockSpec((SGU_GROUPS, SGU_CHUNK, SGU_CHUNK), lambda i: (0, 0, 0)),
                  pl.BlockSpec((SGU_CHUNK, D_MODEL), lambda i: (0, 0))],
        out_specs=pl.BlockSpec((ROW_TILE, D_MODEL), lambda i: (i, 0)),
        out_shape=jax.ShapeDtypeStruct((n_lat_tiles * ROW_TILE, D_MODEL), BF16),
        compiler_params=_params("arbitrary"),
        name="sgu",
    )(p, p, w_s, b_full)


def _merge_kernel(of_ref, ob_ref, z_ref, ysgu_ref, ga_ref, gb_ref, x_ref, wa_ref, wb_ref, wo_ref,
                  gng_ref, gate_ref, nfg_ref, shift_ref, scale_ref, wr_ref, br_ref,
                  xmid_ref, h2_ref, rgate_ref, ygdn_ref):
    o = of_ref[...].astype(F32) + ob_ref[...].astype(F32)
    for h in range(N_HEADS):
        hs = slice(h * HEAD_DIM, (h + 1) * HEAD_DIM)
        oh = o[:, hs]
        inv = lax.rsqrt(jnp.mean(oh * oh, axis=-1, keepdims=True) + EPS)
        ygdn_ref[:, hs] = (oh * inv * gng_ref[...] * z_ref[:, hs].astype(F32)).astype(BF16)
    ya = jnp.dot(ygdn_ref[...], wa_ref[...], preferred_element_type=F32)
    yb = jnp.dot(ysgu_ref[...], wb_ref[...], preferred_element_type=F32)
    merged = ga_ref[...].astype(F32) * ya + gb_ref[...].astype(F32) * yb
    mix = jnp.dot(merged.astype(BF16), wo_ref[...], preferred_element_type=F32)
    xm = x_ref[...] + gate_ref[0] * mix
    xmid_ref[...] = xm
    hn = xm * lax.rsqrt(jnp.mean(xm * xm, axis=-1, keepdims=True) + EPS) * nfg_ref[...]
    h2 = hn * (1.0 + scale_ref[0]) + shift_ref[0]
    h2_ref[...] = h2.astype(BF16)

    h_hi = h2.astype(BF16)
    h_lo = (h2 - h_hi.astype(F32)).astype(BF16)
    wr = wr_ref[...]
    w_hi = wr.astype(BF16)
    w_lo = (wr - w_hi.astype(F32)).astype(BF16)
    dot = lambda a, b: jnp.dot(a, b, preferred_element_type=F32)
    lg = dot(h_hi, w_hi) + dot(h_lo, w_hi) + dot(h_hi, w_lo) + br_ref[...]

    lane = lax.broadcasted_iota(jnp.int32, lg.shape, 1).astype(F32)
    far = float(LANES)
    gl = jnp.where(lane < N_EXPERT_GROUPS, lg, NEG_BIG)
    gmax = jnp.max(gl, axis=-1, keepdims=True)
    p_g = 1.0 / jnp.sum(jnp.exp(gl - gmax), axis=-1, keepdims=True)
    grp = jnp.min(jnp.where(gl == gmax, lane, far), axis=-1, keepdims=True)
    lo = N_EXPERT_GROUPS + EXPERTS_PER_GROUP * grp
    in_grp = jnp.where(lane >= lo, jnp.where(lane < lo + EXPERTS_PER_GROUP, 1.0, 0.0), 0.0)
    el = jnp.where(in_grp > 0.0, lg, NEG_BIG)
    m1 = jnp.max(el, axis=-1, keepdims=True)
    i1 = jnp.min(jnp.where(el == m1, lane, far), axis=-1, keepdims=True)
    el2 = jnp.where(lane == i1, NEG_BIG, el)
    m2 = jnp.max(el2, axis=-1, keepdims=True)
    i2 = jnp.min(jnp.where(el2 == m2, lane, far), axis=-1, keepdims=True)
    t = jnp.exp(m2 - m1)
    w1 = p_g / (1.0 + t)
    w2 = w1 * t
    rgate_ref[...] = jnp.where(lane == 0.0, grp, jnp.where(lane == i1, w1, 0.0) + jnp.where(lane == i2, w2, 0.0))


def _merge(o_f, o_b, p, y_sgu, x2, w_a, w_b, w_o, gng, gate3, nfg, shift3, scale3, w_r, b_r,
           lat_tile0, tiles_per_batch):
    t_lat = x2.shape[0]
    n_tiles = t_lat // ROW_TILE
    lat = lambda c: pl.BlockSpec((ROW_TILE, D_MODEL), lambda i: (i + lat_tile0, c))
    own = lambda: pl.BlockSpec((ROW_TILE, D_MODEL), lambda i: (i, 0))
    mat = lambda: pl.BlockSpec((D_MODEL, D_MODEL), lambda i: (0, 0))
    vec = lambda: pl.BlockSpec((1, D_MODEL), lambda i: (0, 0))
    per_b = lambda: pl.BlockSpec((1, 1, D_MODEL), lambda i: (i // tiles_per_batch, 0, 0))
    return pl.pallas_call(
        _merge_kernel,
        grid=(n_tiles,),
        in_specs=[lat(0), lat(0), lat(PCOL_Z), own(), lat(PCOL_GA), lat(PCOL_GB), own(),
                  mat(), mat(), mat(), pl.BlockSpec((1, HEAD_DIM), lambda i: (0, 0)),
                  per_b(), vec(), per_b(), per_b(),
                  pl.BlockSpec((D_MODEL, LANES), lambda i: (0, 0)),
                  pl.BlockSpec((1, LANES), lambda i: (0, 0))],
        out_specs=[own(), own(), pl.BlockSpec((ROW_TILE, LANES), lambda i: (i, 0))],
        out_shape=[jax.ShapeDtypeStruct((t_lat, D_MODEL), F32),
                   jax.ShapeDtypeStruct((t_lat, D_MODEL), BF16),
                   jax.ShapeDtypeStruct((t_lat, LANES), F32)],
        scratch_shapes=[pltpu.VMEM((ROW_TILE, D_MODEL), BF16)],
        compiler_params=_params("arbitrary"),
        name="merge",
    )(o_f, o_b, p, y_sgu, p, p, x2, w_a, w_b, w_o, gng, gate3, nfg, shift3, scale3, w_r, b_r)


def _split3(x):
    hi = x.astype(BF16)
    r1 = x - hi.astype(F32)
    mid = r1.astype(BF16)
    return hi, mid, (r1 - mid.astype(F32)).astype(BF16)


def _moe_kernel(h_ref, rg_ref, w1_ref, w3_ref, w2_ref, xmid_ref, gate_ref, fng_ref, o_ref,
                before_ref, destc_ref, xs_ref, ys_ref, gs_ref, start_ref):
    i, g, hf = pl.program_id(0), pl.program_id(1), pl.program_id(2)
    tm = h_ref.shape[0]
    dot = lambda a, b: jnp.dot(a, b, preferred_element_type=F32)

    @pl.when((i == 0) & (g == 0) & (hf == 0))
    def _():
        r = lax.broadcasted_iota(jnp.int32, (tm, tm), 0)
        c = lax.broadcasted_iota(jnp.int32, (tm, tm), 1)
        before_ref[...] = jnp.where(r < c, 1.0, 0.0).astype(BF16)

    @pl.when((g == 0) & (hf == 0))
    def _():
        rg = rg_ref[...]
        sub = lax.broadcasted_iota(jnp.int32, (8, tm), 0).astype(F32)
        m_rows = jnp.where(rg.T[0:1, :] == sub, 1.0, 0.0)
        total_r = jnp.sum(m_rows, axis=1, keepdims=True)
        start_r = jnp.zeros_like(total_r)
        for gg in range(N_EXPERT_GROUPS - 1):
            start_r = start_r + jnp.where(sub[:, 0:1] > gg, total_r[gg:gg + 1, :], 0.0)
        rank_r = dot(m_rows.astype(BF16), before_ref[...])
        dest_r = jnp.sum(jnp.where(m_rows > 0.0, rank_r + start_r, 0.0), axis=0, keepdims=True)
        destc_ref[...] = jnp.broadcast_to(dest_r, (LANES, tm)).T
        acc = jnp.int32(0)
        for gg in range(N_EXPERT_GROUPS):
            start_ref[gg] = acc
            acc = acc + total_r[gg, 0].astype(jnp.int32)
        start_ref[N_EXPERT_GROUPS] = acc
        slot = lax.broadcasted_iota(jnp.int32, (tm, tm), 0).astype(F32)
        perm = jnp.where(dest_r == slot, 1.0, 0.0).astype(BF16)
        xs_ref[...] = dot(perm, h_ref[...]).astype(BF16)
        g_hi, g_mid, _ = _split3(rg)
        gs_ref[...] = dot(perm, g_hi) + dot(perm, g_mid)
        ys_ref[...] = jnp.zeros_like(ys_ref)

    bs = MOE_BLOCK
    first = start_ref[g] // bs
    n_unit = (start_ref[g + 1] + (bs - 1)) // bs - first

    def experts(row0, size):
        rows = pl.ds(pl.multiple_of(row0, bs), size)
        x = xs_ref[rows, :]
        gs = gs_ref[rows, :]
        lane = lax.broadcasted_iota(jnp.int32, gs.shape, 1)
        y = ys_ref[rows, :]
        for e in range(MOE_EXPERTS_PER_STEP):
            col = N_EXPERT_GROUPS + g * EXPERTS_PER_GROUP + hf * MOE_EXPERTS_PER_STEP + e
            gate = jnp.sum(jnp.where(lane == col, gs, 0.0), axis=-1, keepdims=True)
            hid = _silu(dot(x, w1_ref[e])) * dot(x, w3_ref[e]) * gate
            y = y + dot(hid.astype(BF16), w2_ref[e])
        ys_ref[rows, :] = y

    big = MOE_BLOCK_MULTIPLES[0]

    def big_block(k, carry):
        experts((first + k * big) * bs, big * bs)
        return carry

    lax.fori_loop(0, n_unit // big, big_block, 0)
    done = first + (n_unit // big) * big
    for m in MOE_BLOCK_MULTIPLES[1:]:
        take = (n_unit & m) != 0

        @pl.when(take)
        def _(done=done, m=m):
            experts(done * bs, m * bs)

        done = done + jnp.where(take, m, 0)

    @pl.when((g == pl.num_programs(1) - 1) & (hf == pl.num_programs(2) - 1))
    def _():
        slot = lax.broadcasted_iota(jnp.int32, (tm, tm), 1).astype(F32)
        perm_t = jnp.where(destc_ref[:, 0:1] == slot, 1.0, 0.0).astype(BF16)
        xo = xmid_ref[...] + gate_ref[0] * dot(perm_t, ys_ref[...].astype(BF16))
        o_ref[...] = xo * lax.rsqrt(jnp.mean(xo * xo, axis=-1, keepdims=True) + EPS) * fng_ref[...]


def _moe(h2, rgate, w1, w3, w2, xmid, gate3, fng, tiles_per_batch):
    t_lat = h2.shape[0]
    n_tiles = t_lat // MOE_TILE
    eps = MOE_EXPERTS_PER_STEP
    halves = EXPERTS_PER_GROUP // eps
    row = lambda w: pl.BlockSpec((MOE_TILE, w), lambda i, g, hf: (i, 0))
    wspec = lambda a, b: pl.BlockSpec((eps, a, b), lambda i, g, hf: (g * halves + hf, 0, 0))
    return pl.pallas_call(
        _moe_kernel,
        grid=(n_tiles, N_EXPERT_GROUPS, halves),
        in_specs=[row(D_MODEL), row(LANES),
                  wspec(D_MODEL, D_EXPERT), wspec(D_MODEL, D_EXPERT), wspec(D_EXPERT, D_MODEL),
                  row(D_MODEL),
                  pl.BlockSpec((1, 1, D_MODEL), lambda i, g, hf: (i // tiles_per_batch, 0, 0)),
                  pl.BlockSpec((1, D_MODEL), lambda i, g, hf: (0, 0))],
        out_specs=row(D_MODEL),
        out_shape=jax.ShapeDtypeStruct((t_lat, D_MODEL), F32),
        scratch_shapes=[pltpu.VMEM((MOE_TILE, MOE_TILE), BF16),
                        pltpu.VMEM((MOE_TILE, LANES), F32),
                        pltpu.VMEM((MOE_TILE, D_MODEL), BF16),
                        pltpu.VMEM((MOE_TILE, D_MODEL), F32),
                        pltpu.VMEM((MOE_TILE, LANES), F32),
                        pltpu.SMEM((N_EXPERT_GROUPS + 1,), jnp.int32)],
        compiler_params=pltpu.CompilerParams(dimension_semantics=("arbitrary",) * 3,
                                             vmem_limit_bytes=MOE_VMEM_LIMIT),
        name="moe",
    )(h2, rgate, w1, w3, w2, xmid, gate3, fng)


def kernel(x, c, ctx, c_ctx, ada_w, ada_b, norm_mix_g, w_in, conv_w, a_log, dt_bias, gdn_norm_g, sgu_ln_g, sgu_ln_b, sgu_w, sgu_b, w_branch_a, w_branch_b, w_out, norm_ffn_g, router_group_w, router_group_b, router_expert_w, router_expert_b, expert_w1, expert_w3, expert_w2, final_norm_g):
    batch, seq, d = x.shape
    ctx_len = ctx.shape[1]
    assert d == D_MODEL and ada_w.shape[0] == 1, "single-layer block with D_MODEL channels"
    assert batch * ctx_len == ROW_TILE, "context rows of all samples form one row tile"
    assert seq % MOE_TILE == 0 and ctx_len % PREP_TILE == 0 and batch + 1 <= 8
    t_lat = batch * seq
    row = lambda v: v.reshape(1, -1).astype(F32)

    cond = jnp.zeros((8, d), F32).at[:batch].set(c).at[batch].set(c_ctx)
    mod = _adaln(cond, ada_w[0], ada_b[0]).reshape(8, N_MOD, d)
    mod_row = lambda k: mod[:batch + 1, k].reshape(batch + 1, 1, d)

    w_l = w_in[0]
    w_main = jnp.concatenate([w_l[:, :COL_BETA], w_l[:, COL_Z:]], axis=1).astype(BF16)
    w_small = jnp.zeros((d, LANES), BF16).at[:, :4 * N_HEADS].set(w_l[:, COL_BETA:COL_Z].astype(BF16))
    x2 = x.reshape(t_lat, d)
    tiles_per_batch = seq // ROW_TILE
    p, small = _inproj(x2, ctx.reshape(batch * ctx_len, d), row(norm_mix_g), mod_row(0), mod_row(1),
                       w_main, w_small, row(sgu_ln_g), row(sgu_ln_b), tiles_per_batch)

    ctx_t, lat_t = ctx_len // PREP_TILE, seq // PREP_TILE
    starts = [b * ctx_t for b in range(batch)] + [batch * ctx_t + b * lat_t for b in range(batch)]
    ends = [(b + 1) * ctx_t - 1 for b in range(batch)] + [batch * ctx_t + (b + 1) * lat_t - 1 for b in range(batch)]
    conv_w8 = jnp.zeros((8, 3 * D_GDN), F32).at[:CONV_K].set(conv_w[0])
    pad_lanes = lambda v: jnp.zeros((1, LANES), F32).at[0, 2 * N_HEADS:4 * N_HEADS].set(v.reshape(-1))
    qkv, gb = _prep(p, small, conv_w8, pad_lanes(-jnp.exp(a_log[0])), pad_lanes(dt_bias[0]),
                    tuple(starts), tuple(ends))

    step_rows = GDN_CHUNK * GDN_STEP_CHUNKS
    assert ctx_len % step_rows == 0 and seq % step_rows == 0
    o_f, o_b = _gdn(qkv, gb, batch, ctx_len // step_rows, seq // step_rows)

    b_full = jnp.repeat(sgu_b[0].T, SGU_GROUP_DIM, axis=1).astype(F32)
    y_sgu = _sgu(p, sgu_w[0].astype(BF16), b_full, 1, t_lat // ROW_TILE)

    w_r = jnp.zeros((d, LANES), F32).at[:, :N_EXPERT_GROUPS].set(router_group_w[0]) \
        .at[:, N_EXPERT_GROUPS:N_EXPERT_GROUPS + N_EXPERTS].set(router_expert_w[0])
    b_r = jnp.zeros((1, LANES), F32).at[0, :N_EXPERT_GROUPS].set(router_group_b[0]) \
        .at[0, N_EXPERT_GROUPS:N_EXPERT_GROUPS + N_EXPERTS].set(router_expert_b[0])
    gng = gdn_norm_g[0].reshape(1, HEAD_DIM).astype(F32)
    xmid, h2, rgate = _merge(o_f, o_b, p, y_sgu, x2, w_branch_a[0].astype(BF16), w_branch_b[0].astype(BF16),
                             w_out[0].astype(BF16), gng, mod_row(2)[:batch], row(norm_ffn_g),
                             mod_row(3)[:batch], mod_row(4)[:batch], w_r, b_r, 1, tiles_per_batch)

    out = _moe(h2, rgate, expert_w1[0].astype(BF16), expert_w3[0].astype(BF16), expert_w2[0].astype(BF16),
               xmid, mod_row(5)[:batch], row(final_norm_g), seq // MOE_TILE)
    return out.reshape(batch, seq, d)
```

```python
import functools
import math

import jax
import jax.numpy as jnp
from jax import lax
from jax.experimental import pallas as pl
from jax.experimental.pallas import tpu as pltpu

F32 = jnp.float32
BF16 = jnp.bfloat16

D_MODEL = 1024
N_HEADS = 8
HEAD_DIM = 128
D_GDN = N_HEADS * HEAD_DIM
CONV_K = 5
GDN_CHUNK = 64
GDN_STEP_CHUNKS = 2
SGU_GROUPS = 8
SGU_GROUP_DIM = 128
D_SGU = SGU_GROUPS * SGU_GROUP_DIM
SGU_CHUNK = 128
N_EXPERT_GROUPS = 4
EXPERTS_PER_GROUP = 8
N_EXPERTS = N_EXPERT_GROUPS * EXPERTS_PER_GROUP
D_EXPERT = 256
N_MOD = 6
EPS = 1e-6
COL_BETA = 3 * D_GDN
COL_Z = COL_BETA + 4 * N_HEADS

LANES = 128
ROW_TILE = 512
INPROJ_SEGS = 4
MERGE_PARTS = 2
PREP_TILE = 256
MOE_TILE = 1024
MOE_EXPERTS_PER_STEP = 4
MOE_BLOCK = 128
MOE_BLOCK_MULTIPLES = (4, 2, 1)
VMEM_LIMIT = 48 * 1024 * 1024
MOE_VMEM_LIMIT = 56 * 1024 * 1024
NEG_BIG = -1e30

PCOL_Q, PCOL_K, PCOL_V, PCOL_Z, PCOL_U, PCOL_SV, PCOL_GA, PCOL_GB = range(8)


def _mm(a, b):
    return jnp.dot(a.astype(BF16), b.astype(BF16), preferred_element_type=F32)


def _mm_nt(a, b):
    return lax.dot_general(a.astype(BF16), b.astype(BF16), (((1,), (1,)), ((), ())),
                           preferred_element_type=F32)


def _sigmoid(x):
    return 1.0 / (1.0 + jnp.exp(-x))


def _silu(x):
    return x * _sigmoid(x)


def _gelu_tanh(x):
    return 0.5 * x * (1.0 + jnp.tanh(math.sqrt(2.0 / math.pi) * (x + 0.044715 * (x * x * x))))


def _params(*sem):
    return pltpu.CompilerParams(dimension_semantics=sem, vmem_limit_bytes=VMEM_LIMIT)


def _adaln_kernel(c_ref, w_ref, b_ref, o_ref):
    o_ref[...] = _mm(_silu(c_ref[...]), w_ref[...]) + b_ref[...]


def _adaln(cond, w, b):
    n = w.shape[1]
    tn = 1536
    return pl.pallas_call(
        _adaln_kernel,
        grid=(n // tn,),
        in_specs=[pl.BlockSpec((8, D_MODEL), lambda j: (0, 0)),
                  pl.BlockSpec((D_MODEL, tn), lambda j: (0, j)),
                  pl.BlockSpec((1, tn), lambda j: (0, j))],
        out_specs=pl.BlockSpec((8, tn), lambda j: (0, j)),
        out_shape=jax.ShapeDtypeStruct((8, n), F32),
        compiler_params=_params("arbitrary"),
        name="adaln",
    )(cond, w, b.reshape(1, n))


def _inproj_kernel(x_ref, ctx_ref, g_ref, shift_ref, scale_ref, wm_ref, ws_ref, lng_ref, lnb_ref,
                   p_ref, small_ref, h_ref):
    i = pl.program_id(0)
    j = pl.program_id(1)

    def norm_mod(xv):
        y = xv * lax.rsqrt(jnp.mean(xv * xv, axis=-1, keepdims=True) + EPS) * g_ref[...]
        return y * (1.0 + scale_ref[0]) + shift_ref[0]

    @pl.when(j == 0)
    def _():
        @pl.when(i == 0)
        def _():
            h_ref[...] = norm_mod(ctx_ref[...]).astype(BF16)

        @pl.when(i > 0)
        def _():
            h_ref[...] = norm_mod(x_ref[...]).astype(BF16)

        small_ref[...] = jnp.dot(h_ref[...], ws_ref[...], preferred_element_type=F32)

    def raw(a):
        return a

    def gelu_ln(a):
        a = _gelu_tanh(a)
        mu = jnp.mean(a, axis=-1, keepdims=True)
        ac = a - mu
        var = jnp.mean(ac * ac, axis=-1, keepdims=True)
        return ac * lax.rsqrt(var + EPS) * lng_ref[...] + lnb_ref[...]

    epilogues = (raw, raw, raw, _silu, _gelu_tanh, gelu_ln, _sigmoid, _sigmoid)
    for step in range(len(epilogues) // INPROJ_SEGS):
        @pl.when(j == step)
        def _(step=step):
            for seg in range(INPROJ_SEGS):
                cs = slice(seg * D_MODEL, (seg + 1) * D_MODEL)
                a = jnp.dot(h_ref[...], wm_ref[:, cs], preferred_element_type=F32)
                p_ref[:, cs] = epilogues[INPROJ_SEGS * step + seg](a).astype(BF16)


def _inproj(x2, ctx2, norm_g, shift3, scale3, w_main, w_small, ln_g, ln_b, tiles_per_batch):
    t_lat = x2.shape[0]
    n_lat = t_lat // ROW_TILE
    n_tiles = n_lat + 1
    tn = INPROJ_SEGS * D_MODEL
    n_col = w_main.shape[1] // tn
    n_batch = shift3.shape[0] - 1
    sel = lambda i: jnp.where(i == 0, n_batch, (jnp.maximum(i, 1) - 1) // tiles_per_batch)
    vec = lambda: pl.BlockSpec((1, D_MODEL), lambda i, j: (0, 0))
    return pl.pallas_call(
        _inproj_kernel,
        grid=(n_tiles, n_col),
        in_specs=[pl.BlockSpec((ROW_TILE, D_MODEL), lambda i, j: (jnp.maximum(i, 1) - 1, 0)),
                  pl.BlockSpec((ROW_TILE, D_MODEL), lambda i, j: (0, 0)),
                  vec(),
                  pl.BlockSpec((1, 1, D_MODEL), lambda i, j: (sel(i), 0, 0)),
                  pl.BlockSpec((1, 1, D_MODEL), lambda i, j: (sel(i), 0, 0)),
                  pl.BlockSpec((D_MODEL, tn), lambda i, j: (0, j)),
                  pl.BlockSpec((D_MODEL, LANES), lambda i, j: (0, 0)),
                  vec(), vec()],
        out_specs=[pl.BlockSpec((ROW_TILE, tn), lambda i, j: (i, j)),
                   pl.BlockSpec((ROW_TILE, LANES), lambda i, j: (i, 0))],
        out_shape=[jax.ShapeDtypeStruct((n_tiles * ROW_TILE, n_col * tn), BF16),
                   jax.ShapeDtypeStruct((n_tiles * ROW_TILE, LANES), F32)],
        scratch_shapes=[pltpu.VMEM((ROW_TILE, D_MODEL), BF16)],
        compiler_params=_params("arbitrary", "arbitrary"),
        name="inproj",
    )(x2, ctx2, norm_g, shift3, scale3, w_main, w_small, ln_g, ln_b)


def _prep_kernel(pm_ref, pp_ref, pn_ref, cw_ref, small_ref, nega_ref, dtb_ref, qkv_ref, gb_ref, shift_ref,
                 *, first_tiles, last_tiles):
    r = pl.program_id(0)
    tr = pm_ref.shape[0]
    is_first = functools.reduce(jnp.logical_or, [r == t for t in first_tiles])
    is_last = functools.reduce(jnp.logical_or, [r == t for t in last_tiles])
    keep_prev = jnp.where(is_first, 0.0, 1.0)
    keep_next = jnp.where(is_last, 0.0, 1.0)

    half = CONV_K // 2
    offsets = [o for o in range(-half, half + 1) if o != 0]

    @pl.when(r == 0)
    def _():
        i0 = lax.broadcasted_iota(jnp.int32, (tr, tr), 0)
        i1 = lax.broadcasted_iota(jnp.int32, (tr, tr), 1)
        for m, o in enumerate(offsets):
            shift_ref[m * tr:(m + 1) * tr, :] = jnp.where(i1 == i0 + o, 1.0, 0.0).astype(BF16)

    def conv_silu(cs):
        x = pm_ref[:, cs]
        tap = lambda o: cw_ref[half + o:half + o + 1, cs]
        shifted = jnp.dot(shift_ref[...], x, preferred_element_type=F32)
        acc = tap(0) * x.astype(F32)
        for m, o in enumerate(offsets):
            acc = acc + tap(o) * shifted[m * tr:(m + 1) * tr, :]
        prev = pp_ref[:, cs].astype(F32)[8:16, :] * keep_prev
        nxt = pn_ref[:, cs].astype(F32)[0:8, :] * keep_next
        sub = lax.broadcasted_iota(jnp.int32, prev.shape, 0)
        top = jnp.zeros_like(prev)
        bot = jnp.zeros_like(prev)
        for o in range(1, half + 1):
            top = top + tap(-o) * jnp.where(sub < o, pltpu.roll(prev, o, 0), 0.0)
            bot = bot + tap(o) * jnp.where(sub >= 8 - o, pltpu.roll(nxt, 8 - o, 0), 0.0)
        return _silu(jnp.concatenate([acc[0:8] + top, acc[8:tr - 8], acc[tr - 8:tr] + bot], axis=0))

    for j, scale in ((PCOL_Q, HEAD_DIM ** -0.5), (PCOL_K, 1.0)):
        y = conv_silu(slice(j * D_GDN, (j + 1) * D_GDN))
        for h in range(N_HEADS):
            yh = y[:, h * HEAD_DIM:(h + 1) * HEAD_DIM]
            inv = lax.rsqrt(jnp.sum(yh * yh, axis=-1, keepdims=True) + EPS) * scale
            qkv_ref[:, j * D_GDN + h * HEAD_DIM:j * D_GDN + (h + 1) * HEAD_DIM] = (yh * inv).astype(BF16)
    vs = slice(PCOL_V * D_GDN, (PCOL_V + 1) * D_GDN)
    qkv_ref[:, vs] = conv_silu(vs).astype(BF16)

    s = small_ref[...]
    lane = lax.broadcasted_iota(jnp.int32, s.shape, 1)
    beta = _sigmoid(s)
    z = s + dtb_ref[...]
    softplus = jnp.maximum(z, 0.0) + jnp.log(1.0 + jnp.exp(-jnp.abs(z)))
    g = nega_ref[...] * softplus
    gb_ref[...] = jnp.where(lane < 2 * N_HEADS, beta, jnp.where(lane < 4 * N_HEADS, g, 0.0))


def _prep(p, small, conv_w8, nega, dtb, first_tiles, last_tiles):
    ta = p.shape[0]
    n_tiles = ta // PREP_TILE
    sub = PREP_TILE // 16
    n_sub = ta // 16
    width = 3 * D_GDN
    kern = functools.partial(_prep_kernel, first_tiles=first_tiles, last_tiles=last_tiles)
    return pl.pallas_call(
        kern,
        grid=(n_tiles,),
        in_specs=[pl.BlockSpec((PREP_TILE, width), lambda r: (r, 0)),
                  pl.BlockSpec((16, width), lambda r: (jnp.maximum(r * sub - 1, 0), 0)),
                  pl.BlockSpec((16, width), lambda r: (jnp.minimum((r + 1) * sub, n_sub - 1), 0)),
                  pl.BlockSpec((8, width), lambda r: (0, 0)),
                  pl.BlockSpec((PREP_TILE, LANES), lambda r: (r, 0)),
                  pl.BlockSpec((1, LANES), lambda r: (0, 0)),
                  pl.BlockSpec((1, LANES), lambda r: (0, 0))],
        out_specs=[pl.BlockSpec((PREP_TILE, width), lambda r: (r, 0)),
                   pl.BlockSpec((PREP_TILE, LANES), lambda r: (r, 0))],
        out_shape=[jax.ShapeDtypeStruct((ta, width), BF16),
                   jax.ShapeDtypeStruct((ta, LANES), F32)],
        scratch_shapes=[pltpu.VMEM(((CONV_K - 1) * PREP_TILE, PREP_TILE), BF16)],
        compiler_params=_params("arbitrary"),
        name="prep",
    )(p, p, p, conv_w8, small, nega, dtb)


def _gdn_decays(d, gb):
    c = GDN_CHUNK
    row = lax.broadcasted_iota(jnp.int32, (c, c), 0)
    col = lax.broadcasted_iota(jnp.int32, (c, c), 1)
    incl = row >= col if d == 0 else row <= col
    lane = lax.broadcasted_iota(jnp.int32, gb.shape, 1)
    g_only = jnp.where(lane >= 2 * N_HEADS, jnp.where(lane < 4 * N_HEADS, gb, 0.0), 0.0)
    tri = jnp.where(incl, 1.0, 0.0).astype(BF16)
    g_hi = g_only.astype(BF16)
    g_r1 = g_only - g_hi.astype(F32)
    g_mid = g_r1.astype(BF16)
    g_lo = (g_r1 - g_mid.astype(F32)).astype(BF16)
    dot = lambda a, b: jnp.dot(a, b, preferred_element_type=F32)
    gcum = dot(tri, g_hi) + dot(tri, g_mid) + dot(tri, g_lo)
    g_end = gcum[c - 1:c, :] if d == 0 else gcum[0:1, :]
    return gcum, gcum.T, jnp.exp(gcum), jnp.exp(g_end - gcum), jnp.exp(g_end)


def _gdn_kernel(qf, kf, vf, gbf, qb, kb, vb, gbb, of_ref, ob_ref, s_ref):
    @pl.when(pl.program_id(1) == 0)
    def _():
        s_ref[...] = jnp.zeros_like(s_ref)

    c = GDN_CHUNK
    n_sub = qf.shape[0] // c
    row = lax.broadcasted_iota(jnp.int32, (c, c), 0)
    col = lax.broadcasted_iota(jnp.int32, (c, c), 1)
    eye = jnp.where(row == col, 1.0, 0.0)
    masks = ((row >= col, row > col), (row <= col, row < col))
    refs = ((qf, kf, vf, gbf, of_ref), (qb, kb, vb, gbb, ob_ref))

    def rows(d, sub):
        k = sub if d == 0 else n_sub - 1 - sub
        return slice(k * c, (k + 1) * c)

    scans = [(d, sub) for sub in range(n_sub) for d in range(2)]
    gbv = {ds: refs[ds[0]][3][rows(*ds), :] for ds in scans}
    dec = {ds: _gdn_decays(ds[0], gbv[ds]) for ds in scans}
    chains = [(d, sub, h) for d, sub in scans for h in range(N_HEADS)]

    st = []
    for d, sub, h in chains:
        q_ref, k_ref, v_ref, _, _ = refs[d]
        rs, hs = rows(d, sub), slice(h * HEAD_DIM, (h + 1) * HEAD_DIM)
        q, k, v = q_ref[rs, hs], k_ref[rs, hs], v_ref[rs, hs]
        both = _mm_nt(jnp.concatenate([q, k], axis=0), k)
        st.append(dict(q=q.astype(F32), k=k.astype(F32), v=v.astype(F32), qk=both[:c], kk=both[c:]))

    for (d, sub, h), e in zip(chains, st):
        cb = d * N_HEADS + h
        cg = 2 * N_HEADS + cb
        gcum, gcum_t, exp_g, exp_rest, exp_end = dec[d, sub]
        incl, strict = masks[d]
        e["beta"] = gbv[d, sub][:, cb:cb + 1]
        diff = gcum[:, cg:cg + 1] - gcum_t[cg:cg + 1, :]
        decay = jnp.where(incl, jnp.exp(jnp.minimum(diff, 0.0)), 0.0)
        e["y"] = jnp.where(strict, -(e["beta"] * e["kk"] * decay), 0.0)
        e["qk"] = e["qk"] * decay
        e["eg"] = exp_g[:, cg:cg + 1]
        e["er"] = exp_rest[:, cg:cg + 1]
        e["ee"] = exp_end[:, cg:cg + 1]

    levels = range(GDN_CHUNK.bit_length() - 1)
    sels = []
    for d in range(2):
        inner, outer = (col, row) if d == 0 else (row, col)
        sels.append([((outer >> lvl) & 1 == 1) & ((inner >> lvl) == (outer >> lvl) - 1) for lvl in levels])
    for (d, sub, h), e in zip(chains, st):
        e["t"] = eye + jnp.where(sels[d][0], e["y"], 0.0)
    def active_rows(d, b):
        return [(2 * p + 1 - d) * b for p in range(c // (2 * b))]

    def pick(x, starts, b):
        return jnp.concatenate([x[s:s + b] for s in starts], axis=0)

    def place(pieces, starts, b, base):
        out = []
        for blk in range(c // b):
            cur = None if base is None else base[blk * b:(blk + 1) * b]
            if blk * b in starts:
                piece = pieces[starts.index(blk * b) * b:(starts.index(blk * b) + 1) * b]
                cur = piece if cur is None else cur + piece
            out.append(jnp.zeros((b, c), F32) if cur is None else cur)
        return jnp.concatenate(out, axis=0)

    for lvl in levels[1:]:
        b = 1 << lvl
        if b % 8 == 0:
            for (d, sub, h), e in zip(chains, st):
                act = active_rows(d, b)
                yd = _mm(pick(jnp.where(sels[d][lvl], e["y"], 0.0), act, b), e["t"])
                e["yd"] = place(yd, act, b, None)
            for (d, sub, h), e in zip(chains, st):
                act = active_rows(d, b)
                e["t"] = place(_mm(pick(e["t"], act, b), e["yd"]), act, b, e["t"])
        else:
            for (d, sub, h), e in zip(chains, st):
                e["yd"] = _mm(jnp.where(sels[d][lvl], e["y"], 0.0), e["t"])
            for e in st:
                e["t"] = e["t"] + _mm(e["t"], e["yd"])

    for e in st:
        kb_ = e["k"] * e["beta"]
        uw = _mm(e["t"], jnp.concatenate([e["v"] * e["beta"], kb_ * e["eg"]], axis=-1))
        e["u"], e["w"] = uw[:, :HEAD_DIM], uw[:, HEAD_DIM:]

    state = {(d, h): s_ref[d, h] for d in range(2) for h in range(N_HEADS)}
    for step in range(n_sub):
        now = [(key, e) for key, e in zip(chains, st) if key[1] == step]
        for (d, sub, h), e in now:
            ws = _mm(jnp.concatenate([e["w"], e["q"] * e["eg"]], axis=0), state[d, h])
            e["v_new"] = e["u"] - ws[:c]
            e["o"] = ws[c:]
        for (d, sub, h), e in now:
            k_dec_t = (e["k"] * e["er"]).T
            out = _mm(jnp.concatenate([e["qk"], k_dec_t], axis=0), e["v_new"])
            refs[d][4][rows(d, sub), h * HEAD_DIM:(h + 1) * HEAD_DIM] = (e["o"] + out[:c]).astype(BF16)
            state[d, h] = state[d, h] * e["ee"] + out[c:]
    for (d, h), s in state.items():
        s_ref[d, h] = s


def _gdn(qkv, gb, batch, ctx_blocks, lat_blocks):
    ta = qkv.shape[0]
    rows = GDN_CHUNK * GDN_STEP_CHUNKS
    n_steps = ctx_blocks + lat_blocks
    lat0 = batch * ctx_blocks

    def fwd_blk(b, s):
        return jnp.where(s < ctx_blocks, b * ctx_blocks + s, lat0 + b * lat_blocks + (s - ctx_blocks))

    def bwd_blk(b, s):
        return jnp.where(s < ctx_blocks, b * ctx_blocks + (ctx_blocks - 1 - s),
                         lat0 + b * lat_blocks + (lat_blocks - 1 - (s - ctx_blocks)))

    def specs(blk):
        col = lambda j: pl.BlockSpec((rows, D_MODEL), lambda b, s: (blk(b, s), j))
        return [col(0), col(1), col(2), pl.BlockSpec((rows, LANES), lambda b, s: (blk(b, s), 0))]

    out = lambda blk: pl.BlockSpec((rows, D_MODEL), lambda b, s: (blk(b, s), 0))
    return pl.pallas_call(
        _gdn_kernel,
        grid=(batch, n_steps),
        in_specs=specs(fwd_blk) + specs(bwd_blk),
        out_specs=[out(fwd_blk), out(bwd_blk)],
        out_shape=[jax.ShapeDtypeStruct((ta, D_MODEL), BF16)] * 2,
        scratch_shapes=[pltpu.VMEM((2, N_HEADS, HEAD_DIM, HEAD_DIM), F32)],
        compiler_params=_params("arbitrary", "arbitrary"),
        name="gdn",
    )(qkv, qkv, qkv, gb, qkv, qkv, qkv, gb)


def _sgu_kernel(u_ref, v_ref, w_ref, b_ref, o_ref):
    for ch in range(u_ref.shape[0] // SGU_CHUNK):
        rs = slice(ch * SGU_CHUNK, (ch + 1) * SGU_CHUNK)
        for g in range(SGU_GROUPS):
            cs = slice(g * SGU_GROUP_DIM, (g + 1) * SGU_GROUP_DIM)
            mixed = jnp.dot(w_ref[g], v_ref[rs, cs], preferred_element_type=F32) + b_ref[:, cs]
            o_ref[rs, cs] = (u_ref[rs, cs].astype(F32) * mixed).astype(BF16)


def _sgu(p, w_s, b_full, lat_tile0, n_lat_tiles):
    return pl.pallas_call(
        _sgu_kernel,
        grid=(n_lat_tiles,),
        in_specs=[pl.BlockSpec((ROW_TILE, D_MODEL), lambda i: (i + lat_tile0, PCOL_U)),
                  pl.BlockSpec((ROW_TILE, D_MODEL), lambda i: (i + lat_tile0, PCOL_SV)),
                  pl.BlockSpec((SGU_GROUPS, SGU_CHUNK, SGU_CHUNK), lambda i: (0, 0, 0)),
                  pl.BlockSpec((SGU_CHUNK, D_MODEL), lambda i: (0, 0))],
        out_specs=pl.BlockSpec((ROW_TILE, D_MODEL), lambda i: (i, 0)),
        out_shape=jax.ShapeDtypeStruct((n_lat_tiles * ROW_TILE, D_MODEL), BF16),
        compiler_params=_params("arbitrary"),
        name="sgu",
    )(p, p, w_s, b_full)


def _merge_kernel(of_ref, ob_ref, z_ref, ysgu_ref, ga_ref, gb_ref, x_ref, wa_ref, wb_ref, wo_ref,
                  gng_ref, gate_ref, nfg_ref, shift_ref, scale_ref, wr_ref, br_ref,
                  xmid_ref, h2_ref, rgate_ref, ygdn_ref):
    tm = x_ref.shape[0]
    parts = [slice(k * tm // MERGE_PARTS, (k + 1) * tm // MERGE_PARTS) for k in range(MERGE_PARTS)]
    dot = lambda a, b: jnp.dot(a, b, preferred_element_type=F32)

    for rs in parts:
        o = of_ref[rs, :].astype(F32) + ob_ref[rs, :].astype(F32)
        for h in range(N_HEADS):
            hs = slice(h * HEAD_DIM, (h + 1) * HEAD_DIM)
            oh = o[:, hs]
            inv = lax.rsqrt(jnp.mean(oh * oh, axis=-1, keepdims=True) + EPS)
            ygdn_ref[rs, hs] = (oh * inv * gng_ref[...] * z_ref[rs, hs].astype(F32)).astype(BF16)
    ya = [dot(ygdn_ref[rs, :], wa_ref[...]) for rs in parts]
    yb = [dot(ysgu_ref[rs, :], wb_ref[...]) for rs in parts]
    merged = [(ga_ref[rs, :].astype(F32) * a + gb_ref[rs, :].astype(F32) * b_).astype(BF16)
              for rs, a, b_ in zip(parts, ya, yb)]
    mix = [dot(m, wo_ref[...]) for m in merged]
    h2s = []
    for rs, mx in zip(parts, mix):
        xm = x_ref[rs, :] + gate_ref[0] * mx
        xmid_ref[rs, :] = xm
        hn = xm * lax.rsqrt(jnp.mean(xm * xm, axis=-1, keepdims=True) + EPS) * nfg_ref[...]
        h2 = hn * (1.0 + scale_ref[0]) + shift_ref[0]
        h2_ref[rs, :] = h2.astype(BF16)
        h2s.append(h2)

    wr = wr_ref[...]
    w_hi = wr.astype(BF16)
    w_lo = (wr - w_hi.astype(F32)).astype(BF16)
    logits = []
    for h2 in h2s:
        h_hi = h2.astype(BF16)
        h_lo = (h2 - h_hi.astype(F32)).astype(BF16)
        logits.append(dot(h_hi, w_hi) + dot(h_lo, w_hi) + dot(h_hi, w_lo) + br_ref[...])

    for rs, lg in zip(parts, logits):
        lane = lax.broadcasted_iota(jnp.int32, lg.shape, 1).astype(F32)
        far = float(LANES)
        gl = jnp.where(lane < N_EXPERT_GROUPS, lg, NEG_BIG)
        gmax = jnp.max(gl, axis=-1, keepdims=True)
        p_g = 1.0 / jnp.sum(jnp.exp(gl - gmax), axis=-1, keepdims=True)
        grp = jnp.min(jnp.where(gl == gmax, lane, far), axis=-1, keepdims=True)
        lo = N_EXPERT_GROUPS + EXPERTS_PER_GROUP * grp
        in_grp = jnp.where(lane >= lo, jnp.where(lane < lo + EXPERTS_PER_GROUP, 1.0, 0.0), 0.0)
        el = jnp.where(in_grp > 0.0, lg, NEG_BIG)
        m1 = jnp.max(el, axis=-1, keepdims=True)
        i1 = jnp.min(jnp.where(el == m1, lane, far), axis=-1, keepdims=True)
        el2 = jnp.where(lane == i1, NEG_BIG, el)
        m2 = jnp.max(el2, axis=-1, keepdims=True)
        i2 = jnp.min(jnp.where(el2 == m2, lane, far), axis=-1, keepdims=True)
        t = jnp.exp(m2 - m1)
        w1 = p_g / (1.0 + t)
        w2 = w1 * t
        rgate_ref[rs, :] = jnp.where(lane == 0.0, grp,
                                     jnp.where(lane == i1, w1, 0.0) + jnp.where(lane == i2, w2, 0.0))


def _merge(o_f, o_b, p, y_sgu, x2, w_a, w_b, w_o, gng, gate3, nfg, shift3, scale3, w_r, b_r,
           lat_tile0, tiles_per_batch):
    t_lat = x2.shape[0]
    n_tiles = t_lat // ROW_TILE
    lat = lambda c: pl.BlockSpec((ROW_TILE, D_MODEL), lambda i: (i + lat_tile0, c))
    own = lambda: pl.BlockSpec((ROW_TILE, D_MODEL), lambda i: (i, 0))
    mat = lambda: pl.BlockSpec((D_MODEL, D_MODEL), lambda i: (0, 0))
    vec = lambda: pl.BlockSpec((1, D_MODEL), lambda i: (0, 0))
    per_b = lambda: pl.BlockSpec((1, 1, D_MODEL), lambda i: (i // tiles_per_batch, 0, 0))
    return pl.pallas_call(
        _merge_kernel,
        grid=(n_tiles,),
        in_specs=[lat(0), lat(0), lat(PCOL_Z), own(), lat(PCOL_GA), lat(PCOL_GB), own(),
                  mat(), mat(), mat(), pl.BlockSpec((1, HEAD_DIM), lambda i: (0, 0)),
                  per_b(), vec(), per_b(), per_b(),
                  pl.BlockSpec((D_MODEL, LANES), lambda i: (0, 0)),
                  pl.BlockSpec((1, LANES), lambda i: (0, 0))],
        out_specs=[own(), own(), pl.BlockSpec((ROW_TILE, LANES), lambda i: (i, 0))],
        out_shape=[jax.ShapeDtypeStruct((t_lat, D_MODEL), F32),
                   jax.ShapeDtypeStruct((t_lat, D_MODEL), BF16),
                   jax.ShapeDtypeStruct((t_lat, LANES), F32)],
        scratch_shapes=[pltpu.VMEM((ROW_TILE, D_MODEL), BF16)],
        compiler_params=_params("arbitrary"),
        name="merge",
    )(o_f, o_b, p, y_sgu, p, p, x2, w_a, w_b, w_o, gng, gate3, nfg, shift3, scale3, w_r, b_r)


def _split3(x):
    hi = x.astype(BF16)
    r1 = x - hi.astype(F32)
    mid = r1.astype(BF16)
    return hi, mid, (r1 - mid.astype(F32)).astype(BF16)


def _moe_kernel(h_ref, rg_ref, w1_ref, w3_ref, w2_ref, xmid_ref, gate_ref, fng_ref, o_ref,
                before_ref, destc_ref, xs_ref, ys_ref, gs_ref, start_ref):
    i, g, hf = pl.program_id(0), pl.program_id(1), pl.program_id(2)
    tm = h_ref.shape[0]
    dot = lambda a, b: jnp.dot(a, b, preferred_element_type=F32)

    @pl.when((i == 0) & (g == 0) & (hf == 0))
    def _():
        r = lax.broadcasted_iota(jnp.int32, (tm, tm), 0)
        c = lax.broadcasted_iota(jnp.int32, (tm, tm), 1)
        before_ref[...] = jnp.where(r < c, 1.0, 0.0).astype(BF16)

    @pl.when((g == 0) & (hf == 0))
    def _():
        rg = rg_ref[...]
        sub = lax.broadcasted_iota(jnp.int32, (8, tm), 0).astype(F32)
        m_rows = jnp.where(rg.T[0:1, :] == sub, 1.0, 0.0)
        total_r = jnp.sum(m_rows, axis=1, keepdims=True)
        start_r = jnp.zeros_like(total_r)
        for gg in range(N_EXPERT_GROUPS - 1):
            start_r = start_r + jnp.where(sub[:, 0:1] > gg, total_r[gg:gg + 1, :], 0.0)
        rank_r = dot(m_rows.astype(BF16), before_ref[...])
        dest_r = jnp.sum(jnp.where(m_rows > 0.0, rank_r + start_r, 0.0), axis=0, keepdims=True)
        destc_ref[...] = jnp.broadcast_to(dest_r, (LANES, tm)).T
        acc = jnp.int32(0)
        for gg in range(N_EXPERT_GROUPS):
            start_ref[gg] = acc
            acc = acc + total_r[gg, 0].astype(jnp.int32)
        start_ref[N_EXPERT_GROUPS] = acc
        slot = lax.broadcasted_iota(jnp.int32, (tm, tm), 0).astype(F32)
        perm = jnp.where(dest_r == slot, 1.0, 0.0).astype(BF16)
        xs_ref[...] = dot(perm, h_ref[...]).astype(BF16)
        g_hi, g_mid, _ = _split3(rg)
        gs_ref[...] = dot(perm, g_hi) + dot(perm, g_mid)
        ys_ref[...] = jnp.zeros_like(ys_ref)

    bs = MOE_BLOCK
    first = start_ref[g] // bs
    n_unit = (start_ref[g + 1] + (bs - 1)) // bs - first

    def experts(row0, size):
        rows = pl.ds(pl.multiple_of(row0, bs), size)
        x = xs_ref[rows, :]
        gs = gs_ref[rows, :]
        lane = lax.broadcasted_iota(jnp.int32, gs.shape, 1)
        y = ys_ref[rows, :]
        for e in range(MOE_EXPERTS_PER_STEP):
            col = N_EXPERT_GROUPS + g * EXPERTS_PER_GROUP + hf * MOE_EXPERTS_PER_STEP + e
            gate = jnp.sum(jnp.where(lane == col, gs, 0.0), axis=-1, keepdims=True)
            hid = _silu(dot(x, w1_ref[e])) * dot(x, w3_ref[e]) * gate
            y = y + dot(hid.astype(BF16), w2_ref[e])
        ys_ref[rows, :] = y

    big = MOE_BLOCK_MULTIPLES[0]

    def big_block(k, carry):
        experts((first + k * big) * bs, big * bs)
        return carry

    lax.fori_loop(0, n_unit // big, big_block, 0)
    done = first + (n_unit // big) * big
    for m in MOE_BLOCK_MULTIPLES[1:]:
        take = (n_unit & m) != 0

        @pl.when(take)
        def _(done=done, m=m):
            experts(done * bs, m * bs)

        done = done + jnp.where(take, m, 0)

    @pl.when((g == pl.num_programs(1) - 1) & (hf == pl.num_programs(2) - 1))
    def _():
        slot = lax.broadcasted_iota(jnp.int32, (tm, tm), 1).astype(F32)
        perm_t = jnp.where(destc_ref[:, 0:1] == slot, 1.0, 0.0).astype(BF16)
        xo = xmid_ref[...] + gate_ref[0] * dot(perm_t, ys_ref[...].astype(BF16))
        o_ref[...] = xo * lax.rsqrt(jnp.mean(xo * xo, axis=-1, keepdims=True) + EPS) * fng_ref[...]


def _moe(h2, rgate, w1, w3, w2, xmid, gate3, fng, tiles_per_batch):
    t_lat = h2.shape[0]
    n_tiles = t_lat // MOE_TILE
    eps = MOE_EXPERTS_PER_STEP
    halves = EXPERTS_PER_GROUP // eps
    row = lambda w: pl.BlockSpec((MOE_TILE, w), lambda i, g, hf: (i, 0))
    wspec = lambda a, b: pl.BlockSpec((eps, a, b), lambda i, g, hf: (g * halves + hf, 0, 0))
    return pl.pallas_call(
        _moe_kernel,
        grid=(n_tiles, N_EXPERT_GROUPS, halves),
        in_specs=[row(D_MODEL), row(LANES),
                  wspec(D_MODEL, D_EXPERT), wspec(D_MODEL, D_EXPERT), wspec(D_EXPERT, D_MODEL),
                  row(D_MODEL),
                  pl.BlockSpec((1, 1, D_MODEL), lambda i, g, hf: (i // tiles_per_batch, 0, 0)),
                  pl.BlockSpec((1, D_MODEL), lambda i, g, hf: (0, 0))],
        out_specs=row(D_MODEL),
        out_shape=jax.ShapeDtypeStruct((t_lat, D_MODEL), F32),
        scratch_shapes=[pltpu.VMEM((MOE_TILE, MOE_TILE), BF16),
                        pltpu.VMEM((MOE_TILE, LANES), F32),
                        pltpu.VMEM((MOE_TILE, D_MODEL), BF16),
                        pltpu.VMEM((MOE_TILE, D_MODEL), F32),
                        pltpu.VMEM((MOE_TILE, LANES), F32),
                        pltpu.SMEM((N_EXPERT_GROUPS + 1,), jnp.int32)],
        compiler_params=pltpu.CompilerParams(dimension_semantics=("arbitrary",) * 3,
                                             vmem_limit_bytes=MOE_VMEM_LIMIT),
        name="moe",
    )(h2, rgate, w1, w3, w2, xmid, gate3, fng)


def kernel(x, c, ctx, c_ctx, ada_w, ada_b, norm_mix_g, w_in, conv_w, a_log, dt_bias, gdn_norm_g, sgu_ln_g, sgu_ln_b, sgu_w, sgu_b, w_branch_a, w_branch_b, w_out, norm_ffn_g, router_group_w, router_group_b, router_expert_w, router_expert_b, expert_w1, expert_w3, expert_w2, final_norm_g):
    batch, seq, d = x.shape
    ctx_len = ctx.shape[1]
    assert d == D_MODEL and ada_w.shape[0] == 1, "single-layer block with D_MODEL channels"
    assert batch * ctx_len == ROW_TILE, "context rows of all samples form one row tile"
    assert seq % MOE_TILE == 0 and ctx_len % PREP_TILE == 0 and batch + 1 <= 8
    t_lat = batch * seq
    row = lambda v: v.reshape(1, -1).astype(F32)

    cond = jnp.zeros((8, d), F32).at[:batch].set(c).at[batch].set(c_ctx)
    mod = _adaln(cond, ada_w[0], ada_b[0]).reshape(8, N_MOD, d)
    mod_row = lambda k: mod[:batch + 1, k].reshape(batch + 1, 1, d)

    w_l = w_in[0]
    w_main = jnp.concatenate([w_l[:, :COL_BETA], w_l[:, COL_Z:]], axis=1).astype(BF16)
    w_small = jnp.zeros((d, LANES), BF16).at[:, :4 * N_HEADS].set(w_l[:, COL_BETA:COL_Z].astype(BF16))
    x2 = x.reshape(t_lat, d)
    tiles_per_batch = seq // ROW_TILE
    p, small = _inproj(x2, ctx.reshape(batch * ctx_len, d), row(norm_mix_g), mod_row(0), mod_row(1),
                       w_main, w_small, row(sgu_ln_g), row(sgu_ln_b), tiles_per_batch)

    ctx_t, lat_t = ctx_len // PREP_TILE, seq // PREP_TILE
    starts = [b * ctx_t for b in range(batch)] + [batch * ctx_t + b * lat_t for b in range(batch)]
    ends = [(b + 1) * ctx_t - 1 for b in range(batch)] + [batch * ctx_t + (b + 1) * lat_t - 1 for b in range(batch)]
    conv_w8 = jnp.zeros((8, 3 * D_GDN), F32).at[:CONV_K].set(conv_w[0])
    pad_lanes = lambda v: jnp.zeros((1, LANES), F32).at[0, 2 * N_HEADS:4 * N_HEADS].set(v.reshape(-1))
    qkv, gb = _prep(p, small, conv_w8, pad_lanes(-jnp.exp(a_log[0])), pad_lanes(dt_bias[0]),
                    tuple(starts), tuple(ends))

    step_rows = GDN_CHUNK * GDN_STEP_CHUNKS
    assert ctx_len % step_rows == 0 and seq % step_rows == 0
    o_f, o_b = _gdn(qkv, gb, batch, ctx_len // step_rows, seq // step_rows)

    b_full = jnp.repeat(sgu_b[0].T, SGU_GROUP_DIM, axis=1).astype(F32)
    y_sgu = _sgu(p, sgu_w[0].astype(BF16), b_full, 1, t_lat // ROW_TILE)

    w_r = jnp.zeros((d, LANES), F32).at[:, :N_EXPERT_GROUPS].set(router_group_w[0]) \
        .at[:, N_EXPERT_GROUPS:N_EXPERT_GROUPS + N_EXPERTS].set(router_expert_w[0])
    b_r = jnp.zeros((1, LANES), F32).at[0, :N_EXPERT_GROUPS].set(router_group_b[0]) \
        .at[0, N_EXPERT_GROUPS:N_EXPERT_GROUPS + N_EXPERTS].set(router_expert_b[0])
    gng = gdn_norm_g[0].reshape(1, HEAD_DIM).astype(F32)
    xmid, h2, rgate = _merge(o_f, o_b, p, y_sgu, x2, w_branch_a[0].astype(BF16), w_branch_b[0].astype(BF16),
                             w_out[0].astype(BF16), gng, mod_row(2)[:batch], row(norm_ffn_g),
                             mod_row(3)[:batch], mod_row(4)[:batch], w_r, b_r, 1, tiles_per_batch)

    out = _moe(h2, rgate, expert_w1[0].astype(BF16), expert_w3[0].astype(BF16), expert_w2[0].astype(BF16),
               xmid, mod_row(5)[:batch], row(final_norm_g), seq // MOE_TILE)
    return out.reshape(batch, seq, d)
```

```python
import functools
import math

import jax
import jax.numpy as jnp
from jax import lax
from jax.experimental import pallas as pl
from jax.experimental.pallas import tpu as pltpu

F32 = jnp.float32
BF16 = jnp.bfloat16

D_MODEL = 1024
N_HEADS = 8
HEAD_DIM = 128
D_GDN = N_HEADS * HEAD_DIM
CONV_K = 5
GDN_CHUNK = 64
GDN_STEP_CHUNKS = 2
SGU_GROUPS = 8
SGU_GROUP_DIM = 128
D_SGU = SGU_GROUPS * SGU_GROUP_DIM
SGU_CHUNK = 128
N_EXPERT_GROUPS = 4
EXPERTS_PER_GROUP = 8
N_EXPERTS = N_EXPERT_GROUPS * EXPERTS_PER_GROUP
D_EXPERT = 256
N_MOD = 6
EPS = 1e-6
COL_BETA = 3 * D_GDN
COL_Z = COL_BETA + 4 * N_HEADS

LANES = 128
ROW_TILE = 512
INPROJ_SEGS = 4
MERGE_PARTS = 2
PREP_TILE = 256
MOE_TILE = 1024
MOE_EXPERTS_PER_STEP = 4
MOE_BLOCK = 128
MOE_BLOCK_MULTIPLES = (4, 2, 1)
VMEM_LIMIT = 48 * 1024 * 1024
MOE_VMEM_LIMIT = 56 * 1024 * 1024
NEG_BIG = -1e30

PCOL_Q, PCOL_K, PCOL_V, PCOL_Z, PCOL_U, PCOL_SV, PCOL_GA, PCOL_GB = range(8)


def _mm(a, b):
    return jnp.dot(a.astype(BF16), b.astype(BF16), preferred_element_type=F32)


def _mm_nt(a, b):
    return lax.dot_general(a.astype(BF16), b.astype(BF16), (((1,), (1,)), ((), ())),
                           preferred_element_type=F32)


def _sigmoid(x):
    return 0.5 + 0.5 * jnp.tanh(0.5 * x)


def _silu(x):
    return x * _sigmoid(x)


def _gelu_tanh(x):
    return 0.5 * x * (1.0 + jnp.tanh(math.sqrt(2.0 / math.pi) * (x + 0.044715 * (x * x * x))))


def _params(*sem):
    return pltpu.CompilerParams(dimension_semantics=sem, vmem_limit_bytes=VMEM_LIMIT)


def _adaln_kernel(c_ref, w_ref, b_ref, o_ref):
    o_ref[...] = _mm(_silu(c_ref[...]), w_ref[...]) + b_ref[...]


def _adaln(cond, w, b):
    n = w.shape[1]
    tn = 1536
    return pl.pallas_call(
        _adaln_kernel,
        grid=(n // tn,),
        in_specs=[pl.BlockSpec((8, D_MODEL), lambda j: (0, 0)),
                  pl.BlockSpec((D_MODEL, tn), lambda j: (0, j)),
                  pl.BlockSpec((1, tn), lambda j: (0, j))],
        out_specs=pl.BlockSpec((8, tn), lambda j: (0, j)),
        out_shape=jax.ShapeDtypeStruct((8, n), F32),
        compiler_params=_params("arbitrary"),
        name="adaln",
    )(cond, w, b.reshape(1, n))


def _inproj_kernel(x_ref, ctx_ref, g_ref, shift_ref, scale_ref, wm_ref, ws_ref, lng_ref, lnb_ref,
                   p_ref, small_ref, h_ref):
    i = pl.program_id(0)
    j = pl.program_id(1)

    def norm_mod(xv):
        y = xv * lax.rsqrt(jnp.mean(xv * xv, axis=-1, keepdims=True) + EPS) * g_ref[...]
        return y * (1.0 + scale_ref[0]) + shift_ref[0]

    @pl.when(j == 0)
    def _():
        @pl.when(i == 0)
        def _():
            h_ref[...] = norm_mod(ctx_ref[...]).astype(BF16)

        @pl.when(i > 0)
        def _():
            h_ref[...] = norm_mod(x_ref[...]).astype(BF16)

        small_ref[...] = jnp.dot(h_ref[...], ws_ref[...], preferred_element_type=F32)

    def raw(a):
        return a

    def gelu_ln(a):
        a = _gelu_tanh(a)
        mu = jnp.mean(a, axis=-1, keepdims=True)
        ac = a - mu
        var = jnp.mean(ac * ac, axis=-1, keepdims=True)
        return ac * lax.rsqrt(var + EPS) * lng_ref[...] + lnb_ref[...]

    epilogues = (raw, raw, raw, _silu, _gelu_tanh, gelu_ln, _sigmoid, _sigmoid)
    for step in range(len(epilogues) // INPROJ_SEGS):
        @pl.when(j == step)
        def _(step=step):
            for seg in range(INPROJ_SEGS):
                cs = slice(seg * D_MODEL, (seg + 1) * D_MODEL)
                a = jnp.dot(h_ref[...], wm_ref[:, cs], preferred_element_type=F32)
                p_ref[:, cs] = epilogues[INPROJ_SEGS * step + seg](a).astype(BF16)


def _inproj(x2, ctx2, norm_g, shift3, scale3, w_main, w_small, ln_g, ln_b, tiles_per_batch):
    t_lat = x2.shape[0]
    n_lat = t_lat // ROW_TILE
    n_tiles = n_lat + 1
    tn = INPROJ_SEGS * D_MODEL
    n_col = w_main.shape[1] // tn
    n_batch = shift3.shape[0] - 1
    sel = lambda i: jnp.where(i == 0, n_batch, (jnp.maximum(i, 1) - 1) // tiles_per_batch)
    vec = lambda: pl.BlockSpec((1, D_MODEL), lambda i, j: (0, 0))
    return pl.pallas_call(
        _inproj_kernel,
        grid=(n_tiles, n_col),
        in_specs=[pl.BlockSpec((ROW_TILE, D_MODEL), lambda i, j: (jnp.maximum(i, 1) - 1, 0)),
                  pl.BlockSpec((ROW_TILE, D_MODEL), lambda i, j: (0, 0)),
                  vec(),
                  pl.BlockSpec((1, 1, D_MODEL), lambda i, j: (sel(i), 0, 0)),
                  pl.BlockSpec((1, 1, D_MODEL), lambda i, j: (sel(i), 0, 0)),
                  pl.BlockSpec((D_MODEL, tn), lambda i, j: (0, j)),
                  pl.BlockSpec((D_MODEL, LANES), lambda i, j: (0, 0)),
                  vec(), vec()],
        out_specs=[pl.BlockSpec((ROW_TILE, tn), lambda i, j: (i, j)),
                   pl.BlockSpec((ROW_TILE, LANES), lambda i, j: (i, 0))],
        out_shape=[jax.ShapeDtypeStruct((n_tiles * ROW_TILE, n_col * tn), BF16),
                   jax.ShapeDtypeStruct((n_tiles * ROW_TILE, LANES), F32)],
        scratch_shapes=[pltpu.VMEM((ROW_TILE, D_MODEL), BF16)],
        compiler_params=_params("arbitrary", "arbitrary"),
        name="inproj",
    )(x2, ctx2, norm_g, shift3, scale3, w_main, w_small, ln_g, ln_b)


def _prep_kernel(pm_ref, pp_ref, pn_ref, cw_ref, small_ref, nega_ref, dtb_ref, qkv_ref, gb_ref, shift_ref,
                 *, first_tiles, last_tiles):
    r = pl.program_id(0)
    tr = pm_ref.shape[0]
    is_first = functools.reduce(jnp.logical_or, [r == t for t in first_tiles])
    is_last = functools.reduce(jnp.logical_or, [r == t for t in last_tiles])
    keep_prev = jnp.where(is_first, 0.0, 1.0)
    keep_next = jnp.where(is_last, 0.0, 1.0)

    half = CONV_K // 2
    offsets = [o for o in range(-half, half + 1) if o != 0]

    @pl.when(r == 0)
    def _():
        i0 = lax.broadcasted_iota(jnp.int32, (tr, tr), 0)
        i1 = lax.broadcasted_iota(jnp.int32, (tr, tr), 1)
        for m, o in enumerate(offsets):
            shift_ref[m * tr:(m + 1) * tr, :] = jnp.where(i1 == i0 + o, 1.0, 0.0).astype(BF16)

    def conv_silu(cs):
        x = pm_ref[:, cs]
        tap = lambda o: cw_ref[half + o:half + o + 1, cs]
        shifted = jnp.dot(shift_ref[...], x, preferred_element_type=F32)
        acc = tap(0) * x.astype(F32)
        for m, o in enumerate(offsets):
            acc = acc + tap(o) * shifted[m * tr:(m + 1) * tr, :]
        prev = pp_ref[:, cs].astype(F32)[8:16, :] * keep_prev
        nxt = pn_ref[:, cs].astype(F32)[0:8, :] * keep_next
        sub = lax.broadcasted_iota(jnp.int32, prev.shape, 0)
        top = jnp.zeros_like(prev)
        bot = jnp.zeros_like(prev)
        for o in range(1, half + 1):
            top = top + tap(-o) * jnp.where(sub < o, pltpu.roll(prev, o, 0), 0.0)
            bot = bot + tap(o) * jnp.where(sub >= 8 - o, pltpu.roll(nxt, 8 - o, 0), 0.0)
        return _silu(jnp.concatenate([acc[0:8] + top, acc[8:tr - 8], acc[tr - 8:tr] + bot], axis=0))

    for j, scale in ((PCOL_Q, HEAD_DIM ** -0.5), (PCOL_K, 1.0)):
        y = conv_silu(slice(j * D_GDN, (j + 1) * D_GDN))
        for h in range(N_HEADS):
            yh = y[:, h * HEAD_DIM:(h + 1) * HEAD_DIM]
            inv = lax.rsqrt(jnp.sum(yh * yh, axis=-1, keepdims=True) + EPS) * scale
            qkv_ref[:, j * D_GDN + h * HEAD_DIM:j * D_GDN + (h + 1) * HEAD_DIM] = (yh * inv).astype(BF16)
    vs = slice(PCOL_V * D_GDN, (PCOL_V + 1) * D_GDN)
    qkv_ref[:, vs] = conv_silu(vs).astype(BF16)

    s = small_ref[...]
    lane = lax.broadcasted_iota(jnp.int32, s.shape, 1)
    beta = _sigmoid(s)
    z = s + dtb_ref[...]
    softplus = jnp.maximum(z, 0.0) + jnp.log(1.0 + jnp.exp(-jnp.abs(z)))
    g = nega_ref[...] * softplus
    gb_ref[...] = jnp.where(lane < 2 * N_HEADS, beta, jnp.where(lane < 4 * N_HEADS, g, 0.0))


def _prep(p, small, conv_w8, nega, dtb, first_tiles, last_tiles):
    ta = p.shape[0]
    n_tiles = ta // PREP_TILE
    sub = PREP_TILE // 16
    n_sub = ta // 16
    width = 3 * D_GDN
    kern = functools.partial(_prep_kernel, first_tiles=first_tiles, last_tiles=last_tiles)
    return pl.pallas_call(
        kern,
        grid=(n_tiles,),
        in_specs=[pl.BlockSpec((PREP_TILE, width), lambda r: (r, 0)),
                  pl.BlockSpec((16, width), lambda r: (jnp.maximum(r * sub - 1, 0), 0)),
                  pl.BlockSpec((16, width), lambda r: (jnp.minimum((r + 1) * sub, n_sub - 1), 0)),
                  pl.BlockSpec((8, width), lambda r: (0, 0)),
                  pl.BlockSpec((PREP_TILE, LANES), lambda r: (r, 0)),
                  pl.BlockSpec((1, LANES), lambda r: (0, 0)),
                  pl.BlockSpec((1, LANES), lambda r: (0, 0))],
        out_specs=[pl.BlockSpec((PREP_TILE, width), lambda r: (r, 0)),
                   pl.BlockSpec((PREP_TILE, LANES), lambda r: (r, 0))],
        out_shape=[jax.ShapeDtypeStruct((ta, width), BF16),
                   jax.ShapeDtypeStruct((ta, LANES), F32)],
        scratch_shapes=[pltpu.VMEM(((CONV_K - 1) * PREP_TILE, PREP_TILE), BF16)],
        compiler_params=_params("arbitrary"),
        name="prep",
    )(p, p, p, conv_w8, small, nega, dtb)


def _gdn_decays(d, gb):
    c = GDN_CHUNK
    row = lax.broadcasted_iota(jnp.int32, (c, c), 0)
    col = lax.broadcasted_iota(jnp.int32, (c, c), 1)
    incl = row >= col if d == 0 else row <= col
    lane = lax.broadcasted_iota(jnp.int32, gb.shape, 1)
    g_only = jnp.where(lane >= 2 * N_HEADS, jnp.where(lane < 4 * N_HEADS, gb, 0.0), 0.0)
    tri = jnp.where(incl, 1.0, 0.0).astype(BF16)
    g_hi = g_only.astype(BF16)
    g_r1 = g_only - g_hi.astype(F32)
    g_mid = g_r1.astype(BF16)
    g_lo = (g_r1 - g_mid.astype(F32)).astype(BF16)
    dot = lambda a, b: jnp.dot(a, b, preferred_element_type=F32)
    gcum = dot(tri, g_hi) + dot(tri, g_mid) + dot(tri, g_lo)
    g_end = gcum[c - 1:c, :] if d == 0 else gcum[0:1, :]
    return gcum, gcum.T, jnp.exp(gcum), jnp.exp(g_end - gcum), jnp.exp(g_end)


def _gdn_kernel(qf, kf, vf, gbf, qb, kb, vb, gbb, of_ref, ob_ref, s_ref):
    @pl.when(pl.program_id(1) == 0)
    def _():
        s_ref[...] = jnp.zeros_like(s_ref)

    c = GDN_CHUNK
    n_sub = qf.shape[0] // c
    row = lax.broadcasted_iota(jnp.int32, (c, c), 0)
    col = lax.broadcasted_iota(jnp.int32, (c, c), 1)
    eye = jnp.where(row == col, 1.0, 0.0)
    masks = ((row >= col, row > col), (row <= col, row < col))
    refs = ((qf, kf, vf, gbf, of_ref), (qb, kb, vb, gbb, ob_ref))

    def rows(d, sub):
        k = sub if d == 0 else n_sub - 1 - sub
        return slice(k * c, (k + 1) * c)

    scans = [(d, sub) for sub in range(n_sub) for d in range(2)]
    gbv = {ds: refs[ds[0]][3][rows(*ds), :] for ds in scans}
    dec = {ds: _gdn_decays(ds[0], gbv[ds]) for ds in scans}
    chains = [(d, sub, h) for d, sub in scans for h in range(N_HEADS)]

    st = []
    for d, sub, h in chains:
        q_ref, k_ref, v_ref, _, _ = refs[d]
        rs, hs = rows(d, sub), slice(h * HEAD_DIM, (h + 1) * HEAD_DIM)
        q, k, v = q_ref[rs, hs], k_ref[rs, hs], v_ref[rs, hs]
        both = _mm_nt(jnp.concatenate([q, k], axis=0), k)
        st.append(dict(q=q.astype(F32), k=k.astype(F32), v=v.astype(F32), qk=both[:c], kk=both[c:]))

    for (d, sub, h), e in zip(chains, st):
        cb = d * N_HEADS + h
        cg = 2 * N_HEADS + cb
        gcum, gcum_t, exp_g, exp_rest, exp_end = dec[d, sub]
        incl, strict = masks[d]
        e["beta"] = gbv[d, sub][:, cb:cb + 1]
        diff = gcum[:, cg:cg + 1] - gcum_t[cg:cg + 1, :]
        decay = jnp.where(incl, jnp.exp(jnp.minimum(diff, 0.0)), 0.0)
        e["y"] = jnp.where(strict, -(e["beta"] * e["kk"] * decay), 0.0)
        e["qk"] = e["qk"] * decay
        e["eg"] = exp_g[:, cg:cg + 1]
        e["er"] = exp_rest[:, cg:cg + 1]
        e["ee"] = exp_end[:, cg:cg + 1]

    levels = range(GDN_CHUNK.bit_length() - 1)
    sels = []
    for d in range(2):
        inner, outer = (col, row) if d == 0 else (row, col)
        sels.append([((outer >> lvl) & 1 == 1) & ((inner >> lvl) == (outer >> lvl) - 1) for lvl in levels])
    for (d, sub, h), e in zip(chains, st):
        e["t"] = eye + jnp.where(sels[d][0], e["y"], 0.0)
    def active_rows(d, b):
        return [(2 * p + 1 - d) * b for p in range(c // (2 * b))]

    def pick(x, starts, b):
        return jnp.concatenate([x[s:s + b] for s in starts], axis=0)

    def place(pieces, starts, b, base):
        out = []
        for blk in range(c // b):
            cur = None if base is None else base[blk * b:(blk + 1) * b]
            if blk * b in starts:
                piece = pieces[starts.index(blk * b) * b:(starts.index(blk * b) + 1) * b]
                cur = piece if cur is None else cur + piece
            out.append(jnp.zeros((b, c), F32) if cur is None else cur)
        return jnp.concatenate(out, axis=0)

    for lvl in levels[1:]:
        b = 1 << lvl
        if b % 8 == 0:
            for (d, sub, h), e in zip(chains, st):
                act = active_rows(d, b)
                yd = _mm(pick(jnp.where(sels[d][lvl], e["y"], 0.0), act, b), e["t"])
                e["yd"] = place(yd, act, b, None)
            for (d, sub, h), e in zip(chains, st):
                act = active_rows(d, b)
                e["t"] = place(_mm(pick(e["t"], act, b), e["yd"]), act, b, e["t"])
        else:
            for (d, sub, h), e in zip(chains, st):
                e["yd"] = _mm(jnp.where(sels[d][lvl], e["y"], 0.0), e["t"])
            for e in st:
                e["t"] = e["t"] + _mm(e["t"], e["yd"])

    for e in st:
        kb_ = e["k"] * e["beta"]
        uw = _mm(e["t"], jnp.concatenate([e["v"] * e["beta"], kb_ * e["eg"]], axis=-1))
        e["u"], e["w"] = uw[:, :HEAD_DIM], uw[:, HEAD_DIM:]

    state = {(d, h): s_ref[d, h] for d in range(2) for h in range(N_HEADS)}
    for step in range(n_sub):
        now = [(key, e) for key, e in zip(chains, st) if key[1] == step]
        for (d, sub, h), e in now:
            ws = _mm(jnp.concatenate([e["w"], e["q"] * e["eg"]], axis=0), state[d, h])
            e["v_new"] = e["u"] - ws[:c]
            e["o"] = ws[c:]
        for (d, sub, h), e in now:
            k_dec_t = (e["k"] * e["er"]).T
            out = _mm(jnp.concatenate([e["qk"], k_dec_t], axis=0), e["v_new"])
            refs[d][4][rows(d, sub), h * HEAD_DIM:(h + 1) * HEAD_DIM] = (e["o"] + out[:c]).astype(BF16)
            state[d, h] = state[d, h] * e["ee"] + out[c:]
    for (d, h), s in state.items():
        s_ref[d, h] = s


def _gdn(qkv, gb, batch, ctx_blocks, lat_blocks):
    ta = qkv.shape[0]
    rows = GDN_CHUNK * GDN_STEP_CHUNKS
    n_steps = ctx_blocks + lat_blocks
    lat0 = batch * ctx_blocks

    def fwd_blk(b, s):
        return jnp.where(s < ctx_blocks, b * ctx_blocks + s, lat0 + b * lat_blocks + (s - ctx_blocks))

    def bwd_blk(b, s):
        return jnp.where(s < ctx_blocks, b * ctx_blocks + (ctx_blocks - 1 - s),
                         lat0 + b * lat_blocks + (lat_blocks - 1 - (s - ctx_blocks)))

    def specs(blk):
        col = lambda j: pl.BlockSpec((rows, D_MODEL), lambda b, s: (blk(b, s), j))
        return [col(0), col(1), col(2), pl.BlockSpec((rows, LANES), lambda b, s: (blk(b, s), 0))]

    out = lambda blk: pl.BlockSpec((rows, D_MODEL), lambda b, s: (blk(b, s), 0))
    return pl.pallas_call(
        _gdn_kernel,
        grid=(batch, n_steps),
        in_specs=specs(fwd_blk) + specs(bwd_blk),
        out_specs=[out(fwd_blk), out(bwd_blk)],
        out_shape=[jax.ShapeDtypeStruct((ta, D_MODEL), BF16)] * 2,
        scratch_shapes=[pltpu.VMEM((2, N_HEADS, HEAD_DIM, HEAD_DIM), F32)],
        compiler_params=_params("arbitrary", "arbitrary"),
        name="gdn",
    )(qkv, qkv, qkv, gb, qkv, qkv, qkv, gb)


def _merge_kernel(of_ref, ob_ref, z_ref, su_ref, sv_ref, ga_ref, gb_ref, x_ref, ws_ref, bs_ref,
                  wa_ref, wb_ref, wo_ref, gng_ref, gate_ref, nfg_ref, shift_ref, scale_ref, wr_ref, br_ref,
                  xmid_ref, h2_ref, rgate_ref, ygdn_ref, ysgu_ref):
    tm = x_ref.shape[0]
    parts = [slice(k * tm // MERGE_PARTS, (k + 1) * tm // MERGE_PARTS) for k in range(MERGE_PARTS)]
    dot = lambda a, b: jnp.dot(a, b, preferred_element_type=F32)

    for ch in range(tm // SGU_CHUNK):
        rs = slice(ch * SGU_CHUNK, (ch + 1) * SGU_CHUNK)
        for g in range(SGU_GROUPS):
            cs = slice(g * SGU_GROUP_DIM, (g + 1) * SGU_GROUP_DIM)
            mixed = dot(ws_ref[g], sv_ref[rs, cs]) + bs_ref[:, cs]
            ysgu_ref[rs, cs] = (su_ref[rs, cs].astype(F32) * mixed).astype(BF16)

    for rs in parts:
        o = of_ref[rs, :].astype(F32) + ob_ref[rs, :].astype(F32)
        for h in range(N_HEADS):
            hs = slice(h * HEAD_DIM, (h + 1) * HEAD_DIM)
            oh = o[:, hs]
            inv = lax.rsqrt(jnp.mean(oh * oh, axis=-1, keepdims=True) + EPS)
            ygdn_ref[rs, hs] = (oh * inv * gng_ref[...] * z_ref[rs, hs].astype(F32)).astype(BF16)
    ya = [dot(ygdn_ref[rs, :], wa_ref[...]) for rs in parts]
    yb = [dot(ysgu_ref[rs, :], wb_ref[...]) for rs in parts]
    merged = [(ga_ref[rs, :].astype(F32) * a + gb_ref[rs, :].astype(F32) * b_).astype(BF16)
              for rs, a, b_ in zip(parts, ya, yb)]
    mix = [dot(m, wo_ref[...]) for m in merged]
    h2s = []
    for rs, mx in zip(parts, mix):
        xm = x_ref[rs, :] + gate_ref[0] * mx
        xmid_ref[rs, :] = xm
        hn = xm * lax.rsqrt(jnp.mean(xm * xm, axis=-1, keepdims=True) + EPS) * nfg_ref[...]
        h2 = hn * (1.0 + scale_ref[0]) + shift_ref[0]
        h2_ref[rs, :] = h2.astype(BF16)
        h2s.append(h2)

    wr = wr_ref[...]
    w_hi = wr.astype(BF16)
    w_lo = (wr - w_hi.astype(F32)).astype(BF16)
    logits = []
    for h2 in h2s:
        h_hi = h2.astype(BF16)
        h_lo = (h2 - h_hi.astype(F32)).astype(BF16)
        logits.append(dot(h_hi, w_hi) + dot(h_lo, w_hi) + dot(h_hi, w_lo) + br_ref[...])

    for rs, lg in zip(parts, logits):
        lane = lax.broadcasted_iota(jnp.int32, lg.shape, 1).astype(F32)
        far = float(LANES)
        gl = jnp.where(lane < N_EXPERT_GROUPS, lg, NEG_BIG)
        gmax = jnp.max(gl, axis=-1, keepdims=True)
        p_g = 1.0 / jnp.sum(jnp.exp(gl - gmax), axis=-1, keepdims=True)
        grp = jnp.min(jnp.where(gl == gmax, lane, far), axis=-1, keepdims=True)
        lo = N_EXPERT_GROUPS + EXPERTS_PER_GROUP * grp
        in_grp = jnp.where(lane >= lo, jnp.where(lane < lo + EXPERTS_PER_GROUP, 1.0, 0.0), 0.0)
        el = jnp.where(in_grp > 0.0, lg, NEG_BIG)
        m1 = jnp.max(el, axis=-1, keepdims=True)
        i1 = jnp.min(jnp.where(el == m1, lane, far), axis=-1, keepdims=True)
        el2 = jnp.where(lane == i1, NEG_BIG, el)
        m2 = jnp.max(el2, axis=-1, keepdims=True)
        i2 = jnp.min(jnp.where(el2 == m2, lane, far), axis=-1, keepdims=True)
        t = jnp.exp(m2 - m1)
        w1 = p_g / (1.0 + t)
        w2 = w1 * t
        rgate_ref[rs, :] = jnp.where(lane == 0.0, grp,
                                     jnp.where(lane == i1, w1, 0.0) + jnp.where(lane == i2, w2, 0.0))


def _merge(o_f, o_b, p, x2, w_s, b_s, w_a, w_b, w_o, gng, gate3, nfg, shift3, scale3, w_r, b_r,
           lat_tile0, tiles_per_batch):
    t_lat = x2.shape[0]
    n_tiles = t_lat // ROW_TILE
    lat = lambda c: pl.BlockSpec((ROW_TILE, D_MODEL), lambda i: (i + lat_tile0, c))
    own = lambda: pl.BlockSpec((ROW_TILE, D_MODEL), lambda i: (i, 0))
    mat = lambda: pl.BlockSpec((D_MODEL, D_MODEL), lambda i: (0, 0))
    vec = lambda: pl.BlockSpec((1, D_MODEL), lambda i: (0, 0))
    per_b = lambda: pl.BlockSpec((1, 1, D_MODEL), lambda i: (i // tiles_per_batch, 0, 0))
    return pl.pallas_call(
        _merge_kernel,
        grid=(n_tiles,),
        in_specs=[lat(0), lat(0), lat(PCOL_Z), lat(PCOL_U), lat(PCOL_SV), lat(PCOL_GA), lat(PCOL_GB), own(),
                  pl.BlockSpec((SGU_GROUPS, SGU_CHUNK, SGU_CHUNK), lambda i: (0, 0, 0)),
                  pl.BlockSpec((SGU_CHUNK, D_MODEL), lambda i: (0, 0)),
                  mat(), mat(), mat(), pl.BlockSpec((1, HEAD_DIM), lambda i: (0, 0)),
                  per_b(), vec(), per_b(), per_b(),
                  pl.BlockSpec((D_MODEL, LANES), lambda i: (0, 0)),
                  pl.BlockSpec((1, LANES), lambda i: (0, 0))],
        out_specs=[own(), own(), pl.BlockSpec((ROW_TILE, LANES), lambda i: (i, 0))],
        out_shape=[jax.ShapeDtypeStruct((t_lat, D_MODEL), F32),
                   jax.ShapeDtypeStruct((t_lat, D_MODEL), BF16),
                   jax.ShapeDtypeStruct((t_lat, LANES), F32)],
        scratch_shapes=[pltpu.VMEM((ROW_TILE, D_MODEL), BF16), pltpu.VMEM((ROW_TILE, D_MODEL), BF16)],
        compiler_params=_params("arbitrary"),
        name="merge",
    )(o_f, o_b, p, p, p, p, p, x2, w_s, b_s, w_a, w_b, w_o, gng, gate3, nfg, shift3, scale3, w_r, b_r)


def _split3(x):
    hi = x.astype(BF16)
    r1 = x - hi.astype(F32)
    mid = r1.astype(BF16)
    return hi, mid, (r1 - mid.astype(F32)).astype(BF16)


def _moe_kernel(h_ref, rg_ref, w1_ref, w3_ref, w2_ref, xmid_ref, gate_ref, fng_ref, o_ref,
                before_ref, destc_ref, xs_ref, ys_ref, gs_ref, start_ref):
    i, g, hf = pl.program_id(0), pl.program_id(1), pl.program_id(2)
    tm = h_ref.shape[0]
    dot = lambda a, b: jnp.dot(a, b, preferred_element_type=F32)

    @pl.when((i == 0) & (g == 0) & (hf == 0))
    def _():
        r = lax.broadcasted_iota(jnp.int32, (tm, tm), 0)
        c = lax.broadcasted_iota(jnp.int32, (tm, tm), 1)
        before_ref[...] = jnp.where(r < c, 1.0, 0.0).astype(BF16)

    @pl.when((g == 0) & (hf == 0))
    def _():
        rg = rg_ref[...]
        sub = lax.broadcasted_iota(jnp.int32, (8, tm), 0).astype(F32)
        m_rows = jnp.where(rg.T[0:1, :] == sub, 1.0, 0.0)
        total_r = jnp.sum(m_rows, axis=1, keepdims=True)
        start_r = jnp.zeros_like(total_r)
        for gg in range(N_EXPERT_GROUPS - 1):
            start_r = start_r + jnp.where(sub[:, 0:1] > gg, total_r[gg:gg + 1, :], 0.0)
        rank_r = dot(m_rows.astype(BF16), before_ref[...])
        dest_r = jnp.sum(jnp.where(m_rows > 0.0, rank_r + start_r, 0.0), axis=0, keepdims=True)
        destc_ref[...] = jnp.broadcast_to(dest_r, (LANES, tm)).T
        acc = jnp.int32(0)
        for gg in range(N_EXPERT_GROUPS):
            start_ref[gg] = acc
            acc = acc + total_r[gg, 0].astype(jnp.int32)
        start_ref[N_EXPERT_GROUPS] = acc
        slot = lax.broadcasted_iota(jnp.int32, (tm, tm), 0).astype(F32)
        perm = jnp.where(dest_r == slot, 1.0, 0.0).astype(BF16)
        xs_ref[...] = dot(perm, h_ref[...]).astype(BF16)
        g_hi, g_mid, _ = _split3(rg)
        gs_ref[...] = dot(perm, g_hi) + dot(perm, g_mid)
        ys_ref[...] = jnp.zeros_like(ys_ref)

    bs = MOE_BLOCK
    first = start_ref[g] // bs
    n_unit = (start_ref[g + 1] + (bs - 1)) // bs - first

    def experts(row0, size):
        rows = pl.ds(pl.multiple_of(row0, bs), size)
        x = xs_ref[rows, :]
        gs = gs_ref[rows, :]
        lane = lax.broadcasted_iota(jnp.int32, gs.shape, 1)
        y = ys_ref[rows, :]
        for e in range(MOE_EXPERTS_PER_STEP):
            col = N_EXPERT_GROUPS + g * EXPERTS_PER_GROUP + hf * MOE_EXPERTS_PER_STEP + e
            gate = jnp.sum(jnp.where(lane == col, gs, 0.0), axis=-1, keepdims=True)
            hid = _silu(dot(x, w1_ref[e])) * dot(x, w3_ref[e]) * gate
            y = y + dot(hid.astype(BF16), w2_ref[e])
        ys_ref[rows, :] = y

    big = MOE_BLOCK_MULTIPLES[0]

    def big_block(k, carry):
        experts((first + k * big) * bs, big * bs)
        return carry

    lax.fori_loop(0, n_unit // big, big_block, 0)
    done = first + (n_unit // big) * big
    for m in MOE_BLOCK_MULTIPLES[1:]:
        take = (n_unit & m) != 0

        @pl.when(take)
        def _(done=done, m=m):
            experts(done * bs, m * bs)

        done = done + jnp.where(take, m, 0)

    @pl.when((g == pl.num_programs(1) - 1) & (hf == pl.num_programs(2) - 1))
    def _():
        slot = lax.broadcasted_iota(jnp.int32, (tm, tm), 1).astype(F32)
        perm_t = jnp.where(destc_ref[:, 0:1] == slot, 1.0, 0.0).astype(BF16)
        xo = xmid_ref[...] + gate_ref[0] * dot(perm_t, ys_ref[...].astype(BF16))
        o_ref[...] = xo * lax.rsqrt(jnp.mean(xo * xo, axis=-1, keepdims=True) + EPS) * fng_ref[...]


def _moe(h2, rgate, w1, w3, w2, xmid, gate3, fng, tiles_per_batch):
    t_lat = h2.shape[0]
    n_tiles = t_lat // MOE_TILE
    eps = MOE_EXPERTS_PER_STEP
    halves = EXPERTS_PER_GROUP // eps
    row = lambda w: pl.BlockSpec((MOE_TILE, w), lambda i, g, hf: (i, 0))
    wspec = lambda a, b: pl.BlockSpec((eps, a, b), lambda i, g, hf: (g * halves + hf, 0, 0))
    return pl.pallas_call(
        _moe_kernel,
        grid=(n_tiles, N_EXPERT_GROUPS, halves),
        in_specs=[row(D_MODEL), row(LANES),
                  wspec(D_MODEL, D_EXPERT), wspec(D_MODEL, D_EXPERT), wspec(D_EXPERT, D_MODEL),
                  row(D_MODEL),
                  pl.BlockSpec((1, 1, D_MODEL), lambda i, g, hf: (i // tiles_per_batch, 0, 0)),
                  pl.BlockSpec((1, D_MODEL), lambda i, g, hf: (0, 0))],
        out_specs=row(D_MODEL),
        out_shape=jax.ShapeDtypeStruct((t_lat, D_MODEL), F32),
        scratch_shapes=[pltpu.VMEM((MOE_TILE, MOE_TILE), BF16),
                        pltpu.VMEM((MOE_TILE, LANES), F32),
                        pltpu.VMEM((MOE_TILE, D_MODEL), BF16),
                        pltpu.VMEM((MOE_TILE, D_MODEL), F32),
                        pltpu.VMEM((MOE_TILE, LANES), F32),
                        pltpu.SMEM((N_EXPERT_GROUPS + 1,), jnp.int32)],
        compiler_params=pltpu.CompilerParams(dimension_semantics=("arbitrary",) * 3,
                                             vmem_limit_bytes=MOE_VMEM_LIMIT),
        name="moe",
    )(h2, rgate, w1, w3, w2, xmid, gate3, fng)


def kernel(x, c, ctx, c_ctx, ada_w, ada_b, norm_mix_g, w_in, conv_w, a_log, dt_bias, gdn_norm_g, sgu_ln_g, sgu_ln_b, sgu_w, sgu_b, w_branch_a, w_branch_b, w_out, norm_ffn_g, router_group_w, router_group_b, router_expert_w, router_expert_b, expert_w1, expert_w3, expert_w2, final_norm_g):
    batch, seq, d = x.shape
    ctx_len = ctx.shape[1]
    assert d == D_MODEL and ada_w.shape[0] == 1, "single-layer block with D_MODEL channels"
    assert batch * ctx_len == ROW_TILE, "context rows of all samples form one row tile"
    assert seq % MOE_TILE == 0 and ctx_len % PREP_TILE == 0 and batch + 1 <= 8
    t_lat = batch * seq
    row = lambda v: v.reshape(1, -1).astype(F32)

    cond = jnp.zeros((8, d), F32).at[:batch].set(c).at[batch].set(c_ctx)
    mod = _adaln(cond, ada_w[0], ada_b[0]).reshape(8, N_MOD, d)
    mod_row = lambda k: mod[:batch + 1, k].reshape(batch + 1, 1, d)

    w_l = w_in[0]
    w_main = jnp.concatenate([w_l[:, :COL_BETA], w_l[:, COL_Z:]], axis=1).astype(BF16)
    w_small = jnp.zeros((d, LANES), BF16).at[:, :4 * N_HEADS].set(w_l[:, COL_BETA:COL_Z].astype(BF16))
    x2 = x.reshape(t_lat, d)
    tiles_per_batch = seq // ROW_TILE
    p, small = _inproj(x2, ctx.reshape(batch * ctx_len, d), row(norm_mix_g), mod_row(0), mod_row(1),
                       w_main, w_small, row(sgu_ln_g), row(sgu_ln_b), tiles_per_batch)

    ctx_t, lat_t = ctx_len // PREP_TILE, seq // PREP_TILE
    starts = [b * ctx_t for b in range(batch)] + [batch * ctx_t + b * lat_t for b in range(batch)]
    ends = [(b + 1) * ctx_t - 1 for b in range(batch)] + [batch * ctx_t + (b + 1) * lat_t - 1 for b in range(batch)]
    conv_w8 = jnp.zeros((8, 3 * D_GDN), F32).at[:CONV_K].set(conv_w[0])
    pad_lanes = lambda v: jnp.zeros((1, LANES), F32).at[0, 2 * N_HEADS:4 * N_HEADS].set(v.reshape(-1))
    qkv, gb = _prep(p, small, conv_w8, pad_lanes(-jnp.exp(a_log[0])), pad_lanes(dt_bias[0]),
                    tuple(starts), tuple(ends))

    step_rows = GDN_CHUNK * GDN_STEP_CHUNKS
    assert ctx_len % step_rows == 0 and seq % step_rows == 0
    o_f, o_b = _gdn(qkv, gb, batch, ctx_len // step_rows, seq // step_rows)

    b_full = jnp.repeat(sgu_b[0].T, SGU_GROUP_DIM, axis=1).astype(F32)

    w_r = jnp.zeros((d, LANES), F32).at[:, :N_EXPERT_GROUPS].set(router_group_w[0]) \
        .at[:, N_EXPERT_GROUPS:N_EXPERT_GROUPS + N_EXPERTS].set(router_expert_w[0])
    b_r = jnp.zeros((1, LANES), F32).at[0, :N_EXPERT_GROUPS].set(router_group_b[0]) \
        .at[0, N_EXPERT_GROUPS:N_EXPERT_GROUPS + N_EXPERTS].set(router_expert_b[0])
    gng = gdn_norm_g[0].reshape(1, HEAD_DIM).astype(F32)
    xmid, h2, rgate = _merge(o_f, o_b, p, x2, sgu_w[0].astype(BF16), b_full,
                             w_branch_a[0].astype(BF16), w_branch_b[0].astype(BF16),
                             w_out[0].astype(BF16), gng, mod_row(2)[:batch], row(norm_ffn_g),
                             mod_row(3)[:batch], mod_row(4)[:batch], w_r, b_r, 1, tiles_per_batch)

    out = _moe(h2, rgate, expert_w1[0].astype(BF16), expert_w3[0].astype(BF16), expert_w2[0].astype(BF16),
               xmid, mod_row(5)[:batch], row(final_norm_g), seq // MOE_TILE)
    return out.reshape(batch, seq, d)
```

```python
import functools
import math

import jax
import jax.numpy as jnp
from jax import lax
from jax.experimental import pallas as pl
from jax.experimental.pallas import tpu as pltpu

F32 = jnp.float32
BF16 = jnp.bfloat16

D_MODEL = 1024
N_HEADS = 8
HEAD_DIM = 128
D_GDN = N_HEADS * HEAD_DIM
CONV_K = 5
GDN_CHUNK = 64
GDN_STEP_CHUNKS = 4
SGU_GROUPS = 8
SGU_GROUP_DIM = 128
D_SGU = SGU_GROUPS * SGU_GROUP_DIM
SGU_CHUNK = 128
N_EXPERT_GROUPS = 4
EXPERTS_PER_GROUP = 8
N_EXPERTS = N_EXPERT_GROUPS * EXPERTS_PER_GROUP
D_EXPERT = 256
N_MOD = 6
EPS = 1e-6
COL_BETA = 3 * D_GDN
COL_Z = COL_BETA + 4 * N_HEADS

LANES = 128
ROW_TILE = 512
INPROJ_SEGS = 4
MERGE_PARTS = 2
PREP_TILE = 256
MOE_TILE = 1024
MOE_EXPERTS_PER_STEP = 4
MOE_BLOCK = 128
MOE_BLOCK_MULTIPLES = (4, 2, 1)
VMEM_LIMIT = 48 * 1024 * 1024
MOE_VMEM_LIMIT = 56 * 1024 * 1024
NEG_BIG = -1e30

PCOL_Q, PCOL_K, PCOL_V, PCOL_Z, PCOL_U, PCOL_SV, PCOL_GA, PCOL_GB = range(8)


def _mm(a, b):
    return jnp.dot(a.astype(BF16), b.astype(BF16), preferred_element_type=F32)


def _mm_nt(a, b):
    return lax.dot_general(a.astype(BF16), b.astype(BF16), (((1,), (1,)), ((), ())),
                           preferred_element_type=F32)


def _sigmoid(x):
    return 0.5 + 0.5 * jnp.tanh(0.5 * x)


def _silu(x):
    return x * _sigmoid(x)


def _gelu_tanh(x):
    return 0.5 * x * (1.0 + jnp.tanh(math.sqrt(2.0 / math.pi) * (x + 0.044715 * (x * x * x))))


def _params(*sem):
    return pltpu.CompilerParams(dimension_semantics=sem, vmem_limit_bytes=VMEM_LIMIT)


def _adaln_kernel(c_ref, w_ref, b_ref, o_ref):
    o_ref[...] = _mm(_silu(c_ref[...]), w_ref[...]) + b_ref[...]


def _adaln(cond, w, b):
    n = w.shape[1]
    tn = 1536
    return pl.pallas_call(
        _adaln_kernel,
        grid=(n // tn,),
        in_specs=[pl.BlockSpec((8, D_MODEL), lambda j: (0, 0)),
                  pl.BlockSpec((D_MODEL, tn), lambda j: (0, j)),
                  pl.BlockSpec((1, tn), lambda j: (0, j))],
        out_specs=pl.BlockSpec((8, tn), lambda j: (0, j)),
        out_shape=jax.ShapeDtypeStruct((8, n), F32),
        compiler_params=_params("arbitrary"),
        name="adaln",
    )(cond, w, b.reshape(1, n))


def _inproj_kernel(x_ref, ctx_ref, g_ref, shift_ref, scale_ref, wm_ref, ws_ref, lng_ref, lnb_ref,
                   p_ref, small_ref, h_ref):
    i = pl.program_id(0)
    j = pl.program_id(1)

    def norm_mod(xv):
        y = xv * lax.rsqrt(jnp.mean(xv * xv, axis=-1, keepdims=True) + EPS) * g_ref[...]
        return y * (1.0 + scale_ref[0]) + shift_ref[0]

    @pl.when(j == 0)
    def _():
        @pl.when(i == 0)
        def _():
            h_ref[...] = norm_mod(ctx_ref[...]).astype(BF16)

        @pl.when(i > 0)
        def _():
            h_ref[...] = norm_mod(x_ref[...]).astype(BF16)

        small_ref[...] = jnp.dot(h_ref[...], ws_ref[...], preferred_element_type=F32)

    def raw(a):
        return a

    def gelu_ln(a):
        a = _gelu_tanh(a)
        mu = jnp.mean(a, axis=-1, keepdims=True)
        ac = a - mu
        var = jnp.mean(ac * ac, axis=-1, keepdims=True)
        return ac * lax.rsqrt(var + EPS) * lng_ref[...] + lnb_ref[...]

    epilogues = (raw, raw, raw, _silu, _gelu_tanh, gelu_ln, _sigmoid, _sigmoid)
    for step in range(len(epilogues) // INPROJ_SEGS):
        @pl.when(j == step)
        def _(step=step):
            for seg in range(INPROJ_SEGS):
                cs = slice(seg * D_MODEL, (seg + 1) * D_MODEL)
                a = jnp.dot(h_ref[...], wm_ref[:, cs], preferred_element_type=F32)
                p_ref[:, cs] = epilogues[INPROJ_SEGS * step + seg](a).astype(BF16)


def _inproj(x2, ctx2, norm_g, shift3, scale3, w_main, w_small, ln_g, ln_b, tiles_per_batch):
    t_lat = x2.shape[0]
    n_lat = t_lat // ROW_TILE
    n_tiles = n_lat + 1
    tn = INPROJ_SEGS * D_MODEL
    n_col = w_main.shape[1] // tn
    n_batch = shift3.shape[0] - 1
    sel = lambda i: jnp.where(i == 0, n_batch, (jnp.maximum(i, 1) - 1) // tiles_per_batch)
    vec = lambda: pl.BlockSpec((1, D_MODEL), lambda i, j: (0, 0))
    return pl.pallas_call(
        _inproj_kernel,
        grid=(n_tiles, n_col),
        in_specs=[pl.BlockSpec((ROW_TILE, D_MODEL), lambda i, j: (jnp.maximum(i, 1) - 1, 0)),
                  pl.BlockSpec((ROW_TILE, D_MODEL), lambda i, j: (0, 0)),
                  vec(),
                  pl.BlockSpec((1, 1, D_MODEL), lambda i, j: (sel(i), 0, 0)),
                  pl.BlockSpec((1, 1, D_MODEL), lambda i, j: (sel(i), 0, 0)),
                  pl.BlockSpec((D_MODEL, tn), lambda i, j: (0, j)),
                  pl.BlockSpec((D_MODEL, LANES), lambda i, j: (0, 0)),
                  vec(), vec()],
        out_specs=[pl.BlockSpec((ROW_TILE, tn), lambda i, j: (i, j)),
                   pl.BlockSpec((ROW_TILE, LANES), lambda i, j: (i, 0))],
        out_shape=[jax.ShapeDtypeStruct((n_tiles * ROW_TILE, n_col * tn), BF16),
                   jax.ShapeDtypeStruct((n_tiles * ROW_TILE, LANES), F32)],
        scratch_shapes=[pltpu.VMEM((ROW_TILE, D_MODEL), BF16)],
        compiler_params=_params("arbitrary", "arbitrary"),
        name="inproj",
    )(x2, ctx2, norm_g, shift3, scale3, w_main, w_small, ln_g, ln_b)


def _prep_kernel(pm_ref, pp_ref, pn_ref, cw_ref, small_ref, nega_ref, dtb_ref, qkv_ref, gb_ref, shift_ref,
                 *, first_tiles, last_tiles):
    r = pl.program_id(0)
    tr = pm_ref.shape[0]
    is_first = functools.reduce(jnp.logical_or, [r == t for t in first_tiles])
    is_last = functools.reduce(jnp.logical_or, [r == t for t in last_tiles])
    keep_prev = jnp.where(is_first, 0.0, 1.0)
    keep_next = jnp.where(is_last, 0.0, 1.0)

    half = CONV_K // 2
    offsets = [o for o in range(-half, half + 1) if o != 0]

    @pl.when(r == 0)
    def _():
        i0 = lax.broadcasted_iota(jnp.int32, (tr, tr), 0)
        i1 = lax.broadcasted_iota(jnp.int32, (tr, tr), 1)
        for m, o in enumerate(offsets):
            shift_ref[m * tr:(m + 1) * tr, :] = jnp.where(i1 == i0 + o, 1.0, 0.0).astype(BF16)

    def conv_silu(cs):
        x = pm_ref[:, cs]
        tap = lambda o: cw_ref[half + o:half + o + 1, cs]
        shifted = jnp.dot(shift_ref[...], x, preferred_element_type=F32)
        acc = tap(0) * x.astype(F32)
        for m, o in enumerate(offsets):
            acc = acc + tap(o) * shifted[m * tr:(m + 1) * tr, :]
        prev = pp_ref[:, cs].astype(F32)[8:16, :] * keep_prev
        nxt = pn_ref[:, cs].astype(F32)[0:8, :] * keep_next
        sub = lax.broadcasted_iota(jnp.int32, prev.shape, 0)
        top = jnp.zeros_like(prev)
        bot = jnp.zeros_like(prev)
        for o in range(1, half + 1):
            top = top + tap(-o) * jnp.where(sub < o, pltpu.roll(prev, o, 0), 0.0)
            bot = bot + tap(o) * jnp.where(sub >= 8 - o, pltpu.roll(nxt, 8 - o, 0), 0.0)
        return _silu(jnp.concatenate([acc[0:8] + top, acc[8:tr - 8], acc[tr - 8:tr] + bot], axis=0))

    for j, scale in ((PCOL_Q, HEAD_DIM ** -0.5), (PCOL_K, 1.0)):
        y = conv_silu(slice(j * D_GDN, (j + 1) * D_GDN))
        for h in range(N_HEADS):
            yh = y[:, h * HEAD_DIM:(h + 1) * HEAD_DIM]
            inv = lax.rsqrt(jnp.sum(yh * yh, axis=-1, keepdims=True) + EPS) * scale
            qkv_ref[:, j * D_GDN + h * HEAD_DIM:j * D_GDN + (h + 1) * HEAD_DIM] = (yh * inv).astype(BF16)
    vs = slice(PCOL_V * D_GDN, (PCOL_V + 1) * D_GDN)
    qkv_ref[:, vs] = conv_silu(vs).astype(BF16)

    s = small_ref[...]
    lane = lax.broadcasted_iota(jnp.int32, s.shape, 1)
    beta = _sigmoid(s)
    z = s + dtb_ref[...]
    softplus = jnp.maximum(z, 0.0) + jnp.log(1.0 + jnp.exp(-jnp.abs(z)))
    g = nega_ref[...] * softplus
    gb_ref[...] = jnp.where(lane < 2 * N_HEADS, beta, jnp.where(lane < 4 * N_HEADS, g, 0.0))


def _prep(p, small, conv_w8, nega, dtb, first_tiles, last_tiles):
    ta = p.shape[0]
    n_tiles = ta // PREP_TILE
    sub = PREP_TILE // 16
    n_sub = ta // 16
    width = 3 * D_GDN
    kern = functools.partial(_prep_kernel, first_tiles=first_tiles, last_tiles=last_tiles)
    return pl.pallas_call(
        kern,
        grid=(n_tiles,),
        in_specs=[pl.BlockSpec((PREP_TILE, width), lambda r: (r, 0)),
                  pl.BlockSpec((16, width), lambda r: (jnp.maximum(r * sub - 1, 0), 0)),
                  pl.BlockSpec((16, width), lambda r: (jnp.minimum((r + 1) * sub, n_sub - 1), 0)),
                  pl.BlockSpec((8, width), lambda r: (0, 0)),
                  pl.BlockSpec((PREP_TILE, LANES), lambda r: (r, 0)),
                  pl.BlockSpec((1, LANES), lambda r: (0, 0)),
                  pl.BlockSpec((1, LANES), lambda r: (0, 0))],
        out_specs=[pl.BlockSpec((PREP_TILE, width), lambda r: (r, 0)),
                   pl.BlockSpec((PREP_TILE, LANES), lambda r: (r, 0))],
        out_shape=[jax.ShapeDtypeStruct((ta, width), BF16),
                   jax.ShapeDtypeStruct((ta, LANES), F32)],
        scratch_shapes=[pltpu.VMEM(((CONV_K - 1) * PREP_TILE, PREP_TILE), BF16)],
        compiler_params=_params("arbitrary"),
        name="prep",
    )(p, p, p, conv_w8, small, nega, dtb)


def _gdn_decays(d, gb):
    c = GDN_CHUNK
    row = lax.broadcasted_iota(jnp.int32, (c, c), 0)
    col = lax.broadcasted_iota(jnp.int32, (c, c), 1)
    incl = row >= col if d == 0 else row <= col
    lane = lax.broadcasted_iota(jnp.int32, gb.shape, 1)
    g_only = jnp.where(lane >= 2 * N_HEADS, jnp.where(lane < 4 * N_HEADS, gb, 0.0), 0.0)
    tri = jnp.where(incl, 1.0, 0.0).astype(BF16)
    g_hi = g_only.astype(BF16)
    g_r1 = g_only - g_hi.astype(F32)
    g_mid = g_r1.astype(BF16)
    g_lo = (g_r1 - g_mid.astype(F32)).astype(BF16)
    dot = lambda a, b: jnp.dot(a, b, preferred_element_type=F32)
    gcum = dot(tri, g_hi) + dot(tri, g_mid) + dot(tri, g_lo)
    g_end = gcum[c - 1:c, :] if d == 0 else gcum[0:1, :]
    return gcum, gcum.T, jnp.exp(gcum), jnp.exp(g_end - gcum), jnp.exp(g_end)


def _gdn_kernel(qf, kf, vf, gbf, qb, kb, vb, gbb, of_ref, ob_ref, s_ref):
    @pl.when(pl.program_id(1) == 0)
    def _():
        s_ref[...] = jnp.zeros_like(s_ref)

    c = GDN_CHUNK
    n_sub = qf.shape[0] // c
    row = lax.broadcasted_iota(jnp.int32, (c, c), 0)
    col = lax.broadcasted_iota(jnp.int32, (c, c), 1)
    eye = jnp.where(row == col, 1.0, 0.0)
    masks = ((row >= col, row > col), (row <= col, row < col))
    refs = ((qf, kf, vf, gbf, of_ref), (qb, kb, vb, gbb, ob_ref))

    def rows(d, sub):
        k = sub if d == 0 else n_sub - 1 - sub
        return slice(k * c, (k + 1) * c)

    scans = [(d, sub) for sub in range(n_sub) for d in range(2)]
    gbv = {ds: refs[ds[0]][3][rows(*ds), :] for ds in scans}
    dec = {ds: _gdn_decays(ds[0], gbv[ds]) for ds in scans}
    chains = [(d, sub, h) for d, sub in scans for h in range(N_HEADS)]

    st = []
    for d, sub, h in chains:
        q_ref, k_ref, v_ref, _, _ = refs[d]
        rs, hs = rows(d, sub), slice(h * HEAD_DIM, (h + 1) * HEAD_DIM)
        q, k, v = q_ref[rs, hs], k_ref[rs, hs], v_ref[rs, hs]
        both = _mm_nt(jnp.concatenate([q, k], axis=0), k)
        st.append(dict(q=q.astype(F32), k=k.astype(F32), v=v.astype(F32), qk=both[:c], kk=both[c:]))

    for (d, sub, h), e in zip(chains, st):
        cb = d * N_HEADS + h
        cg = 2 * N_HEADS + cb
        gcum, gcum_t, exp_g, exp_rest, exp_end = dec[d, sub]
        incl, strict = masks[d]
        e["beta"] = gbv[d, sub][:, cb:cb + 1]
        diff = gcum[:, cg:cg + 1] - gcum_t[cg:cg + 1, :]
        decay = jnp.where(incl, jnp.exp(jnp.minimum(diff, 0.0)), 0.0)
        e["y"] = jnp.where(strict, -(e["beta"] * e["kk"] * decay), 0.0)
        e["qk"] = e["qk"] * decay
        e["eg"] = exp_g[:, cg:cg + 1]
        e["er"] = exp_rest[:, cg:cg + 1]
        e["ee"] = exp_end[:, cg:cg + 1]

    levels = range(GDN_CHUNK.bit_length() - 1)
    sels = []
    for d in range(2):
        inner, outer = (col, row) if d == 0 else (row, col)
        sels.append([((outer >> lvl) & 1 == 1) & ((inner >> lvl) == (outer >> lvl) - 1) for lvl in levels])
    for (d, sub, h), e in zip(chains, st):
        e["t"] = eye + jnp.where(sels[d][0], e["y"], 0.0)
    def active_rows(d, b):
        return [(2 * p + 1 - d) * b for p in range(c // (2 * b))]

    def pick(x, starts, b):
        return jnp.concatenate([x[s:s + b] for s in starts], axis=0)

    def place(pieces, starts, b, base):
        out = []
        for blk in range(c // b):
            cur = None if base is None else base[blk * b:(blk + 1) * b]
            if blk * b in starts:
                piece = pieces[starts.index(blk * b) * b:(starts.index(blk * b) + 1) * b]
                cur = piece if cur is None else cur + piece
            out.append(jnp.zeros((b, c), F32) if cur is None else cur)
        return jnp.concatenate(out, axis=0)

    for lvl in levels[1:]:
        b = 1 << lvl
        if b % 8 == 0:
            for (d, sub, h), e in zip(chains, st):
                act = active_rows(d, b)
                yd = _mm(pick(jnp.where(sels[d][lvl], e["y"], 0.0), act, b), e["t"])
                e["yd"] = place(yd, act, b, None)
            for (d, sub, h), e in zip(chains, st):
                act = active_rows(d, b)
                e["t"] = place(_mm(pick(e["t"], act, b), e["yd"]), act, b, e["t"])
        else:
            for (d, sub, h), e in zip(chains, st):
                e["yd"] = _mm(jnp.where(sels[d][lvl], e["y"], 0.0), e["t"])
            for e in st:
                e["t"] = e["t"] + _mm(e["t"], e["yd"])

    for e in st:
        kb_ = e["k"] * e["beta"]
        uw = _mm(e["t"], jnp.concatenate([e["v"] * e["beta"], kb_ * e["eg"]], axis=-1))
        e["u"], e["w"] = uw[:, :HEAD_DIM], uw[:, HEAD_DIM:]

    state = {(d, h): s_ref[d, h] for d in range(2) for h in range(N_HEADS)}
    for step in range(n_sub):
        now = [(key, e) for key, e in zip(chains, st) if key[1] == step]
        for (d, sub, h), e in now:
            ws = _mm(jnp.concatenate([e["w"], e["q"] * e["eg"]], axis=0), state[d, h])
            e["v_new"] = e["u"] - ws[:c]
            e["o"] = ws[c:]
        for (d, sub, h), e in now:
            k_dec_t = (e["k"] * e["er"]).T
            out = _mm(jnp.concatenate([e["qk"], k_dec_t], axis=0), e["v_new"])
            refs[d][4][rows(d, sub), h * HEAD_DIM:(h + 1) * HEAD_DIM] = (e["o"] + out[:c]).astype(BF16)
            state[d, h] = state[d, h] * e["ee"] + out[c:]
    for (d, h), s in state.items():
        s_ref[d, h] = s


def _gdn(qkv, gb, batch, ctx_blocks, lat_blocks):
    ta = qkv.shape[0]
    rows = GDN_CHUNK * GDN_STEP_CHUNKS
    n_steps = ctx_blocks + lat_blocks
    lat0 = batch * ctx_blocks

    def fwd_blk(b, s):
        return jnp.where(s < ctx_blocks, b * ctx_blocks + s, lat0 + b * lat_blocks + (s - ctx_blocks))

    def bwd_blk(b, s):
        return jnp.where(s < ctx_blocks, b * ctx_blocks + (ctx_blocks - 1 - s),
                         lat0 + b * lat_blocks + (lat_blocks - 1 - (s - ctx_blocks)))

    def specs(blk):
        col = lambda j: pl.BlockSpec((rows, D_MODEL), lambda b, s: (blk(b, s), j))
        return [col(0), col(1), col(2), pl.BlockSpec((rows, LANES), lambda b, s: (blk(b, s), 0))]

    out = lambda blk: pl.BlockSpec((rows, D_MODEL), lambda b, s: (blk(b, s), 0))
    return pl.pallas_call(
        _gdn_kernel,
        grid=(batch, n_steps),
        in_specs=specs(fwd_blk) + specs(bwd_blk),
        out_specs=[out(fwd_blk), out(bwd_blk)],
        out_shape=[jax.ShapeDtypeStruct((ta, D_MODEL), BF16)] * 2,
        scratch_shapes=[pltpu.VMEM((2, N_HEADS, HEAD_DIM, HEAD_DIM), F32)],
        compiler_params=_params("arbitrary", "arbitrary"),
        name="gdn",
    )(qkv, qkv, qkv, gb, qkv, qkv, qkv, gb)


def _merge_kernel(of_ref, ob_ref, z_ref, su_ref, sv_ref, ga_ref, gb_ref, x_ref, ws_ref, bs_ref,
                  wa_ref, wb_ref, wo_ref, gng_ref, gate_ref, nfg_ref, shift_ref, scale_ref, wr_ref, br_ref,
                  xmid_ref, h2_ref, rgate_ref, ygdn_ref, ysgu_ref):
    tm = x_ref.shape[0]
    parts = [slice(k * tm // MERGE_PARTS, (k + 1) * tm // MERGE_PARTS) for k in range(MERGE_PARTS)]
    dot = lambda a, b: jnp.dot(a, b, preferred_element_type=F32)

    for ch in range(tm // SGU_CHUNK):
        rs = slice(ch * SGU_CHUNK, (ch + 1) * SGU_CHUNK)
        for g in range(SGU_GROUPS):
            cs = slice(g * SGU_GROUP_DIM, (g + 1) * SGU_GROUP_DIM)
            mixed = dot(ws_ref[g], sv_ref[rs, cs]) + bs_ref[:, cs]
            ysgu_ref[rs, cs] = (su_ref[rs, cs].astype(F32) * mixed).astype(BF16)

    for rs in parts:
        o = of_ref[rs, :].astype(F32) + ob_ref[rs, :].astype(F32)
        for h in range(N_HEADS):
            hs = slice(h * HEAD_DIM, (h + 1) * HEAD_DIM)
            oh = o[:, hs]
            inv = lax.rsqrt(jnp.mean(oh * oh, axis=-1, keepdims=True) + EPS)
            ygdn_ref[rs, hs] = (oh * inv * gng_ref[...] * z_ref[rs, hs].astype(F32)).astype(BF16)
    ya = [dot(ygdn_ref[rs, :], wa_ref[...]) for rs in parts]
    yb = [dot(ysgu_ref[rs, :], wb_ref[...]) for rs in parts]
    merged = [(ga_ref[rs, :].astype(F32) * a + gb_ref[rs, :].astype(F32) * b_).astype(BF16)
              for rs, a, b_ in zip(parts, ya, yb)]
    mix = [dot(m, wo_ref[...]) for m in merged]
    h2s = []
    for rs, mx in zip(parts, mix):
        xm = x_ref[rs, :] + gate_ref[0] * mx
        xmid_ref[rs, :] = xm
        hn = xm * lax.rsqrt(jnp.mean(xm * xm, axis=-1, keepdims=True) + EPS) * nfg_ref[...]
        h2 = hn * (1.0 + scale_ref[0]) + shift_ref[0]
        h2_ref[rs, :] = h2.astype(BF16)
        h2s.append(h2)

    wr = wr_ref[...]
    w_hi = wr.astype(BF16)
    w_lo = (wr - w_hi.astype(F32)).astype(BF16)
    logits = []
    for h2 in h2s:
        h_hi = h2.astype(BF16)
        h_lo = (h2 - h_hi.astype(F32)).astype(BF16)
        logits.append(dot(h_hi, w_hi) + dot(h_lo, w_hi) + dot(h_hi, w_lo) + br_ref[...])

    for rs, lg in zip(parts, logits):
        lane = lax.broadcasted_iota(jnp.int32, lg.shape, 1).astype(F32)
        far = float(LANES)
        gl = jnp.where(lane < N_EXPERT_GROUPS, lg, NEG_BIG)
        gmax = jnp.max(gl, axis=-1, keepdims=True)
        p_g = 1.0 / jnp.sum(jnp.exp(gl - gmax), axis=-1, keepdims=True)
        grp = jnp.min(jnp.where(gl == gmax, lane, far), axis=-1, keepdims=True)
        lo = N_EXPERT_GROUPS + EXPERTS_PER_GROUP * grp
        in_grp = jnp.where(lane >= lo, jnp.where(lane < lo + EXPERTS_PER_GROUP, 1.0, 0.0), 0.0)
        el = jnp.where(in_grp > 0.0, lg, NEG_BIG)
        m1 = jnp.max(el, axis=-1, keepdims=True)
        i1 = jnp.min(jnp.where(el == m1, lane, far), axis=-1, keepdims=True)
        el2 = jnp.where(lane == i1, NEG_BIG, el)
        m2 = jnp.max(el2, axis=-1, keepdims=True)
        i2 = jnp.min(jnp.where(el2 == m2, lane, far), axis=-1, keepdims=True)
        t = jnp.exp(m2 - m1)
        w1 = p_g / (1.0 + t)
        w2 = w1 * t
        rgate_ref[rs, :] = jnp.where(lane == 0.0, grp,
                                     jnp.where(lane == i1, w1, 0.0) + jnp.where(lane == i2, w2, 0.0))


def _merge(o_f, o_b, p, x2, w_s, b_s, w_a, w_b, w_o, gng, gate3, nfg, shift3, scale3, w_r, b_r,
           lat_tile0, tiles_per_batch):
    t_lat = x2.shape[0]
    n_tiles = t_lat // ROW_TILE
    lat = lambda c: pl.BlockSpec((ROW_TILE, D_MODEL), lambda i: (i + lat_tile0, c))
    own = lambda: pl.BlockSpec((ROW_TILE, D_MODEL), lambda i: (i, 0))
    mat = lambda: pl.BlockSpec((D_MODEL, D_MODEL), lambda i: (0, 0))
    vec = lambda: pl.BlockSpec((1, D_MODEL), lambda i: (0, 0))
    per_b = lambda: pl.BlockSpec((1, 1, D_MODEL), lambda i: (i // tiles_per_batch, 0, 0))
    return pl.pallas_call(
        _merge_kernel,
        grid=(n_tiles,),
        in_specs=[lat(0), lat(0), lat(PCOL_Z), lat(PCOL_U), lat(PCOL_SV), lat(PCOL_GA), lat(PCOL_GB), own(),
                  pl.BlockSpec((SGU_GROUPS, SGU_CHUNK, SGU_CHUNK), lambda i: (0, 0, 0)),
                  pl.BlockSpec((SGU_CHUNK, D_MODEL), lambda i: (0, 0)),
                  mat(), mat(), mat(), pl.BlockSpec((1, HEAD_DIM), lambda i: (0, 0)),
                  per_b(), vec(), per_b(), per_b(),
                  pl.BlockSpec((D_MODEL, LANES), lambda i: (0, 0)),
                  pl.BlockSpec((1, LANES), lambda i: (0, 0))],
        out_specs=[own(), own(), pl.BlockSpec((ROW_TILE, LANES), lambda i: (i, 0))],
        out_shape=[jax.ShapeDtypeStruct((t_lat, D_MODEL), F32),
                   jax.ShapeDtypeStruct((t_lat, D_MODEL), BF16),
                   jax.ShapeDtypeStruct((t_lat, LANES), F32)],
        scratch_shapes=[pltpu.VMEM((ROW_TILE, D_MODEL), BF16), pltpu.VMEM((ROW_TILE, D_MODEL), BF16)],
        compiler_params=_params("arbitrary"),
        name="merge",
    )(o_f, o_b, p, p, p, p, p, x2, w_s, b_s, w_a, w_b, w_o, gng, gate3, nfg, shift3, scale3, w_r, b_r)


def _split3(x):
    hi = x.astype(BF16)
    r1 = x - hi.astype(F32)
    mid = r1.astype(BF16)
    return hi, mid, (r1 - mid.astype(F32)).astype(BF16)


def _moe_kernel(h_ref, rg_ref, w1_ref, w3_ref, w2_ref, xmid_ref, gate_ref, fng_ref, o_ref,
                before_ref, destc_ref, xs_ref, ys_ref, gs_ref, start_ref):
    i, g, hf = pl.program_id(0), pl.program_id(1), pl.program_id(2)
    tm = h_ref.shape[0]
    dot = lambda a, b: jnp.dot(a, b, preferred_element_type=F32)

    @pl.when((i == 0) & (g == 0) & (hf == 0))
    def _():
        r = lax.broadcasted_iota(jnp.int32, (tm, tm), 0)
        c = lax.broadcasted_iota(jnp.int32, (tm, tm), 1)
        before_ref[...] = jnp.where(r < c, 1.0, 0.0).astype(BF16)

    @pl.when((g == 0) & (hf == 0))
    def _():
        rg = rg_ref[...]
        sub = lax.broadcasted_iota(jnp.int32, (8, tm), 0).astype(F32)
        m_rows = jnp.where(rg.T[0:1, :] == sub, 1.0, 0.0)
        total_r = jnp.sum(m_rows, axis=1, keepdims=True)
        start_r = jnp.zeros_like(total_r)
        for gg in range(N_EXPERT_GROUPS - 1):
            start_r = start_r + jnp.where(sub[:, 0:1] > gg, total_r[gg:gg + 1, :], 0.0)
        rank_r = dot(m_rows.astype(BF16), before_ref[...])
        dest_r = jnp.sum(jnp.where(m_rows > 0.0, rank_r + start_r, 0.0), axis=0, keepdims=True)
        destc_ref[...] = jnp.broadcast_to(dest_r, (LANES, tm)).T
        acc = jnp.int32(0)
        for gg in range(N_EXPERT_GROUPS):
            start_ref[gg] = acc
            acc = acc + total_r[gg, 0].astype(jnp.int32)
        start_ref[N_EXPERT_GROUPS] = acc
        slot = lax.broadcasted_iota(jnp.int32, (tm, tm), 0).astype(F32)
        perm = jnp.where(dest_r == slot, 1.0, 0.0).astype(BF16)
        g_hi, g_mid, _ = _split3(rg)
        moved = dot(perm, jnp.concatenate([h_ref[...], g_hi, g_mid], axis=1))
        xs_ref[...] = moved[:, :D_MODEL].astype(BF16)
        gs_ref[...] = moved[:, D_MODEL:D_MODEL + LANES] + moved[:, D_MODEL + LANES:]
        ys_ref[...] = jnp.zeros_like(ys_ref)

    bs = MOE_BLOCK
    first = start_ref[g] // bs
    n_unit = (start_ref[g + 1] + (bs - 1)) // bs - first

    def experts(row0, size):
        rows = pl.ds(pl.multiple_of(row0, bs), size)
        x = xs_ref[rows, :]
        gs = gs_ref[rows, :]
        lane = lax.broadcasted_iota(jnp.int32, gs.shape, 1)
        y = ys_ref[rows, :]
        for e in range(MOE_EXPERTS_PER_STEP):
            col = N_EXPERT_GROUPS + g * EXPERTS_PER_GROUP + hf * MOE_EXPERTS_PER_STEP + e
            gate = jnp.sum(jnp.where(lane == col, gs, 0.0), axis=-1, keepdims=True)
            hid = _silu(dot(x, w1_ref[e])) * dot(x, w3_ref[e]) * gate
            y = y + dot(hid.astype(BF16), w2_ref[e])
        ys_ref[rows, :] = y

    big = MOE_BLOCK_MULTIPLES[0]

    def big_block(k, carry):
        experts((first + k * big) * bs, big * bs)
        return carry

    lax.fori_loop(0, n_unit // big, big_block, 0)
    done = first + (n_unit // big) * big
    for m in MOE_BLOCK_MULTIPLES[1:]:
        take = (n_unit & m) != 0

        @pl.when(take)
        def _(done=done, m=m):
            experts(done * bs, m * bs)

        done = done + jnp.where(take, m, 0)

    @pl.when((g == pl.num_programs(1) - 1) & (hf == pl.num_programs(2) - 1))
    def _():
        slot = lax.broadcasted_iota(jnp.int32, (tm, tm), 1).astype(F32)
        perm_t = jnp.where(destc_ref[:, 0:1] == slot, 1.0, 0.0).astype(BF16)
        xo = xmid_ref[...] + gate_ref[0] * dot(perm_t, ys_ref[...].astype(BF16))
        o_ref[...] = xo * lax.rsqrt(jnp.mean(xo * xo, axis=-1, keepdims=True) + EPS) * fng_ref[...]


def _moe(h2, rgate, w1, w3, w2, xmid, gate3, fng, tiles_per_batch):
    t_lat = h2.shape[0]
    n_tiles = t_lat // MOE_TILE
    eps = MOE_EXPERTS_PER_STEP
    halves = EXPERTS_PER_GROUP // eps
    row = lambda w: pl.BlockSpec((MOE_TILE, w), lambda i, g, hf: (i, 0))
    wspec = lambda a, b: pl.BlockSpec((eps, a, b), lambda i, g, hf: (g * halves + hf, 0, 0))
    return pl.pallas_call(
        _moe_kernel,
        grid=(n_tiles, N_EXPERT_GROUPS, halves),
        in_specs=[row(D_MODEL), row(LANES),
                  wspec(D_MODEL, D_EXPERT), wspec(D_MODEL, D_EXPERT), wspec(D_EXPERT, D_MODEL),
                  row(D_MODEL),
                  pl.BlockSpec((1, 1, D_MODEL), lambda i, g, hf: (i // tiles_per_batch, 0, 0)),
                  pl.BlockSpec((1, D_MODEL), lambda i, g, hf: (0, 0))],
        out_specs=row(D_MODEL),
        out_shape=jax.ShapeDtypeStruct((t_lat, D_MODEL), F32),
        scratch_shapes=[pltpu.VMEM((MOE_TILE, MOE_TILE), BF16),
                        pltpu.VMEM((MOE_TILE, LANES), F32),
                        pltpu.VMEM((MOE_TILE, D_MODEL), BF16),
                        pltpu.VMEM((MOE_TILE, D_MODEL), F32),
                        pltpu.VMEM((MOE_TILE, LANES), F32),
                        pltpu.SMEM((N_EXPERT_GROUPS + 1,), jnp.int32)],
        compiler_params=pltpu.CompilerParams(dimension_semantics=("arbitrary",) * 3,
                                             vmem_limit_bytes=MOE_VMEM_LIMIT),
        name="moe",
    )(h2, rgate, w1, w3, w2, xmid, gate3, fng)


def kernel(x, c, ctx, c_ctx, ada_w, ada_b, norm_mix_g, w_in, conv_w, a_log, dt_bias, gdn_norm_g, sgu_ln_g, sgu_ln_b, sgu_w, sgu_b, w_branch_a, w_branch_b, w_out, norm_ffn_g, router_group_w, router_group_b, router_expert_w, router_expert_b, expert_w1, expert_w3, expert_w2, final_norm_g):
    batch, seq, d = x.shape
    ctx_len = ctx.shape[1]
    assert d == D_MODEL and ada_w.shape[0] == 1, "single-layer block with D_MODEL channels"
    assert batch * ctx_len == ROW_TILE, "context rows of all samples form one row tile"
    assert seq % MOE_TILE == 0 and ctx_len % PREP_TILE == 0 and batch + 1 <= 8
    t_lat = batch * seq
    row = lambda v: v.reshape(1, -1).astype(F32)

    cond = jnp.zeros((8, d), F32).at[:batch].set(c).at[batch].set(c_ctx)
    mod = _adaln(cond, ada_w[0], ada_b[0]).reshape(8, N_MOD, d)
    mod_row = lambda k: mod[:batch + 1, k].reshape(batch + 1, 1, d)

    w_l = w_in[0]
    w_main = jnp.concatenate([w_l[:, :COL_BETA], w_l[:, COL_Z:]], axis=1).astype(BF16)
    w_small = jnp.zeros((d, LANES), BF16).at[:, :4 * N_HEADS].set(w_l[:, COL_BETA:COL_Z].astype(BF16))
    x2 = x.reshape(t_lat, d)
    tiles_per_batch = seq // ROW_TILE
    p, small = _inproj(x2, ctx.reshape(batch * ctx_len, d), row(norm_mix_g), mod_row(0), mod_row(1),
                       w_main, w_small, row(sgu_ln_g), row(sgu_ln_b), tiles_per_batch)

    ctx_t, lat_t = ctx_len // PREP_TILE, seq // PREP_TILE
    starts = [b * ctx_t for b in range(batch)] + [batch * ctx_t + b * lat_t for b in range(batch)]
    ends = [(b + 1) * ctx_t - 1 for b in range(batch)] + [batch * ctx_t + (b + 1) * lat_t - 1 for b in range(batch)]
    conv_w8 = jnp.zeros((8, 3 * D_GDN), F32).at[:CONV_K].set(conv_w[0])
    pad_lanes = lambda v: jnp.zeros((1, LANES), F32).at[0, 2 * N_HEADS:4 * N_HEADS].set(v.reshape(-1))
    qkv, gb = _prep(p, small, conv_w8, pad_lanes(-jnp.exp(a_log[0])), pad_lanes(dt_bias[0]),
                    tuple(starts), tuple(ends))

    step_rows = GDN_CHUNK * GDN_STEP_CHUNKS
    assert ctx_len % step_rows == 0 and seq % step_rows == 0
    o_f, o_b = _gdn(qkv, gb, batch, ctx_len // step_rows, seq // step_rows)

    b_full = jnp.repeat(sgu_b[0].T, SGU_GROUP_DIM, axis=1).astype(F32)

    w_r = jnp.zeros((d, LANES), F32).at[:, :N_EXPERT_GROUPS].set(router_group_w[0]) \
        .at[:, N_EXPERT_GROUPS:N_EXPERT_GROUPS + N_EXPERTS].set(router_expert_w[0])
    b_r = jnp.zeros((1, LANES), F32).at[0, :N_EXPERT_GROUPS].set(router_group_b[0]) \
        .at[0, N_EXPERT_GROUPS:N_EXPERT_GROUPS + N_EXPERTS].set(router_expert_b[0])
    gng = gdn_norm_g[0].reshape(1, HEAD_DIM).astype(F32)
    xmid, h2, rgate = _merge(o_f, o_b, p, x2, sgu_w[0].astype(BF16), b_full,
                             w_branch_a[0].astype(BF16), w_branch_b[0].astype(BF16),
                             w_out[0].astype(BF16), gng, mod_row(2)[:batch], row(norm_ffn_g),
                             mod_row(3)[:batch], mod_row(4)[:batch], w_r, b_r, 1, tiles_per_batch)

    out = _moe(h2, rgate, expert_w1[0].astype(BF16), expert_w3[0].astype(BF16), expert_w2[0].astype(BF16),
               xmid, mod_row(5)[:batch], row(final_norm_g), seq // MOE_TILE)
    return out.reshape(batch, seq, d)
```

```python
import functools
import math

import jax
import jax.numpy as jnp
from jax import lax
from jax.experimental import pallas as pl
from jax.experimental.pallas import tpu as pltpu

F32 = jnp.float32
BF16 = jnp.bfloat16

D_MODEL = 1024
N_HEADS = 8
HEAD_DIM = 128
D_GDN = N_HEADS * HEAD_DIM
CONV_K = 5
GDN_CHUNK = 64
GDN_STEP_CHUNKS = 4
SGU_GROUPS = 8
SGU_GROUP_DIM = 128
D_SGU = SGU_GROUPS * SGU_GROUP_DIM
SGU_CHUNK = 128
N_EXPERT_GROUPS = 4
EXPERTS_PER_GROUP = 8
N_EXPERTS = N_EXPERT_GROUPS * EXPERTS_PER_GROUP
D_EXPERT = 256
N_MOD = 6
EPS = 1e-6
COL_BETA = 3 * D_GDN
COL_Z = COL_BETA + 4 * N_HEADS

LANES = 128
ROW_TILE = 512
INPROJ_SEGS = 4
MERGE_PARTS = 2
PREP_TILE = 256
MOE_TILE = 1024
MOE_EXPERTS_PER_STEP = 4
MOE_BLOCK = 128
MOE_BLOCK_MULTIPLES = (4, 2, 1)
VMEM_LIMIT = 48 * 1024 * 1024
MOE_VMEM_LIMIT = 56 * 1024 * 1024
NEG_BIG = -1e30

PCOL_Q, PCOL_K, PCOL_V, PCOL_GB, PCOL_Z, PCOL_U, PCOL_SV, PCOL_GA = range(8)


def _mm(a, b):
    return jnp.dot(a.astype(BF16), b.astype(BF16), preferred_element_type=F32)


def _mm_nt(a, b):
    return lax.dot_general(a.astype(BF16), b.astype(BF16), (((1,), (1,)), ((), ())),
                           preferred_element_type=F32)


def _sigmoid(x):
    return 0.5 + 0.5 * jnp.tanh(0.5 * x)


def _silu(x):
    return x * _sigmoid(x)


def _gelu_tanh(x):
    return 0.5 * x * (1.0 + jnp.tanh(math.sqrt(2.0 / math.pi) * (x + 0.044715 * (x * x * x))))


def _params(*sem):
    return pltpu.CompilerParams(dimension_semantics=sem, vmem_limit_bytes=VMEM_LIMIT)


def _adaln_kernel(c_ref, w_ref, b_ref, o_ref):
    o_ref[...] = _mm(_silu(c_ref[...]), w_ref[...]) + b_ref[...]


def _adaln(cond, w, b):
    n = w.shape[1]
    tn = 1536
    return pl.pallas_call(
        _adaln_kernel,
        grid=(n // tn,),
        in_specs=[pl.BlockSpec((8, D_MODEL), lambda j: (0, 0)),
                  pl.BlockSpec((D_MODEL, tn), lambda j: (0, j)),
                  pl.BlockSpec((1, tn), lambda j: (0, j))],
        out_specs=pl.BlockSpec((8, tn), lambda j: (0, j)),
        out_shape=jax.ShapeDtypeStruct((8, n), F32),
        compiler_params=_params("arbitrary"),
        name="adaln",
    )(cond, w, b.reshape(1, n))


def _inproj_kernel(x_ref, ctx_ref, g_ref, shift_ref, scale_ref, wh_ref, wt_ref, ws_ref, lng_ref, lnb_ref,
                   p_ref, small_ref, h_ref):
    i = pl.program_id(0)
    j = pl.program_id(1)

    def norm_mod(xv):
        y = xv * lax.rsqrt(jnp.mean(xv * xv, axis=-1, keepdims=True) + EPS) * g_ref[...]
        return y * (1.0 + scale_ref[0]) + shift_ref[0]

    @pl.when(j == 0)
    def _():
        @pl.when(i == 0)
        def _():
            h_ref[...] = norm_mod(ctx_ref[...]).astype(BF16)

        @pl.when(i > 0)
        def _():
            h_ref[...] = norm_mod(x_ref[...]).astype(BF16)

        small_ref[...] = jnp.dot(h_ref[...], ws_ref[...], preferred_element_type=F32)

    def raw(a):
        return a

    def gelu_ln(a):
        a = _gelu_tanh(a)
        mu = jnp.mean(a, axis=-1, keepdims=True)
        ac = a - mu
        var = jnp.mean(ac * ac, axis=-1, keepdims=True)
        return ac * lax.rsqrt(var + EPS) * lng_ref[...] + lnb_ref[...]

    def project(w_ref, w_seg, p_seg, epilogue):
        a = jnp.dot(h_ref[...], w_ref[:, w_seg * D_MODEL:(w_seg + 1) * D_MODEL], preferred_element_type=F32)
        p_ref[:, p_seg * D_MODEL:(p_seg + 1) * D_MODEL] = epilogue(a).astype(BF16)

    @pl.when(j == 0)
    def _():
        for seg in range(3):
            project(wh_ref, seg, seg, raw)
        project(wt_ref, 0, 3, _sigmoid)

    @pl.when(j == 1)
    def _():
        for seg, epilogue in enumerate((_silu, _gelu_tanh, gelu_ln, _sigmoid)):
            project(wt_ref, seg, seg, epilogue)


def _inproj(x2, ctx2, norm_g, shift3, scale3, w_head, w_tail, w_small, ln_g, ln_b, tiles_per_batch):
    t_lat = x2.shape[0]
    n_lat = t_lat // ROW_TILE
    n_tiles = n_lat + 1
    tn = INPROJ_SEGS * D_MODEL
    assert w_head.shape[1] == tn - D_MODEL and w_tail.shape[1] == tn + D_MODEL
    n_batch = shift3.shape[0] - 1
    sel = lambda i: jnp.where(i == 0, n_batch, (jnp.maximum(i, 1) - 1) // tiles_per_batch)
    vec = lambda: pl.BlockSpec((1, D_MODEL), lambda i, j: (0, 0))
    once = pl.Buffered(1)
    return pl.pallas_call(
        _inproj_kernel,
        grid=(n_tiles, 2),
        in_specs=[pl.BlockSpec((ROW_TILE, D_MODEL), lambda i, j: (jnp.maximum(i, 1) - 1, 0)),
                  pl.BlockSpec((ROW_TILE, D_MODEL), lambda i, j: (0, 0), pipeline_mode=once),
                  vec(),
                  pl.BlockSpec((1, 1, D_MODEL), lambda i, j: (sel(i), 0, 0)),
                  pl.BlockSpec((1, 1, D_MODEL), lambda i, j: (sel(i), 0, 0)),
                  pl.BlockSpec((D_MODEL, tn - D_MODEL), lambda i, j: (0, 0), pipeline_mode=once),
                  pl.BlockSpec((D_MODEL, tn), lambda i, j: (0, 1 - j)),
                  pl.BlockSpec((D_MODEL, LANES), lambda i, j: (0, 0)),
                  vec(), vec()],
        out_specs=[pl.BlockSpec((ROW_TILE, tn), lambda i, j: (i, j)),
                   pl.BlockSpec((ROW_TILE, LANES), lambda i, j: (i, 0))],
        out_shape=[jax.ShapeDtypeStruct((n_tiles * ROW_TILE, 2 * tn), BF16),
                   jax.ShapeDtypeStruct((n_tiles * ROW_TILE, LANES), F32)],
        scratch_shapes=[pltpu.VMEM((ROW_TILE, D_MODEL), BF16)],
        compiler_params=_params("arbitrary", "arbitrary"),
        name="inproj",
    )(x2, ctx2, norm_g, shift3, scale3, w_head, w_tail, w_small, ln_g, ln_b)


def _prep_kernel(pm_ref, pp_ref, pn_ref, cw_ref, small_ref, nega_ref, dtb_ref, qkv_ref, gb_ref, shift_ref,
                 *, first_tiles, last_tiles):
    r = pl.program_id(0)
    tr = pm_ref.shape[0]
    is_first = functools.reduce(jnp.logical_or, [r == t for t in first_tiles])
    is_last = functools.reduce(jnp.logical_or, [r == t for t in last_tiles])
    keep_prev = jnp.where(is_first, 0.0, 1.0)
    keep_next = jnp.where(is_last, 0.0, 1.0)

    half = CONV_K // 2
    offsets = [o for o in range(-half, half + 1) if o != 0]

    @pl.when(r == 0)
    def _():
        i0 = lax.broadcasted_iota(jnp.int32, (tr, tr), 0)
        i1 = lax.broadcasted_iota(jnp.int32, (tr, tr), 1)
        for m, o in enumerate(offsets):
            shift_ref[m * tr:(m + 1) * tr, :] = jnp.where(i1 == i0 + o, 1.0, 0.0).astype(BF16)

    def conv_silu(cs):
        x = pm_ref[:, cs]
        tap = lambda o: cw_ref[half + o:half + o + 1, cs]
        shifted = jnp.dot(shift_ref[...], x, preferred_element_type=F32)
        acc = tap(0) * x.astype(F32)
        for m, o in enumerate(offsets):
            acc = acc + tap(o) * shifted[m * tr:(m + 1) * tr, :]
        prev = pp_ref[:, cs].astype(F32)[8:16, :] * keep_prev
        nxt = pn_ref[:, cs].astype(F32)[0:8, :] * keep_next
        sub = lax.broadcasted_iota(jnp.int32, prev.shape, 0)
        top = jnp.zeros_like(prev)
        bot = jnp.zeros_like(prev)
        for o in range(1, half + 1):
            top = top + tap(-o) * jnp.where(sub < o, pltpu.roll(prev, o, 0), 0.0)
            bot = bot + tap(o) * jnp.where(sub >= 8 - o, pltpu.roll(nxt, 8 - o, 0), 0.0)
        return _silu(jnp.concatenate([acc[0:8] + top, acc[8:tr - 8], acc[tr - 8:tr] + bot], axis=0))

    for j, scale in ((PCOL_Q, HEAD_DIM ** -0.5), (PCOL_K, 1.0)):
        y = conv_silu(slice(j * D_GDN, (j + 1) * D_GDN))
        for h in range(N_HEADS):
            yh = y[:, h * HEAD_DIM:(h + 1) * HEAD_DIM]
            inv = lax.rsqrt(jnp.sum(yh * yh, axis=-1, keepdims=True) + EPS) * scale
            qkv_ref[:, j * D_GDN + h * HEAD_DIM:j * D_GDN + (h + 1) * HEAD_DIM] = (yh * inv).astype(BF16)
    vs = slice(PCOL_V * D_GDN, (PCOL_V + 1) * D_GDN)
    qkv_ref[:, vs] = conv_silu(vs).astype(BF16)

    s = small_ref[...]
    lane = lax.broadcasted_iota(jnp.int32, s.shape, 1)
    beta = _sigmoid(s)
    z = s + dtb_ref[...]
    softplus = jnp.maximum(z, 0.0) + jnp.log(1.0 + jnp.exp(-jnp.abs(z)))
    g = nega_ref[...] * softplus
    gb_ref[...] = jnp.where(lane < 2 * N_HEADS, beta, jnp.where(lane < 4 * N_HEADS, g, 0.0))


def _prep(p, small, conv_w8, nega, dtb, first_tiles, last_tiles):
    ta = p.shape[0]
    n_tiles = ta // PREP_TILE
    sub = PREP_TILE // 16
    n_sub = ta // 16
    width = 3 * D_GDN
    kern = functools.partial(_prep_kernel, first_tiles=first_tiles, last_tiles=last_tiles)
    return pl.pallas_call(
        kern,
        grid=(n_tiles,),
        in_specs=[pl.BlockSpec((PREP_TILE, width), lambda r: (r, 0)),
                  pl.BlockSpec((16, width), lambda r: (jnp.maximum(r * sub - 1, 0), 0)),
                  pl.BlockSpec((16, width), lambda r: (jnp.minimum((r + 1) * sub, n_sub - 1), 0)),
                  pl.BlockSpec((8, width), lambda r: (0, 0)),
                  pl.BlockSpec((PREP_TILE, LANES), lambda r: (r, 0)),
                  pl.BlockSpec((1, LANES), lambda r: (0, 0)),
                  pl.BlockSpec((1, LANES), lambda r: (0, 0))],
        out_specs=[pl.BlockSpec((PREP_TILE, width), lambda r: (r, 0)),
                   pl.BlockSpec((PREP_TILE, LANES), lambda r: (r, 0))],
        out_shape=[jax.ShapeDtypeStruct((ta, width), BF16),
                   jax.ShapeDtypeStruct((ta, LANES), F32)],
        scratch_shapes=[pltpu.VMEM(((CONV_K - 1) * PREP_TILE, PREP_TILE), BF16)],
        compiler_params=_params("arbitrary"),
        name="prep",
    )(p, p, p, conv_w8, small, nega, dtb)


def _gdn_decays(d, gb):
    c = GDN_CHUNK
    row = lax.broadcasted_iota(jnp.int32, (c, c), 0)
    col = lax.broadcasted_iota(jnp.int32, (c, c), 1)
    incl = row >= col if d == 0 else row <= col
    lane = lax.broadcasted_iota(jnp.int32, gb.shape, 1)
    g_only = jnp.where(lane >= 2 * N_HEADS, jnp.where(lane < 4 * N_HEADS, gb, 0.0), 0.0)
    tri = jnp.where(incl, 1.0, 0.0).astype(BF16)
    g_hi = g_only.astype(BF16)
    g_r1 = g_only - g_hi.astype(F32)
    g_mid = g_r1.astype(BF16)
    g_lo = (g_r1 - g_mid.astype(F32)).astype(BF16)
    dot = lambda a, b: jnp.dot(a, b, preferred_element_type=F32)
    gcum = dot(tri, g_hi) + dot(tri, g_mid) + dot(tri, g_lo)
    g_end = gcum[c - 1:c, :] if d == 0 else gcum[0:1, :]
    return gcum, gcum.T, jnp.exp(gcum), jnp.exp(g_end - gcum), jnp.exp(g_end)


def _gdn_kernel(qf, kf, vf, gbf, qb, kb, vb, gbb, of_ref, ob_ref, s_ref):
    @pl.when(pl.program_id(1) == 0)
    def _():
        s_ref[...] = jnp.zeros_like(s_ref)

    c = GDN_CHUNK
    n_sub = qf.shape[0] // c
    row = lax.broadcasted_iota(jnp.int32, (c, c), 0)
    col = lax.broadcasted_iota(jnp.int32, (c, c), 1)
    eye = jnp.where(row == col, 1.0, 0.0)
    masks = ((row >= col, row > col), (row <= col, row < col))
    refs = ((qf, kf, vf, gbf, of_ref), (qb, kb, vb, gbb, ob_ref))

    def rows(d, sub):
        k = sub if d == 0 else n_sub - 1 - sub
        return slice(k * c, (k + 1) * c)

    scans = [(d, sub) for sub in range(n_sub) for d in range(2)]
    gbv = {ds: refs[ds[0]][3][rows(*ds), :] for ds in scans}
    dec = {ds: _gdn_decays(ds[0], gbv[ds]) for ds in scans}
    chains = [(d, sub, h) for d, sub in scans for h in range(N_HEADS)]

    st = []
    for d, sub, h in chains:
        q_ref, k_ref, v_ref, _, _ = refs[d]
        rs, hs = rows(d, sub), slice(h * HEAD_DIM, (h + 1) * HEAD_DIM)
        q, k, v = q_ref[rs, hs], k_ref[rs, hs], v_ref[rs, hs]
        both = _mm_nt(jnp.concatenate([q, k], axis=0), k)
        st.append(dict(q=q.astype(F32), k=k.astype(F32), v=v.astype(F32), qk=both[:c], kk=both[c:]))

    for (d, sub, h), e in zip(chains, st):
        cb = d * N_HEADS + h
        cg = 2 * N_HEADS + cb
        gcum, gcum_t, exp_g, exp_rest, exp_end = dec[d, sub]
        incl, strict = masks[d]
        e["beta"] = gbv[d, sub][:, cb:cb + 1]
        diff = gcum[:, cg:cg + 1] - gcum_t[cg:cg + 1, :]
        decay = jnp.where(incl, jnp.exp(jnp.minimum(diff, 0.0)), 0.0)
        e["y"] = jnp.where(strict, -(e["beta"] * e["kk"] * decay), 0.0)
        e["qk"] = e["qk"] * decay
        e["eg"] = exp_g[:, cg:cg + 1]
        e["er"] = exp_rest[:, cg:cg + 1]
        e["ee"] = exp_end[:, cg:cg + 1]

    levels = range(GDN_CHUNK.bit_length() - 1)
    sels = []
    for d in range(2):
        inner, outer = (col, row) if d == 0 else (row, col)
        sels.append([((outer >> lvl) & 1 == 1) & ((inner >> lvl) == (outer >> lvl) - 1) for lvl in levels])
    for (d, sub, h), e in zip(chains, st):
        e["t"] = eye + jnp.where(sels[d][0], e["y"], 0.0)
    def active_rows(d, b):
        return [(2 * p + 1 - d) * b for p in range(c // (2 * b))]

    def pick(x, starts, b):
        return jnp.concatenate([x[s:s + b] for s in starts], axis=0)

    def place(pieces, starts, b, base):
        out = []
        for blk in range(c // b):
            cur = None if base is None else base[blk * b:(blk + 1) * b]
            if blk * b in starts:
                piece = pieces[starts.index(blk * b) * b:(starts.index(blk * b) + 1) * b]
                cur = piece if cur is None else cur + piece
            out.append(jnp.zeros((b, c), F32) if cur is None else cur)
        return jnp.concatenate(out, axis=0)

    for lvl in levels[1:]:
        b = 1 << lvl
        if b % 8 == 0:
            for (d, sub, h), e in zip(chains, st):
                act = active_rows(d, b)
                yd = _mm(pick(jnp.where(sels[d][lvl], e["y"], 0.0), act, b), e["t"])
                e["yd"] = place(yd, act, b, None)
            for (d, sub, h), e in zip(chains, st):
                act = active_rows(d, b)
                e["t"] = place(_mm(pick(e["t"], act, b), e["yd"]), act, b, e["t"])
        else:
            for (d, sub, h), e in zip(chains, st):
                e["yd"] = _mm(jnp.where(sels[d][lvl], e["y"], 0.0), e["t"])
            for e in st:
                e["t"] = e["t"] + _mm(e["t"], e["yd"])

    for e in st:
        kb_ = e["k"] * e["beta"]
        uw = _mm(e["t"], jnp.concatenate([e["v"] * e["beta"], kb_ * e["eg"]], axis=-1))
        e["u"], e["w"] = uw[:, :HEAD_DIM], uw[:, HEAD_DIM:]

    state = {(d, h): s_ref[d, h] for d in range(2) for h in range(N_HEADS)}
    for step in range(n_sub):
        now = [(key, e) for key, e in zip(chains, st) if key[1] == step]
        for (d, sub, h), e in now:
            ws = _mm(jnp.concatenate([e["w"], e["q"] * e["eg"]], axis=0), state[d, h])
            e["v_new"] = e["u"] - ws[:c]
            e["o"] = ws[c:]
        for (d, sub, h), e in now:
            k_dec_t = (e["k"] * e["er"]).T
            out = _mm(jnp.concatenate([e["qk"], k_dec_t], axis=0), e["v_new"])
            refs[d][4][rows(d, sub), h * HEAD_DIM:(h + 1) * HEAD_DIM] = (e["o"] + out[:c]).astype(BF16)
            state[d, h] = state[d, h] * e["ee"] + out[c:]
    for (d, h), s in state.items():
        s_ref[d, h] = s


def _gdn(qkv, gb, batch, ctx_blocks, lat_blocks):
    ta = qkv.shape[0]
    rows = GDN_CHUNK * GDN_STEP_CHUNKS
    n_steps = ctx_blocks + lat_blocks
    lat0 = batch * ctx_blocks

    def fwd_blk(b, s):
        return jnp.where(s < ctx_blocks, b * ctx_blocks + s, lat0 + b * lat_blocks + (s - ctx_blocks))

    def bwd_blk(b, s):
        return jnp.where(s < ctx_blocks, b * ctx_blocks + (ctx_blocks - 1 - s),
                         lat0 + b * lat_blocks + (lat_blocks - 1 - (s - ctx_blocks)))

    def specs(blk):
        col = lambda j: pl.BlockSpec((rows, D_MODEL), lambda b, s: (blk(b, s), j))
        return [col(0), col(1), col(2), pl.BlockSpec((rows, LANES), lambda b, s: (blk(b, s), 0))]

    out = lambda blk: pl.BlockSpec((rows, D_MODEL), lambda b, s: (blk(b, s), 0))
    return pl.pallas_call(
        _gdn_kernel,
        grid=(batch, n_steps),
        in_specs=specs(fwd_blk) + specs(bwd_blk),
        out_specs=[out(fwd_blk), out(bwd_blk)],
        out_shape=[jax.ShapeDtypeStruct((ta, D_MODEL), BF16)] * 2,
        scratch_shapes=[pltpu.VMEM((2, N_HEADS, HEAD_DIM, HEAD_DIM), F32)],
        compiler_params=_params("arbitrary", "arbitrary"),
        name="gdn",
    )(qkv, qkv, qkv, gb, qkv, qkv, qkv, gb)


def _merge_kernel(of_ref, ob_ref, z_ref, su_ref, sv_ref, ga_ref, gb_ref, x_ref, ws_ref, bs_ref,
                  wa_ref, wb_ref, wo_ref, gng_ref, gate_ref, nfg_ref, shift_ref, scale_ref, wr_ref, br_ref,
                  xmid_ref, h2_ref, rgate_ref, ygdn_ref, ysgu_ref):
    tm = x_ref.shape[0]
    parts = [slice(k * tm // MERGE_PARTS, (k + 1) * tm // MERGE_PARTS) for k in range(MERGE_PARTS)]
    dot = lambda a, b: jnp.dot(a, b, preferred_element_type=F32)

    for ch in range(tm // SGU_CHUNK):
        rs = slice(ch * SGU_CHUNK, (ch + 1) * SGU_CHUNK)
        for g in range(SGU_GROUPS):
            cs = slice(g * SGU_GROUP_DIM, (g + 1) * SGU_GROUP_DIM)
            mixed = dot(ws_ref[g], sv_ref[rs, cs]) + bs_ref[:, cs]
            ysgu_ref[rs, cs] = (su_ref[rs, cs].astype(F32) * mixed).astype(BF16)

    for rs in parts:
        o = of_ref[rs, :].astype(F32) + ob_ref[rs, :].astype(F32)
        for h in range(N_HEADS):
            hs = slice(h * HEAD_DIM, (h + 1) * HEAD_DIM)
            oh = o[:, hs]
            inv = lax.rsqrt(jnp.mean(oh * oh, axis=-1, keepdims=True) + EPS)
            ygdn_ref[rs, hs] = (oh * inv * gng_ref[...] * z_ref[rs, hs].astype(F32)).astype(BF16)
    ya = [dot(ygdn_ref[rs, :], wa_ref[...]) for rs in parts]
    yb = [dot(ysgu_ref[rs, :], wb_ref[...]) for rs in parts]
    merged = [(ga_ref[rs, :].astype(F32) * a + gb_ref[rs, :].astype(F32) * b_).astype(BF16)
              for rs, a, b_ in zip(parts, ya, yb)]
    mix = [dot(m, wo_ref[...]) for m in merged]
    h2s = []
    for rs, mx in zip(parts, mix):
        xm = x_ref[rs, :] + gate_ref[0] * mx
        xmid_ref[rs, :] = xm
        hn = xm * lax.rsqrt(jnp.mean(xm * xm, axis=-1, keepdims=True) + EPS) * nfg_ref[...]
        h2 = hn * (1.0 + scale_ref[0]) + shift_ref[0]
        h2_ref[rs, :] = h2.astype(BF16)
        h2s.append(h2)

    wr = wr_ref[...]
    w_hi = wr.astype(BF16)
    w_lo = (wr - w_hi.astype(F32)).astype(BF16)
    logits = []
    for h2 in h2s:
        h_hi = h2.astype(BF16)
        h_lo = (h2 - h_hi.astype(F32)).astype(BF16)
        logits.append(dot(h_hi, w_hi) + dot(h_lo, w_hi) + dot(h_hi, w_lo) + br_ref[...])

    for rs, lg in zip(parts, logits):
        lane = lax.broadcasted_iota(jnp.int32, lg.shape, 1).astype(F32)
        far = float(LANES)
        gl = jnp.where(lane < N_EXPERT_GROUPS, lg, NEG_BIG)
        gmax = jnp.max(gl, axis=-1, keepdims=True)
        p_g = 1.0 / jnp.sum(jnp.exp(gl - gmax), axis=-1, keepdims=True)
        grp = jnp.min(jnp.where(gl == gmax, lane, far), axis=-1, keepdims=True)
        lo = N_EXPERT_GROUPS + EXPERTS_PER_GROUP * grp
        in_grp = jnp.where(lane >= lo, jnp.where(lane < lo + EXPERTS_PER_GROUP, 1.0, 0.0), 0.0)
        el = jnp.where(in_grp > 0.0, lg, NEG_BIG)
        m1 = jnp.max(el, axis=-1, keepdims=True)
        i1 = jnp.min(jnp.where(el == m1, lane, far), axis=-1, keepdims=True)
        el2 = jnp.where(lane == i1, NEG_BIG, el)
        m2 = jnp.max(el2, axis=-1, keepdims=True)
        i2 = jnp.min(jnp.where(el2 == m2, lane, far), axis=-1, keepdims=True)
        t = jnp.exp(m2 - m1)
        w1 = p_g / (1.0 + t)
        w2 = w1 * t
        rgate_ref[rs, :] = jnp.where(lane == 0.0, grp,
                                     jnp.where(lane == i1, w1, 0.0) + jnp.where(lane == i2, w2, 0.0))


def _merge(o_f, o_b, p, x2, w_s, b_s, w_a, w_b, w_o, gng, gate3, nfg, shift3, scale3, w_r, b_r,
           lat_tile0, tiles_per_batch):
    t_lat = x2.shape[0]
    n_tiles = t_lat // ROW_TILE
    lat = lambda c: pl.BlockSpec((ROW_TILE, D_MODEL), lambda i: (i + lat_tile0, c))
    own = lambda: pl.BlockSpec((ROW_TILE, D_MODEL), lambda i: (i, 0))
    mat = lambda: pl.BlockSpec((D_MODEL, D_MODEL), lambda i: (0, 0))
    vec = lambda: pl.BlockSpec((1, D_MODEL), lambda i: (0, 0))
    per_b = lambda: pl.BlockSpec((1, 1, D_MODEL), lambda i: (i // tiles_per_batch, 0, 0))
    return pl.pallas_call(
        _merge_kernel,
        grid=(n_tiles,),
        in_specs=[lat(0), lat(0), lat(PCOL_Z), lat(PCOL_U), lat(PCOL_SV), lat(PCOL_GA), lat(PCOL_GB), own(),
                  pl.BlockSpec((SGU_GROUPS, SGU_CHUNK, SGU_CHUNK), lambda i: (0, 0, 0)),
                  pl.BlockSpec((SGU_CHUNK, D_MODEL), lambda i: (0, 0)),
                  mat(), mat(), mat(), pl.BlockSpec((1, HEAD_DIM), lambda i: (0, 0)),
                  per_b(), vec(), per_b(), per_b(),
                  pl.BlockSpec((D_MODEL, LANES), lambda i: (0, 0)),
                  pl.BlockSpec((1, LANES), lambda i: (0, 0))],
        out_specs=[own(), own(), pl.BlockSpec((ROW_TILE, LANES), lambda i: (i, 0))],
        out_shape=[jax.ShapeDtypeStruct((t_lat, D_MODEL), F32),
                   jax.ShapeDtypeStruct((t_lat, D_MODEL), BF16),
                   jax.ShapeDtypeStruct((t_lat, LANES), F32)],
        scratch_shapes=[pltpu.VMEM((ROW_TILE, D_MODEL), BF16), pltpu.VMEM((ROW_TILE, D_MODEL), BF16)],
        compiler_params=_params("arbitrary"),
        name="merge",
    )(o_f, o_b, p, p, p, p, p, x2, w_s, b_s, w_a, w_b, w_o, gng, gate3, nfg, shift3, scale3, w_r, b_r)


def _split3(x):
    hi = x.astype(BF16)
    r1 = x - hi.astype(F32)
    mid = r1.astype(BF16)
    return hi, mid, (r1 - mid.astype(F32)).astype(BF16)


def _moe_kernel(h_ref, rg_ref, w1_ref, w3_ref, w2_ref, xmid_ref, gate_ref, fng_ref, o_ref,
                before_ref, destc_ref, xs_ref, ys_ref, gs_ref, start_ref):
    i, g, hf = pl.program_id(0), pl.program_id(1), pl.program_id(2)
    tm = h_ref.shape[0]
    dot = lambda a, b: jnp.dot(a, b, preferred_element_type=F32)

    @pl.when((i == 0) & (g == 0) & (hf == 0))
    def _():
        r = lax.broadcasted_iota(jnp.int32, (tm, tm), 0)
        c = lax.broadcasted_iota(jnp.int32, (tm, tm), 1)
        before_ref[...] = jnp.where(r < c, 1.0, 0.0).astype(BF16)

    @pl.when((g == 0) & (hf == 0))
    def _():
        rg = rg_ref[...]
        sub = lax.broadcasted_iota(jnp.int32, (8, tm), 0).astype(F32)
        m_rows = jnp.where(rg.T[0:1, :] == sub, 1.0, 0.0)
        total_r = jnp.sum(m_rows, axis=1, keepdims=True)
        start_r = jnp.zeros_like(total_r)
        for gg in range(N_EXPERT_GROUPS - 1):
            start_r = start_r + jnp.where(sub[:, 0:1] > gg, total_r[gg:gg + 1, :], 0.0)
        rank_r = dot(m_rows.astype(BF16), before_ref[...])
        dest_r = jnp.sum(jnp.where(m_rows > 0.0, rank_r + start_r, 0.0), axis=0, keepdims=True)
        destc_ref[...] = jnp.broadcast_to(dest_r, (LANES, tm)).T
        acc = jnp.int32(0)
        for gg in range(N_EXPERT_GROUPS):
            start_ref[gg] = acc
            acc = acc + total_r[gg, 0].astype(jnp.int32)
        start_ref[N_EXPERT_GROUPS] = acc
        slot = lax.broadcasted_iota(jnp.int32, (tm, tm), 0).astype(F32)
        perm = jnp.where(dest_r == slot, 1.0, 0.0).astype(BF16)
        g_hi, g_mid, _ = _split3(rg)
        moved = dot(perm, jnp.concatenate([h_ref[...], g_hi, g_mid], axis=1))
        xs_ref[...] = moved[:, :D_MODEL].astype(BF16)
        gs_ref[...] = moved[:, D_MODEL:D_MODEL + LANES] + moved[:, D_MODEL + LANES:]
        ys_ref[...] = jnp.zeros_like(ys_ref)

    bs = MOE_BLOCK
    first = start_ref[g] // bs
    n_unit = (start_ref[g + 1] + (bs - 1)) // bs - first

    def experts(row0, size):
        rows = pl.ds(pl.multiple_of(row0, bs), size)
        x = xs_ref[rows, :]
        gs = gs_ref[rows, :]
        lane = lax.broadcasted_iota(jnp.int32, gs.shape, 1)
        y = ys_ref[rows, :]
        for e in range(MOE_EXPERTS_PER_STEP):
            col = N_EXPERT_GROUPS + g * EXPERTS_PER_GROUP + hf * MOE_EXPERTS_PER_STEP + e
            gate = jnp.sum(jnp.where(lane == col, gs, 0.0), axis=-1, keepdims=True)
            hid = _silu(dot(x, w1_ref[e])) * dot(x, w3_ref[e]) * gate
            y = y + dot(hid.astype(BF16), w2_ref[e])
        ys_ref[rows, :] = y

    big = MOE_BLOCK_MULTIPLES[0]

    def big_block(k, carry):
        experts((first + k * big) * bs, big * bs)
        return carry

    lax.fori_loop(0, n_unit // big, big_block, 0)
    done = first + (n_unit // big) * big
    for m in MOE_BLOCK_MULTIPLES[1:]:
        take = (n_unit & m) != 0

        @pl.when(take)
        def _(done=done, m=m):
            experts(done * bs, m * bs)

        done = done + jnp.where(take, m, 0)

    @pl.when((g == pl.num_programs(1) - 1) & (hf == pl.num_programs(2) - 1))
    def _():
        slot = lax.broadcasted_iota(jnp.int32, (tm, tm), 1).astype(F32)
        perm_t = jnp.where(destc_ref[:, 0:1] == slot, 1.0, 0.0).astype(BF16)
        xo = xmid_ref[...] + gate_ref[0] * dot(perm_t, ys_ref[...].astype(BF16))
        o_ref[...] = xo * lax.rsqrt(jnp.mean(xo * xo, axis=-1, keepdims=True) + EPS) * fng_ref[...]


def _moe(h2, rgate, w1, w3, w2, xmid, gate3, fng, tiles_per_batch):
    t_lat = h2.shape[0]
    n_tiles = t_lat // MOE_TILE
    eps = MOE_EXPERTS_PER_STEP
    halves = EXPERTS_PER_GROUP // eps
    row = lambda w: pl.BlockSpec((MOE_TILE, w), lambda i, g, hf: (i, 0))
    wspec = lambda a, b: pl.BlockSpec((eps, a, b), lambda i, g, hf: (g * halves + hf, 0, 0))
    return pl.pallas_call(
        _moe_kernel,
        grid=(n_tiles, N_EXPERT_GROUPS, halves),
        in_specs=[row(D_MODEL), row(LANES),
                  wspec(D_MODEL, D_EXPERT), wspec(D_MODEL, D_EXPERT), wspec(D_EXPERT, D_MODEL),
                  row(D_MODEL),
                  pl.BlockSpec((1, 1, D_MODEL), lambda i, g, hf: (i // tiles_per_batch, 0, 0)),
                  pl.BlockSpec((1, D_MODEL), lambda i, g, hf: (0, 0))],
        out_specs=row(D_MODEL),
        out_shape=jax.ShapeDtypeStruct((t_lat, D_MODEL), F32),
        scratch_shapes=[pltpu.VMEM((MOE_TILE, MOE_TILE), BF16),
                        pltpu.VMEM((MOE_TILE, LANES), F32),
                        pltpu.VMEM((MOE_TILE, D_MODEL), BF16),
                        pltpu.VMEM((MOE_TILE, D_MODEL), F32),
                        pltpu.VMEM((MOE_TILE, LANES), F32),
                        pltpu.SMEM((N_EXPERT_GROUPS + 1,), jnp.int32)],
        compiler_params=pltpu.CompilerParams(dimension_semantics=("arbitrary",) * 3,
                                             vmem_limit_bytes=MOE_VMEM_LIMIT),
        name="moe",
    )(h2, rgate, w1, w3, w2, xmid, gate3, fng)


def kernel(x, c, ctx, c_ctx, ada_w, ada_b, norm_mix_g, w_in, conv_w, a_log, dt_bias, gdn_norm_g, sgu_ln_g, sgu_ln_b, sgu_w, sgu_b, w_branch_a, w_branch_b, w_out, norm_ffn_g, router_group_w, router_group_b, router_expert_w, router_expert_b, expert_w1, expert_w3, expert_w2, final_norm_g):
    batch, seq, d = x.shape
    ctx_len = ctx.shape[1]
    assert d == D_MODEL and ada_w.shape[0] == 1, "single-layer block with D_MODEL channels"
    assert batch * ctx_len == ROW_TILE, "context rows of all samples form one row tile"
    assert seq % MOE_TILE == 0 and ctx_len % PREP_TILE == 0 and batch + 1 <= 8
    t_lat = batch * seq
    row = lambda v: v.reshape(1, -1).astype(F32)

    cond = jnp.zeros((8, d), F32).at[:batch].set(c).at[batch].set(c_ctx)
    mod = _adaln(cond, ada_w[0], ada_b[0]).reshape(8, N_MOD, d)
    mod_row = lambda k: mod[:batch + 1, k].reshape(batch + 1, 1, d)

    w_l = w_in[0]
    w_head = w_l[:, :COL_BETA].astype(BF16)
    w_tail = w_l[:, COL_Z:].astype(BF16)
    w_small = jnp.zeros((d, LANES), BF16).at[:, :4 * N_HEADS].set(w_l[:, COL_BETA:COL_Z].astype(BF16))
    x2 = x.reshape(t_lat, d)
    tiles_per_batch = seq // ROW_TILE
    p, small = _inproj(x2, ctx.reshape(batch * ctx_len, d), row(norm_mix_g), mod_row(0), mod_row(1),
                       w_head, w_tail, w_small, row(sgu_ln_g), row(sgu_ln_b), tiles_per_batch)

    ctx_t, lat_t = ctx_len // PREP_TILE, seq // PREP_TILE
    starts = [b * ctx_t for b in range(batch)] + [batch * ctx_t + b * lat_t for b in range(batch)]
    ends = [(b + 1) * ctx_t - 1 for b in range(batch)] + [batch * ctx_t + (b + 1) * lat_t - 1 for b in range(batch)]
    conv_w8 = jnp.zeros((8, 3 * D_GDN), F32).at[:CONV_K].set(conv_w[0])
    pad_lanes = lambda v: jnp.zeros((1, LANES), F32).at[0, 2 * N_HEADS:4 * N_HEADS].set(v.reshape(-1))
    qkv, gb = _prep(p, small, conv_w8, pad_lanes(-jnp.exp(a_log[0])), pad_lanes(dt_bias[0]),
                    tuple(starts), tuple(ends))

    step_rows = GDN_CHUNK * GDN_STEP_CHUNKS
    assert ctx_len % step_rows == 0 and seq % step_rows == 0
    o_f, o_b = _gdn(qkv, gb, batch, ctx_len // step_rows, seq // step_rows)

    b_full = jnp.repeat(sgu_b[0].T, SGU_GROUP_DIM, axis=1).astype(F32)

    w_r = jnp.zeros((d, LANES), F32).at[:, :N_EXPERT_GROUPS].set(router_group_w[0]) \
        .at[:, N_EXPERT_GROUPS:N_EXPERT_GROUPS + N_EXPERTS].set(router_expert_w[0])
    b_r = jnp.zeros((1, LANES), F32).at[0, :N_EXPERT_GROUPS].set(router_group_b[0]) \
        .at[0, N_EXPERT_GROUPS:N_EXPERT_GROUPS + N_EXPERTS].set(router_expert_b[0])
    gng = gdn_norm_g[0].reshape(1, HEAD_DIM).astype(F32)
    xmid, h2, rgate = _merge(o_f, o_b, p, x2, sgu_w[0].astype(BF16), b_full,
                             w_branch_a[0].astype(BF16), w_branch_b[0].astype(BF16),
                             w_out[0].astype(BF16), gng, mod_row(2)[:batch], row(norm_ffn_g),
                             mod_row(3)[:batch], mod_row(4)[:batch], w_r, b_r, 1, tiles_per_batch)

    out = _moe(h2, rgate, expert_w1[0].astype(BF16), expert_w3[0].astype(BF16), expert_w2[0].astype(BF16),
               xmid, mod_row(5)[:batch], row(final_norm_g), seq // MOE_TILE)
    return out.reshape(batch, seq, d)
```

```python
import functools
import math

import jax
import jax.numpy as jnp
from jax import lax
from jax.experimental import pallas as pl
from jax.experimental.pallas import tpu as pltpu

F32 = jnp.float32
BF16 = jnp.bfloat16

D_MODEL = 1024
N_HEADS = 8
HEAD_DIM = 128
D_GDN = N_HEADS * HEAD_DIM
CONV_K = 5
GDN_CHUNK = 64
GDN_STEP_CHUNKS = 4
SGU_GROUPS = 8
SGU_GROUP_DIM = 128
D_SGU = SGU_GROUPS * SGU_GROUP_DIM
SGU_CHUNK = 128
N_EXPERT_GROUPS = 4
EXPERTS_PER_GROUP = 8
N_EXPERTS = N_EXPERT_GROUPS * EXPERTS_PER_GROUP
D_EXPERT = 256
N_MOD = 6
EPS = 1e-6
COL_BETA = 3 * D_GDN
COL_Z = COL_BETA + 4 * N_HEADS

LANES = 128
ROW_TILE = 512
WSPLIT_ROWS = 128
INPROJ_SEGS = 4
MERGE_PARTS = 2
PREP_TILE = 256
MOE_TILE = 1024
MOE_EXPERTS_PER_STEP = 4
MOE_BLOCK = 128
MOE_BLOCK_MULTIPLES = (4, 2, 1)
VMEM_LIMIT = 48 * 1024 * 1024
MOE_VMEM_LIMIT = 56 * 1024 * 1024
NEG_BIG = -1e30

PCOL_Q, PCOL_K, PCOL_V, PCOL_GB, PCOL_Z, PCOL_U, PCOL_SV, PCOL_GA = range(8)


def _mm(a, b):
    return jnp.dot(a.astype(BF16), b.astype(BF16), preferred_element_type=F32)


def _mm_nt(a, b):
    return lax.dot_general(a.astype(BF16), b.astype(BF16), (((1,), (1,)), ((), ())),
                           preferred_element_type=F32)


def _sigmoid(x):
    return 0.5 + 0.5 * jnp.tanh(0.5 * x)


def _silu(x):
    return x * _sigmoid(x)


def _gelu_tanh(x):
    return 0.5 * x * (1.0 + jnp.tanh(math.sqrt(2.0 / math.pi) * (x + 0.044715 * (x * x * x))))


def _params(*sem):
    return pltpu.CompilerParams(dimension_semantics=sem, vmem_limit_bytes=VMEM_LIMIT)


def _adaln_kernel(c_ref, w_ref, b_ref, o_ref):
    o_ref[...] = _mm(_silu(c_ref[...]), w_ref[...]) + b_ref[...]


def _adaln(cond, w, b):
    n = w.shape[1]
    tn = 1536
    return pl.pallas_call(
        _adaln_kernel,
        grid=(n // tn,),
        in_specs=[pl.BlockSpec((8, D_MODEL), lambda j: (0, 0)),
                  pl.BlockSpec((D_MODEL, tn), lambda j: (0, j)),
                  pl.BlockSpec((1, tn), lambda j: (0, j))],
        out_specs=pl.BlockSpec((8, tn), lambda j: (0, j)),
        out_shape=jax.ShapeDtypeStruct((8, n), F32),
        compiler_params=_params("arbitrary"),
        name="adaln",
    )(cond, w, b.reshape(1, n))


def _wsplit_kernel(w_ref, head_ref, tail_ref, small_ref):
    w = w_ref[...]
    head_ref[...] = w[:, :COL_BETA].astype(BF16)
    tail_ref[...] = w[:, COL_Z:].astype(BF16)
    pad = jnp.zeros((w.shape[0], LANES - (COL_Z - COL_BETA)), F32)
    small_ref[...] = jnp.concatenate([w[:, COL_BETA:COL_Z], pad], axis=1).astype(BF16)


def _wsplit(w):
    d, n = w.shape
    rows = WSPLIT_ROWS
    return pl.pallas_call(
        _wsplit_kernel,
        grid=(d // rows,),
        in_specs=[pl.BlockSpec((rows, n), lambda i: (i, 0))],
        out_specs=[pl.BlockSpec((rows, COL_BETA), lambda i: (i, 0)),
                   pl.BlockSpec((rows, n - COL_Z), lambda i: (i, 0)),
                   pl.BlockSpec((rows, LANES), lambda i: (i, 0))],
        out_shape=[jax.ShapeDtypeStruct((d, COL_BETA), BF16),
                   jax.ShapeDtypeStruct((d, n - COL_Z), BF16),
                   jax.ShapeDtypeStruct((d, LANES), BF16)],
        compiler_params=_params("arbitrary"),
        name="wsplit",
    )(w)


def _inproj_kernel(x_ref, ctx_ref, g_ref, shift_ref, scale_ref, wh_ref, wt_ref, ws_ref, lng_ref, lnb_ref,
                   p_ref, small_ref, h_ref):
    i = pl.program_id(0)
    j = pl.program_id(1)

    def norm_mod(xv):
        y = xv * lax.rsqrt(jnp.mean(xv * xv, axis=-1, keepdims=True) + EPS) * g_ref[...]
        return y * (1.0 + scale_ref[0]) + shift_ref[0]

    @pl.when(j == 0)
    def _():
        @pl.when(i == 0)
        def _():
            h_ref[...] = norm_mod(ctx_ref[...]).astype(BF16)

        @pl.when(i > 0)
        def _():
            h_ref[...] = norm_mod(x_ref[...]).astype(BF16)

        small_ref[...] = jnp.dot(h_ref[...], ws_ref[...], preferred_element_type=F32)

    def raw(a):
        return a

    def gelu_ln(a):
        a = _gelu_tanh(a)
        mu = jnp.mean(a, axis=-1, keepdims=True)
        ac = a - mu
        var = jnp.mean(ac * ac, axis=-1, keepdims=True)
        return ac * lax.rsqrt(var + EPS) * lng_ref[...] + lnb_ref[...]

    def project(w_ref, w_seg, p_seg, epilogue):
        a = jnp.dot(h_ref[...], w_ref[:, w_seg * D_MODEL:(w_seg + 1) * D_MODEL], preferred_element_type=F32)
        p_ref[:, p_seg * D_MODEL:(p_seg + 1) * D_MODEL] = epilogue(a).astype(BF16)

    @pl.when(j == 0)
    def _():
        for seg in range(3):
            project(wh_ref, seg, seg, raw)
        project(wt_ref, 0, 3, _sigmoid)

    @pl.when(j == 1)
    def _():
        for seg, epilogue in enumerate((_silu, _gelu_tanh, gelu_ln, _sigmoid)):
            project(wt_ref, seg, seg, epilogue)


def _inproj(x2, ctx2, norm_g, shift3, scale3, w_head, w_tail, w_small, ln_g, ln_b, tiles_per_batch):
    t_lat = x2.shape[0]
    n_lat = t_lat // ROW_TILE
    n_tiles = n_lat + 1
    tn = INPROJ_SEGS * D_MODEL
    assert w_head.shape[1] == tn - D_MODEL and w_tail.shape[1] == tn + D_MODEL
    n_batch = shift3.shape[0] - 1
    sel = lambda i: jnp.where(i == 0, n_batch, (jnp.maximum(i, 1) - 1) // tiles_per_batch)
    vec = lambda: pl.BlockSpec((1, D_MODEL), lambda i, j: (0, 0))
    once = pl.Buffered(1)
    return pl.pallas_call(
        _inproj_kernel,
        grid=(n_tiles, 2),
        in_specs=[pl.BlockSpec((ROW_TILE, D_MODEL), lambda i, j: (jnp.maximum(i, 1) - 1, 0)),
                  pl.BlockSpec((ROW_TILE, D_MODEL), lambda i, j: (0, 0), pipeline_mode=once),
                  vec(),
                  pl.BlockSpec((1, 1, D_MODEL), lambda i, j: (sel(i), 0, 0)),
                  pl.BlockSpec((1, 1, D_MODEL), lambda i, j: (sel(i), 0, 0)),
                  pl.BlockSpec((D_MODEL, tn - D_MODEL), lambda i, j: (0, 0), pipeline_mode=once),
                  pl.BlockSpec((D_MODEL, tn), lambda i, j: (0, 1 - j)),
                  pl.BlockSpec((D_MODEL, LANES), lambda i, j: (0, 0)),
                  vec(), vec()],
        out_specs=[pl.BlockSpec((ROW_TILE, tn), lambda i, j: (i, j)),
                   pl.BlockSpec((ROW_TILE, LANES), lambda i, j: (i, 0))],
        out_shape=[jax.ShapeDtypeStruct((n_tiles * ROW_TILE, 2 * tn), BF16),
                   jax.ShapeDtypeStruct((n_tiles * ROW_TILE, LANES), F32)],
        scratch_shapes=[pltpu.VMEM((ROW_TILE, D_MODEL), BF16)],
        compiler_params=_params("arbitrary", "arbitrary"),
        name="inproj",
    )(x2, ctx2, norm_g, shift3, scale3, w_head, w_tail, w_small, ln_g, ln_b)


def _prep_kernel(pm_ref, pp_ref, pn_ref, cw_ref, small_ref, nega_ref, dtb_ref, qkv_ref, gb_ref, shift_ref,
                 *, first_tiles, last_tiles):
    r = pl.program_id(0)
    tr = pm_ref.shape[0]
    is_first = functools.reduce(jnp.logical_or, [r == t for t in first_tiles])
    is_last = functools.reduce(jnp.logical_or, [r == t for t in last_tiles])
    keep_prev = jnp.where(is_first, 0.0, 1.0)
    keep_next = jnp.where(is_last, 0.0, 1.0)

    half = CONV_K // 2
    offsets = [o for o in range(-half, half + 1) if o != 0]

    @pl.when(r == 0)
    def _():
        i0 = lax.broadcasted_iota(jnp.int32, (tr, tr), 0)
        i1 = lax.broadcasted_iota(jnp.int32, (tr, tr), 1)
        for m, o in enumerate(offsets):
            shift_ref[m * tr:(m + 1) * tr, :] = jnp.where(i1 == i0 + o, 1.0, 0.0).astype(BF16)

    def conv_silu(cs):
        x = pm_ref[:, cs]
        tap = lambda o: cw_ref[half + o:half + o + 1, cs]
        shifted = jnp.dot(shift_ref[...], x, preferred_element_type=F32)
        acc = tap(0) * x.astype(F32)
        for m, o in enumerate(offsets):
            acc = acc + tap(o) * shifted[m * tr:(m + 1) * tr, :]
        prev = pp_ref[:, cs].astype(F32)[8:16, :] * keep_prev
        nxt = pn_ref[:, cs].astype(F32)[0:8, :] * keep_next
        sub = lax.broadcasted_iota(jnp.int32, prev.shape, 0)
        top = jnp.zeros_like(prev)
        bot = jnp.zeros_like(prev)
        for o in range(1, half + 1):
            top = top + tap(-o) * jnp.where(sub < o, pltpu.roll(prev, o, 0), 0.0)
            bot = bot + tap(o) * jnp.where(sub >= 8 - o, pltpu.roll(nxt, 8 - o, 0), 0.0)
        return _silu(jnp.concatenate([acc[0:8] + top, acc[8:tr - 8], acc[tr - 8:tr] + bot], axis=0))

    for j, scale in ((PCOL_Q, HEAD_DIM ** -0.5), (PCOL_K, 1.0)):
        y = conv_silu(slice(j * D_GDN, (j + 1) * D_GDN))
        for h in range(N_HEADS):
            yh = y[:, h * HEAD_DIM:(h + 1) * HEAD_DIM]
            inv = lax.rsqrt(jnp.sum(yh * yh, axis=-1, keepdims=True) + EPS) * scale
            qkv_ref[:, j * D_GDN + h * HEAD_DIM:j * D_GDN + (h + 1) * HEAD_DIM] = (yh * inv).astype(BF16)
    vs = slice(PCOL_V * D_GDN, (PCOL_V + 1) * D_GDN)
    qkv_ref[:, vs] = conv_silu(vs).astype(BF16)

    s = small_ref[...]
    lane = lax.broadcasted_iota(jnp.int32, s.shape, 1)
    beta = _sigmoid(s)
    z = s + dtb_ref[...]
    softplus = jnp.maximum(z, 0.0) + jnp.log(1.0 + jnp.exp(-jnp.abs(z)))
    g = nega_ref[...] * softplus
    gb_ref[...] = jnp.where(lane < 2 * N_HEADS, beta, jnp.where(lane < 4 * N_HEADS, g, 0.0))


def _prep(p, small, conv_w8, nega, dtb, first_tiles, last_tiles):
    ta = p.shape[0]
    n_tiles = ta // PREP_TILE
    sub = PREP_TILE // 16
    n_sub = ta // 16
    width = 3 * D_GDN
    kern = functools.partial(_prep_kernel, first_tiles=first_tiles, last_tiles=last_tiles)
    return pl.pallas_call(
        kern,
        grid=(n_tiles,),
        in_specs=[pl.BlockSpec((PREP_TILE, width), lambda r: (r, 0)),
                  pl.BlockSpec((16, width), lambda r: (jnp.maximum(r * sub - 1, 0), 0)),
                  pl.BlockSpec((16, width), lambda r: (jnp.minimum((r + 1) * sub, n_sub - 1), 0)),
                  pl.BlockSpec((8, width), lambda r: (0, 0)),
                  pl.BlockSpec((PREP_TILE, LANES), lambda r: (r, 0)),
                  pl.BlockSpec((1, LANES), lambda r: (0, 0)),
                  pl.BlockSpec((1, LANES), lambda r: (0, 0))],
        out_specs=[pl.BlockSpec((PREP_TILE, width), lambda r: (r, 0)),
                   pl.BlockSpec((PREP_TILE, LANES), lambda r: (r, 0))],
        out_shape=[jax.ShapeDtypeStruct((ta, width), BF16),
                   jax.ShapeDtypeStruct((ta, LANES), F32)],
        scratch_shapes=[pltpu.VMEM(((CONV_K - 1) * PREP_TILE, PREP_TILE), BF16)],
        compiler_params=_params("arbitrary"),
        name="prep",
    )(p, p, p, conv_w8, small, nega, dtb)


def _gdn_decays(d, gb):
    c = GDN_CHUNK
    row = lax.broadcasted_iota(jnp.int32, (c, c), 0)
    col = lax.broadcasted_iota(jnp.int32, (c, c), 1)
    incl = row >= col if d == 0 else row <= col
    lane = lax.broadcasted_iota(jnp.int32, gb.shape, 1)
    g_only = jnp.where(lane >= 2 * N_HEADS, jnp.where(lane < 4 * N_HEADS, gb, 0.0), 0.0)
    tri = jnp.where(incl, 1.0, 0.0).astype(BF16)
    g_hi = g_only.astype(BF16)
    g_r1 = g_only - g_hi.astype(F32)
    g_mid = g_r1.astype(BF16)
    g_lo = (g_r1 - g_mid.astype(F32)).astype(BF16)
    dot = lambda a, b: jnp.dot(a, b, preferred_element_type=F32)
    gcum = dot(tri, g_hi) + dot(tri, g_mid) + dot(tri, g_lo)
    g_end = gcum[c - 1:c, :] if d == 0 else gcum[0:1, :]
    return gcum, gcum.T, jnp.exp(gcum), jnp.exp(g_end - gcum), jnp.exp(g_end)


def _gdn_kernel(qf, kf, vf, gbf, qb, kb, vb, gbb, of_ref, ob_ref, s_ref):
    @pl.when(pl.program_id(1) == 0)
    def _():
        s_ref[...] = jnp.zeros_like(s_ref)

    c = GDN_CHUNK
    n_sub = qf.shape[0] // c
    row = lax.broadcasted_iota(jnp.int32, (c, c), 0)
    col = lax.broadcasted_iota(jnp.int32, (c, c), 1)
    eye = jnp.where(row == col, 1.0, 0.0)
    masks = ((row >= col, row > col), (row <= col, row < col))
    refs = ((qf, kf, vf, gbf, of_ref), (qb, kb, vb, gbb, ob_ref))

    def rows(d, sub):
        k = sub if d == 0 else n_sub - 1 - sub
        return slice(k * c, (k + 1) * c)

    scans = [(d, sub) for sub in range(n_sub) for d in range(2)]
    gbv = {ds: refs[ds[0]][3][rows(*ds), :] for ds in scans}
    dec = {ds: _gdn_decays(ds[0], gbv[ds]) for ds in scans}
    chains = [(d, sub, h) for d, sub in scans for h in range(N_HEADS)]

    st = []
    for d, sub, h in chains:
        q_ref, k_ref, v_ref, _, _ = refs[d]
        rs, hs = rows(d, sub), slice(h * HEAD_DIM, (h + 1) * HEAD_DIM)
        q, k, v = q_ref[rs, hs], k_ref[rs, hs], v_ref[rs, hs]
        both = _mm_nt(jnp.concatenate([q, k], axis=0), k)
        st.append(dict(q=q.astype(F32), k=k.astype(F32), v=v.astype(F32), qk=both[:c], kk=both[c:]))

    for (d, sub, h), e in zip(chains, st):
        cb = d * N_HEADS + h
        cg = 2 * N_HEADS + cb
        gcum, gcum_t, exp_g, exp_rest, exp_end = dec[d, sub]
        incl, strict = masks[d]
        e["beta"] = gbv[d, sub][:, cb:cb + 1]
        diff = gcum[:, cg:cg + 1] - gcum_t[cg:cg + 1, :]
        decay = jnp.where(incl, jnp.exp(jnp.minimum(diff, 0.0)), 0.0)
        e["y"] = jnp.where(strict, -(e["beta"] * e["kk"] * decay), 0.0)
        e["qk"] = e["qk"] * decay
        e["eg"] = exp_g[:, cg:cg + 1]
        e["er"] = exp_rest[:, cg:cg + 1]
        e["ee"] = exp_end[:, cg:cg + 1]

    levels = range(GDN_CHUNK.bit_length() - 1)
    sels = []
    for d in range(2):
        inner, outer = (col, row) if d == 0 else (row, col)
        sels.append([((outer >> lvl) & 1 == 1) & ((inner >> lvl) == (outer >> lvl) - 1) for lvl in levels])
    for (d, sub, h), e in zip(chains, st):
        e["t"] = eye + jnp.where(sels[d][0], e["y"], 0.0)
    def active_rows(d, b):
        return [(2 * p + 1 - d) * b for p in range(c // (2 * b))]

    def pick(x, starts, b):
        return jnp.concatenate([x[s:s + b] for s in starts], axis=0)

    def place(pieces, starts, b, base):
        out = []
        for blk in range(c // b):
            cur = None if base is None else base[blk * b:(blk + 1) * b]
            if blk * b in starts:
                piece = pieces[starts.index(blk * b) * b:(starts.index(blk * b) + 1) * b]
                cur = piece if cur is None else cur + piece
            out.append(jnp.zeros((b, c), F32) if cur is None else cur)
        return jnp.concatenate(out, axis=0)

    for lvl in levels[1:]:
        b = 1 << lvl
        if b % 8 == 0:
            for (d, sub, h), e in zip(chains, st):
                act = active_rows(d, b)
                yd = _mm(pick(jnp.where(sels[d][lvl], e["y"], 0.0), act, b), e["t"])
                e["yd"] = place(yd, act, b, None)
            for (d, sub, h), e in zip(chains, st):
                act = active_rows(d, b)
                e["t"] = place(_mm(pick(e["t"], act, b), e["yd"]), act, b, e["t"])
        else:
            for (d, sub, h), e in zip(chains, st):
                e["yd"] = _mm(jnp.where(sels[d][lvl], e["y"], 0.0), e["t"])
            for e in st:
                e["t"] = e["t"] + _mm(e["t"], e["yd"])

    for e in st:
        kb_ = e["k"] * e["beta"]
        uw = _mm(e["t"], jnp.concatenate([e["v"] * e["beta"], kb_ * e["eg"]], axis=-1))
        e["u"], e["w"] = uw[:, :HEAD_DIM], uw[:, HEAD_DIM:]

    state = {(d, h): s_ref[d, h] for d in range(2) for h in range(N_HEADS)}
    for step in range(n_sub):
        now = [(key, e) for key, e in zip(chains, st) if key[1] == step]
        for (d, sub, h), e in now:
            ws = _mm(jnp.concatenate([e["w"], e["q"] * e["eg"]], axis=0), state[d, h])
            e["v_new"] = e["u"] - ws[:c]
            e["o"] = ws[c:]
        for (d, sub, h), e in now:
            k_dec_t = (e["k"] * e["er"]).T
            out = _mm(jnp.concatenate([e["qk"], k_dec_t], axis=0), e["v_new"])
            refs[d][4][rows(d, sub), h * HEAD_DIM:(h + 1) * HEAD_DIM] = (e["o"] + out[:c]).astype(BF16)
            state[d, h] = state[d, h] * e["ee"] + out[c:]
    for (d, h), s in state.items():
        s_ref[d, h] = s


def _gdn(qkv, gb, batch, ctx_blocks, lat_blocks):
    ta = qkv.shape[0]
    rows = GDN_CHUNK * GDN_STEP_CHUNKS
    n_steps = ctx_blocks + lat_blocks
    lat0 = batch * ctx_blocks

    def fwd_blk(b, s):
        return jnp.where(s < ctx_blocks, b * ctx_blocks + s, lat0 + b * lat_blocks + (s - ctx_blocks))

    def bwd_blk(b, s):
        return jnp.where(s < ctx_blocks, b * ctx_blocks + (ctx_blocks - 1 - s),
                         lat0 + b * lat_blocks + (lat_blocks - 1 - (s - ctx_blocks)))

    def specs(blk):
        col = lambda j: pl.BlockSpec((rows, D_MODEL), lambda b, s: (blk(b, s), j))
        return [col(0), col(1), col(2), pl.BlockSpec((rows, LANES), lambda b, s: (blk(b, s), 0))]

    out = lambda blk: pl.BlockSpec((rows, D_MODEL), lambda b, s: (blk(b, s), 0))
    return pl.pallas_call(
        _gdn_kernel,
        grid=(batch, n_steps),
        in_specs=specs(fwd_blk) + specs(bwd_blk),
        out_specs=[out(fwd_blk), out(bwd_blk)],
        out_shape=[jax.ShapeDtypeStruct((ta, D_MODEL), BF16)] * 2,
        scratch_shapes=[pltpu.VMEM((2, N_HEADS, HEAD_DIM, HEAD_DIM), F32)],
        compiler_params=_params("arbitrary", "arbitrary"),
        name="gdn",
    )(qkv, qkv, qkv, gb, qkv, qkv, qkv, gb)


def _merge_kernel(of_ref, ob_ref, z_ref, su_ref, sv_ref, ga_ref, gb_ref, x_ref, ws_ref, bs_ref,
                  wa_ref, wb_ref, wo_ref, gng_ref, gate_ref, nfg_ref, shift_ref, scale_ref, wr_ref, br_ref,
                  xmid_ref, h2_ref, rgate_ref, ygdn_ref, ysgu_ref):
    tm = x_ref.shape[0]
    parts = [slice(k * tm // MERGE_PARTS, (k + 1) * tm // MERGE_PARTS) for k in range(MERGE_PARTS)]
    dot = lambda a, b: jnp.dot(a, b, preferred_element_type=F32)

    for ch in range(tm // SGU_CHUNK):
        rs = slice(ch * SGU_CHUNK, (ch + 1) * SGU_CHUNK)
        for g in range(SGU_GROUPS):
            cs = slice(g * SGU_GROUP_DIM, (g + 1) * SGU_GROUP_DIM)
            mixed = dot(ws_ref[g], sv_ref[rs, cs]) + bs_ref[:, cs]
            ysgu_ref[rs, cs] = (su_ref[rs, cs].astype(F32) * mixed).astype(BF16)

    for rs in parts:
        o = of_ref[rs, :].astype(F32) + ob_ref[rs, :].astype(F32)
        for h in range(N_HEADS):
            hs = slice(h * HEAD_DIM, (h + 1) * HEAD_DIM)
            oh = o[:, hs]
            inv = lax.rsqrt(jnp.mean(oh * oh, axis=-1, keepdims=True) + EPS)
            ygdn_ref[rs, hs] = (oh * inv * gng_ref[...] * z_ref[rs, hs].astype(F32)).astype(BF16)
    ya = [dot(ygdn_ref[rs, :], wa_ref[...]) for rs in parts]
    yb = [dot(ysgu_ref[rs, :], wb_ref[...]) for rs in parts]
    merged = [(ga_ref[rs, :].astype(F32) * a + gb_ref[rs, :].astype(F32) * b_).astype(BF16)
              for rs, a, b_ in zip(parts, ya, yb)]
    mix = [dot(m, wo_ref[...]) for m in merged]
    h2s = []
    for rs, mx in zip(parts, mix):
        xm = x_ref[rs, :] + gate_ref[0] * mx
        xmid_ref[rs, :] = xm
        hn = xm * lax.rsqrt(jnp.mean(xm * xm, axis=-1, keepdims=True) + EPS) * nfg_ref[...]
        h2 = hn * (1.0 + scale_ref[0]) + shift_ref[0]
        h2_ref[rs, :] = h2.astype(BF16)
        h2s.append(h2)

    wr = wr_ref[...]
    w_hi = wr.astype(BF16)
    w_lo = (wr - w_hi.astype(F32)).astype(BF16)
    logits = []
    for h2 in h2s:
        h_hi = h2.astype(BF16)
        h_lo = (h2 - h_hi.astype(F32)).astype(BF16)
        logits.append(dot(h_hi, w_hi) + dot(h_lo, w_hi) + dot(h_hi, w_lo) + br_ref[...])

    for rs, lg in zip(parts, logits):
        lane = lax.broadcasted_iota(jnp.int32, lg.shape, 1).astype(F32)
        far = float(LANES)
        gl = jnp.where(lane < N_EXPERT_GROUPS, lg, NEG_BIG)
        gmax = jnp.max(gl, axis=-1, keepdims=True)
        p_g = 1.0 / jnp.sum(jnp.exp(gl - gmax), axis=-1, keepdims=True)
        grp = jnp.min(jnp.where(gl == gmax, lane, far), axis=-1, keepdims=True)
        lo = N_EXPERT_GROUPS + EXPERTS_PER_GROUP * grp
        in_grp = jnp.where(lane >= lo, jnp.where(lane < lo + EXPERTS_PER_GROUP, 1.0, 0.0), 0.0)
        el = jnp.where(in_grp > 0.0, lg, NEG_BIG)
        m1 = jnp.max(el, axis=-1, keepdims=True)
        i1 = jnp.min(jnp.where(el == m1, lane, far), axis=-1, keepdims=True)
        el2 = jnp.where(lane == i1, NEG_BIG, el)
        m2 = jnp.max(el2, axis=-1, keepdims=True)
        i2 = jnp.min(jnp.where(el2 == m2, lane, far), axis=-1, keepdims=True)
        t = jnp.exp(m2 - m1)
        w1 = p_g / (1.0 + t)
        w2 = w1 * t
        rgate_ref[rs, :] = jnp.where(lane == 0.0, grp,
                                     jnp.where(lane == i1, w1, 0.0) + jnp.where(lane == i2, w2, 0.0))


def _merge(o_f, o_b, p, x2, w_s, b_s, w_a, w_b, w_o, gng, gate3, nfg, shift3, scale3, w_r, b_r,
           lat_tile0, tiles_per_batch):
    t_lat = x2.shape[0]
    n_tiles = t_lat // ROW_TILE
    lat = lambda c: pl.BlockSpec((ROW_TILE, D_MODEL), lambda i: (i + lat_tile0, c))
    own = lambda: pl.BlockSpec((ROW_TILE, D_MODEL), lambda i: (i, 0))
    mat = lambda: pl.BlockSpec((D_MODEL, D_MODEL), lambda i: (0, 0))
    vec = lambda: pl.BlockSpec((1, D_MODEL), lambda i: (0, 0))
    per_b = lambda: pl.BlockSpec((1, 1, D_MODEL), lambda i: (i // tiles_per_batch, 0, 0))
    return pl.pallas_call(
        _merge_kernel,
        grid=(n_tiles,),
        in_specs=[lat(0), lat(0), lat(PCOL_Z), lat(PCOL_U), lat(PCOL_SV), lat(PCOL_GA), lat(PCOL_GB), own(),
                  pl.BlockSpec((SGU_GROUPS, SGU_CHUNK, SGU_CHUNK), lambda i: (0, 0, 0)),
                  pl.BlockSpec((SGU_CHUNK, D_MODEL), lambda i: (0, 0)),
                  mat(), mat(), mat(), pl.BlockSpec((1, HEAD_DIM), lambda i: (0, 0)),
                  per_b(), vec(), per_b(), per_b(),
                  pl.BlockSpec((D_MODEL, LANES), lambda i: (0, 0)),
                  pl.BlockSpec((1, LANES), lambda i: (0, 0))],
        out_specs=[own(), own(), pl.BlockSpec((ROW_TILE, LANES), lambda i: (i, 0))],
        out_shape=[jax.ShapeDtypeStruct((t_lat, D_MODEL), F32),
                   jax.ShapeDtypeStruct((t_lat, D_MODEL), BF16),
                   jax.ShapeDtypeStruct((t_lat, LANES), F32)],
        scratch_shapes=[pltpu.VMEM((ROW_TILE, D_MODEL), BF16), pltpu.VMEM((ROW_TILE, D_MODEL), BF16)],
        compiler_params=_params("arbitrary"),
        name="merge",
    )(o_f, o_b, p, p, p, p, p, x2, w_s, b_s, w_a, w_b, w_o, gng, gate3, nfg, shift3, scale3, w_r, b_r)


def _split3(x):
    hi = x.astype(BF16)
    r1 = x - hi.astype(F32)
    mid = r1.astype(BF16)
    return hi, mid, (r1 - mid.astype(F32)).astype(BF16)


def _moe_kernel(h_ref, rg_ref, w1_ref, w3_ref, w2_ref, xmid_ref, gate_ref, fng_ref, o_ref,
                before_ref, destc_ref, xs_ref, ys_ref, gs_ref, start_ref):
    i, g, hf = pl.program_id(0), pl.program_id(1), pl.program_id(2)
    tm = h_ref.shape[0]
    dot = lambda a, b: jnp.dot(a, b, preferred_element_type=F32)

    @pl.when((i == 0) & (g == 0) & (hf == 0))
    def _():
        r = lax.broadcasted_iota(jnp.int32, (tm, tm), 0)
        c = lax.broadcasted_iota(jnp.int32, (tm, tm), 1)
        before_ref[...] = jnp.where(r < c, 1.0, 0.0).astype(BF16)

    @pl.when((g == 0) & (hf == 0))
    def _():
        rg = rg_ref[...]
        sub = lax.broadcasted_iota(jnp.int32, (8, tm), 0).astype(F32)
        m_rows = jnp.where(rg.T[0:1, :] == sub, 1.0, 0.0)
        total_r = jnp.sum(m_rows, axis=1, keepdims=True)
        start_r = jnp.zeros_like(total_r)
        for gg in range(N_EXPERT_GROUPS - 1):
            start_r = start_r + jnp.where(sub[:, 0:1] > gg, total_r[gg:gg + 1, :], 0.0)
        rank_r = dot(m_rows.astype(BF16), before_ref[...])
        dest_r = jnp.sum(jnp.where(m_rows > 0.0, rank_r + start_r, 0.0), axis=0, keepdims=True)
        destc_ref[...] = jnp.broadcast_to(dest_r, (LANES, tm)).T
        acc = jnp.int32(0)
        for gg in range(N_EXPERT_GROUPS):
            start_ref[gg] = acc
            acc = acc + total_r[gg, 0].astype(jnp.int32)
        start_ref[N_EXPERT_GROUPS] = acc
        slot = lax.broadcasted_iota(jnp.int32, (tm, tm), 0).astype(F32)
        perm = jnp.where(dest_r == slot, 1.0, 0.0).astype(BF16)
        g_hi, g_mid, _ = _split3(rg)
        moved = dot(perm, jnp.concatenate([h_ref[...], g_hi, g_mid], axis=1))
        xs_ref[...] = moved[:, :D_MODEL].astype(BF16)
        gs_ref[...] = moved[:, D_MODEL:D_MODEL + LANES] + moved[:, D_MODEL + LANES:]
        ys_ref[...] = jnp.zeros_like(ys_ref)

    bs = MOE_BLOCK
    first = start_ref[g] // bs
    n_unit = (start_ref[g + 1] + (bs - 1)) // bs - first

    def experts(row0, size):
        rows = pl.ds(pl.multiple_of(row0, bs), size)
        x = xs_ref[rows, :]
        gs = gs_ref[rows, :]
        lane = lax.broadcasted_iota(jnp.int32, gs.shape, 1)
        y = ys_ref[rows, :]
        for e in range(MOE_EXPERTS_PER_STEP):
            col = N_EXPERT_GROUPS + g * EXPERTS_PER_GROUP + hf * MOE_EXPERTS_PER_STEP + e
            gate = jnp.sum(jnp.where(lane == col, gs, 0.0), axis=-1, keepdims=True)
            hid = _silu(dot(x, w1_ref[e])) * dot(x, w3_ref[e]) * gate
            y = y + dot(hid.astype(BF16), w2_ref[e])
        ys_ref[rows, :] = y

    big = MOE_BLOCK_MULTIPLES[0]

    def big_block(k, carry):
        experts((first + k * big) * bs, big * bs)
        return carry

    lax.fori_loop(0, n_unit // big, big_block, 0)
    done = first + (n_unit // big) * big
    for m in MOE_BLOCK_MULTIPLES[1:]:
        take = (n_unit & m) != 0

        @pl.when(take)
        def _(done=done, m=m):
            experts(done * bs, m * bs)

        done = done + jnp.where(take, m, 0)

    @pl.when((g == pl.num_programs(1) - 1) & (hf == pl.num_programs(2) - 1))
    def _():
        slot = lax.broadcasted_iota(jnp.int32, (tm, tm), 1).astype(F32)
        perm_t = jnp.where(destc_ref[:, 0:1] == slot, 1.0, 0.0).astype(BF16)
        xo = xmid_ref[...] + gate_ref[0] * dot(perm_t, ys_ref[...].astype(BF16))
        o_ref[...] = xo * lax.rsqrt(jnp.mean(xo * xo, axis=-1, keepdims=True) + EPS) * fng_ref[...]


def _moe(h2, rgate, w1, w3, w2, xmid, gate3, fng, tiles_per_batch):
    t_lat = h2.shape[0]
    n_tiles = t_lat // MOE_TILE
    eps = MOE_EXPERTS_PER_STEP
    halves = EXPERTS_PER_GROUP // eps
    row = lambda w: pl.BlockSpec((MOE_TILE, w), lambda i, g, hf: (i, 0))
    wspec = lambda a, b: pl.BlockSpec((eps, a, b), lambda i, g, hf: (g * halves + hf, 0, 0))
    return pl.pallas_call(
        _moe_kernel,
        grid=(n_tiles, N_EXPERT_GROUPS, halves),
        in_specs=[row(D_MODEL), row(LANES),
                  wspec(D_MODEL, D_EXPERT), wspec(D_MODEL, D_EXPERT), wspec(D_EXPERT, D_MODEL),
                  row(D_MODEL),
                  pl.BlockSpec((1, 1, D_MODEL), lambda i, g, hf: (i // tiles_per_batch, 0, 0)),
                  pl.BlockSpec((1, D_MODEL), lambda i, g, hf: (0, 0))],
        out_specs=row(D_MODEL),
        out_shape=jax.ShapeDtypeStruct((t_lat, D_MODEL), F32),
        scratch_shapes=[pltpu.VMEM((MOE_TILE, MOE_TILE), BF16),
                        pltpu.VMEM((MOE_TILE, LANES), F32),
                        pltpu.VMEM((MOE_TILE, D_MODEL), BF16),
                        pltpu.VMEM((MOE_TILE, D_MODEL), F32),
                        pltpu.VMEM((MOE_TILE, LANES), F32),
                        pltpu.SMEM((N_EXPERT_GROUPS + 1,), jnp.int32)],
        compiler_params=pltpu.CompilerParams(dimension_semantics=("arbitrary",) * 3,
                                             vmem_limit_bytes=MOE_VMEM_LIMIT),
        name="moe",
    )(h2, rgate, w1, w3, w2, xmid, gate3, fng)


def kernel(x, c, ctx, c_ctx, ada_w, ada_b, norm_mix_g, w_in, conv_w, a_log, dt_bias, gdn_norm_g, sgu_ln_g, sgu_ln_b, sgu_w, sgu_b, w_branch_a, w_branch_b, w_out, norm_ffn_g, router_group_w, router_group_b, router_expert_w, router_expert_b, expert_w1, expert_w3, expert_w2, final_norm_g):
    batch, seq, d = x.shape
    ctx_len = ctx.shape[1]
    assert d == D_MODEL and ada_w.shape[0] == 1, "single-layer block with D_MODEL channels"
    assert batch * ctx_len == ROW_TILE, "context rows of all samples form one row tile"
    assert seq % MOE_TILE == 0 and ctx_len % PREP_TILE == 0 and batch + 1 <= 8
    t_lat = batch * seq
    row = lambda v: v.reshape(1, -1).astype(F32)

    cond = jnp.zeros((8, d), F32).at[:batch].set(c).at[batch].set(c_ctx)
    mod = _adaln(cond, ada_w[0], ada_b[0]).reshape(8, N_MOD, d)
    mod_row = lambda k: mod[:batch + 1, k].reshape(batch + 1, 1, d)

    w_head, w_tail, w_small = _wsplit(w_in[0])
    x2 = x.reshape(t_lat, d)
    tiles_per_batch = seq // ROW_TILE
    p, small = _inproj(x2, ctx.reshape(batch * ctx_len, d), row(norm_mix_g), mod_row(0), mod_row(1),
                       w_head, w_tail, w_small, row(sgu_ln_g), row(sgu_ln_b), tiles_per_batch)

    ctx_t, lat_t = ctx_len // PREP_TILE, seq // PREP_TILE
    starts = [b * ctx_t for b in range(batch)] + [batch * ctx_t + b * lat_t for b in range(batch)]
    ends = [(b + 1) * ctx_t - 1 for b in range(batch)] + [batch * ctx_t + (b + 1) * lat_t - 1 for b in range(batch)]
    conv_w8 = jnp.zeros((8, 3 * D_GDN), F32).at[:CONV_K].set(conv_w[0])
    pad_lanes = lambda v: jnp.zeros((1, LANES), F32).at[0, 2 * N_HEADS:4 * N_HEADS].set(v.reshape(-1))
    qkv, gb = _prep(p, small, conv_w8, pad_lanes(-jnp.exp(a_log[0])), pad_lanes(dt_bias[0]),
                    tuple(starts), tuple(ends))

    step_rows = GDN_CHUNK * GDN_STEP_CHUNKS
    assert ctx_len % step_rows == 0 and seq % step_rows == 0
    o_f, o_b = _gdn(qkv, gb, batch, ctx_len // step_rows, seq // step_rows)

    b_full = jnp.repeat(sgu_b[0].T, SGU_GROUP_DIM, axis=1).astype(F32)

    w_r = jnp.zeros((d, LANES), F32).at[:, :N_EXPERT_GROUPS].set(router_group_w[0]) \
        .at[:, N_EXPERT_GROUPS:N_EXPERT_GROUPS + N_EXPERTS].set(router_expert_w[0])
    b_r = jnp.zeros((1, LANES), F32).at[0, :N_EXPERT_GROUPS].set(router_group_b[0]) \
        .at[0, N_EXPERT_GROUPS:N_EXPERT_GROUPS + N_EXPERTS].set(router_expert_b[0])
    gng = gdn_norm_g[0].reshape(1, HEAD_DIM).astype(F32)
    xmid, h2, rgate = _merge(o_f, o_b, p, x2, sgu_w[0].astype(BF16), b_full,
                             w_branch_a[0].astype(BF16), w_branch_b[0].astype(BF16),
                             w_out[0].astype(BF16), gng, mod_row(2)[:batch], row(norm_ffn_g),
                             mod_row(3)[:batch], mod_row(4)[:batch], w_r, b_r, 1, tiles_per_batch)

    out = _moe(h2, rgate, expert_w1[0].astype(BF16), expert_w3[0].astype(BF16), expert_w2[0].astype(BF16),
               xmid, mod_row(5)[:batch], row(final_norm_g), seq // MOE_TILE)
    return out.reshape(batch, seq, d)
```

```python
import functools
import math

import jax
import jax.numpy as jnp
from jax import lax
from jax.experimental import pallas as pl
from jax.experimental.pallas import tpu as pltpu

F32 = jnp.float32
BF16 = jnp.bfloat16

D_MODEL = 1024
N_HEADS = 8
HEAD_DIM = 128
D_GDN = N_HEADS * HEAD_DIM
CONV_K = 5
GDN_CHUNK = 64
GDN_STEP_CHUNKS = 4
SGU_GROUPS = 8
SGU_GROUP_DIM = 128
D_SGU = SGU_GROUPS * SGU_GROUP_DIM
SGU_CHUNK = 128
N_EXPERT_GROUPS = 4
EXPERTS_PER_GROUP = 8
N_EXPERTS = N_EXPERT_GROUPS * EXPERTS_PER_GROUP
D_EXPERT = 256
N_MOD = 6
EPS = 1e-6
COL_BETA = 3 * D_GDN
COL_Z = COL_BETA + 4 * N_HEADS

LANES = 128
ROW_TILE = 512
INPROJ_SEGS = 4
MERGE_PARTS = 2
PREP_TILE = 256
MOE_TILE = 1024
MOE_EXPERTS_PER_STEP = 4
MOE_BLOCK = 128
MOE_BLOCK_MULTIPLES = (4, 2, 1)
VMEM_LIMIT = 48 * 1024 * 1024
MOE_VMEM_LIMIT = 56 * 1024 * 1024
NEG_BIG = -1e30

PCOL_Q, PCOL_K, PCOL_V, PCOL_GB, PCOL_Z, PCOL_U, PCOL_SV, PCOL_GA = range(8)


def _mm(a, b):
    return jnp.dot(a.astype(BF16), b.astype(BF16), preferred_element_type=F32)


def _mm_nt(a, b):
    return lax.dot_general(a.astype(BF16), b.astype(BF16), (((1,), (1,)), ((), ())),
                           preferred_element_type=F32)


def _sigmoid(x):
    return 0.5 + 0.5 * jnp.tanh(0.5 * x)


def _silu(x):
    return x * _sigmoid(x)


def _gelu_tanh(x):
    return 0.5 * x * (1.0 + jnp.tanh(math.sqrt(2.0 / math.pi) * (x + 0.044715 * (x * x * x))))


def _params(*sem):
    return pltpu.CompilerParams(dimension_semantics=sem, vmem_limit_bytes=VMEM_LIMIT)


def _adaln_kernel(c_ref, w_ref, b_ref, o_ref):
    o_ref[...] = _mm(_silu(c_ref[...]), w_ref[...]) + b_ref[...]


def _adaln(cond, w, b):
    n = w.shape[1]
    tn = 1536
    return pl.pallas_call(
        _adaln_kernel,
        grid=(n // tn,),
        in_specs=[pl.BlockSpec((8, D_MODEL), lambda j: (0, 0)),
                  pl.BlockSpec((D_MODEL, tn), lambda j: (0, j)),
                  pl.BlockSpec((1, tn), lambda j: (0, j))],
        out_specs=pl.BlockSpec((8, tn), lambda j: (0, j)),
        out_shape=jax.ShapeDtypeStruct((8, n), F32),
        compiler_params=_params("arbitrary"),
        name="adaln",
    )(cond, w, b.reshape(1, n))


def _inproj_kernel(x_ref, ctx_ref, g_ref, shift_ref, scale_ref, wh_ref, wt_ref, ws_ref, lng_ref, lnb_ref,
                   p_ref, small_ref, h_ref):
    i = pl.program_id(0)
    j = pl.program_id(1)

    def norm_mod(xv):
        y = xv * lax.rsqrt(jnp.mean(xv * xv, axis=-1, keepdims=True) + EPS) * g_ref[...]
        return y * (1.0 + scale_ref[0]) + shift_ref[0]

    @pl.when(j == 0)
    def _():
        @pl.when(i == 0)
        def _():
            h_ref[...] = norm_mod(ctx_ref[...]).astype(BF16)

        @pl.when(i > 0)
        def _():
            h_ref[...] = norm_mod(x_ref[...]).astype(BF16)

        small_ref[...] = jnp.dot(h_ref[...], ws_ref[...], preferred_element_type=F32)

    def raw(a):
        return a

    def gelu_ln(a):
        a = _gelu_tanh(a)
        mu = jnp.mean(a, axis=-1, keepdims=True)
        ac = a - mu
        var = jnp.mean(ac * ac, axis=-1, keepdims=True)
        return ac * lax.rsqrt(var + EPS) * lng_ref[...] + lnb_ref[...]

    def project(w_ref, w_seg, p_seg, epilogue):
        a = jnp.dot(h_ref[...], w_ref[:, w_seg * D_MODEL:(w_seg + 1) * D_MODEL], preferred_element_type=F32)
        p_ref[:, p_seg * D_MODEL:(p_seg + 1) * D_MODEL] = epilogue(a).astype(BF16)

    @pl.when(j == 0)
    def _():
        for seg in range(3):
            project(wh_ref, seg, seg, raw)
        project(wt_ref, 0, 3, _sigmoid)

    @pl.when(j == 1)
    def _():
        for seg, epilogue in enumerate((_silu, _gelu_tanh, gelu_ln, _sigmoid)):
            project(wt_ref, seg, seg, epilogue)


def _inproj(x2, ctx2, norm_g, shift3, scale3, w_head, w_tail, w_small, ln_g, ln_b, tiles_per_batch):
    t_lat = x2.shape[0]
    n_lat = t_lat // ROW_TILE
    n_tiles = n_lat + 1
    tn = INPROJ_SEGS * D_MODEL
    assert w_head.shape[1] == tn - D_MODEL and w_tail.shape[1] == tn + D_MODEL
    n_batch = shift3.shape[0] - 1
    sel = lambda i: jnp.where(i == 0, n_batch, (jnp.maximum(i, 1) - 1) // tiles_per_batch)
    vec = lambda: pl.BlockSpec((1, D_MODEL), lambda i, j: (0, 0))
    once = pl.Buffered(1)
    return pl.pallas_call(
        _inproj_kernel,
        grid=(n_tiles, 2),
        in_specs=[pl.BlockSpec((ROW_TILE, D_MODEL), lambda i, j: (jnp.maximum(i, 1) - 1, 0)),
                  pl.BlockSpec((ROW_TILE, D_MODEL), lambda i, j: (0, 0), pipeline_mode=once),
                  vec(),
                  pl.BlockSpec((1, 1, D_MODEL), lambda i, j: (sel(i), 0, 0)),
                  pl.BlockSpec((1, 1, D_MODEL), lambda i, j: (sel(i), 0, 0)),
                  pl.BlockSpec((D_MODEL, tn - D_MODEL), lambda i, j: (0, 0), pipeline_mode=once),
                  pl.BlockSpec((D_MODEL, tn), lambda i, j: (0, 1 - j)),
                  pl.BlockSpec((D_MODEL, LANES), lambda i, j: (0, 0)),
                  vec(), vec()],
        out_specs=[pl.BlockSpec((ROW_TILE, tn), lambda i, j: (i, j)),
                   pl.BlockSpec((ROW_TILE, LANES), lambda i, j: (i, 0))],
        out_shape=[jax.ShapeDtypeStruct((n_tiles * ROW_TILE, 2 * tn), BF16),
                   jax.ShapeDtypeStruct((n_tiles * ROW_TILE, LANES), F32)],
        scratch_shapes=[pltpu.VMEM((ROW_TILE, D_MODEL), BF16)],
        compiler_params=_params("arbitrary", "arbitrary"),
        name="inproj",
    )(x2, ctx2, norm_g, shift3, scale3, w_head, w_tail, w_small, ln_g, ln_b)


def _prep_kernel(pm_ref, pp_ref, pn_ref, cw_ref, small_ref, nega_ref, dtb_ref, qkv_ref, gb_ref, shift_ref,
                 *, first_tiles, last_tiles):
    r = pl.program_id(0)
    tr = pm_ref.shape[0]
    is_first = functools.reduce(jnp.logical_or, [r == t for t in first_tiles])
    is_last = functools.reduce(jnp.logical_or, [r == t for t in last_tiles])
    keep_prev = jnp.where(is_first, 0.0, 1.0)
    keep_next = jnp.where(is_last, 0.0, 1.0)

    half = CONV_K // 2
    offsets = [o for o in range(-half, half + 1) if o != 0]

    @pl.when(r == 0)
    def _():
        i0 = lax.broadcasted_iota(jnp.int32, (tr, tr), 0)
        i1 = lax.broadcasted_iota(jnp.int32, (tr, tr), 1)
        for m, o in enumerate(offsets):
            shift_ref[m * tr:(m + 1) * tr, :] = jnp.where(i1 == i0 + o, 1.0, 0.0).astype(BF16)

    def conv_silu(cs):
        x = pm_ref[:, cs]
        tap = lambda o: cw_ref[half + o:half + o + 1, cs]
        shifted = jnp.dot(shift_ref[...], x, preferred_element_type=F32)
        acc = tap(0) * x.astype(F32)
        for m, o in enumerate(offsets):
            acc = acc + tap(o) * shifted[m * tr:(m + 1) * tr, :]
        prev = pp_ref[:, cs].astype(F32)[8:16, :] * keep_prev
        nxt = pn_ref[:, cs].astype(F32)[0:8, :] * keep_next
        sub = lax.broadcasted_iota(jnp.int32, prev.shape, 0)
        top = jnp.zeros_like(prev)
        bot = jnp.zeros_like(prev)
        for o in range(1, half + 1):
            top = top + tap(-o) * jnp.where(sub < o, pltpu.roll(prev, o, 0), 0.0)
            bot = bot + tap(o) * jnp.where(sub >= 8 - o, pltpu.roll(nxt, 8 - o, 0), 0.0)
        return _silu(jnp.concatenate([acc[0:8] + top, acc[8:tr - 8], acc[tr - 8:tr] + bot], axis=0))

    for j, scale in ((PCOL_Q, HEAD_DIM ** -0.5), (PCOL_K, 1.0)):
        y = conv_silu(slice(j * D_GDN, (j + 1) * D_GDN))
        for h in range(N_HEADS):
            yh = y[:, h * HEAD_DIM:(h + 1) * HEAD_DIM]
            inv = lax.rsqrt(jnp.sum(yh * yh, axis=-1, keepdims=True) + EPS) * scale
            qkv_ref[:, j * D_GDN + h * HEAD_DIM:j * D_GDN + (h + 1) * HEAD_DIM] = (yh * inv).astype(BF16)
    vs = slice(PCOL_V * D_GDN, (PCOL_V + 1) * D_GDN)
    qkv_ref[:, vs] = conv_silu(vs).astype(BF16)

    s = small_ref[...]
    lane = lax.broadcasted_iota(jnp.int32, s.shape, 1)
    beta = _sigmoid(s)
    z = s + dtb_ref[...]
    softplus = jnp.maximum(z, 0.0) + jnp.log(1.0 + jnp.exp(-jnp.abs(z)))
    g = nega_ref[...] * softplus
    gb_ref[...] = jnp.where(lane < 2 * N_HEADS, beta, jnp.where(lane < 4 * N_HEADS, g, 0.0))


def _prep(p, small, conv_w8, nega, dtb, first_tiles, last_tiles):
    ta = p.shape[0]
    n_tiles = ta // PREP_TILE
    sub = PREP_TILE // 16
    n_sub = ta // 16
    width = 3 * D_GDN
    kern = functools.partial(_prep_kernel, first_tiles=first_tiles, last_tiles=last_tiles)
    return pl.pallas_call(
        kern,
        grid=(n_tiles,),
        in_specs=[pl.BlockSpec((PREP_TILE, width), lambda r: (r, 0)),
                  pl.BlockSpec((16, width), lambda r: (jnp.maximum(r * sub - 1, 0), 0)),
                  pl.BlockSpec((16, width), lambda r: (jnp.minimum((r + 1) * sub, n_sub - 1), 0)),
                  pl.BlockSpec((8, width), lambda r: (0, 0)),
                  pl.BlockSpec((PREP_TILE, LANES), lambda r: (r, 0)),
                  pl.BlockSpec((1, LANES), lambda r: (0, 0)),
                  pl.BlockSpec((1, LANES), lambda r: (0, 0))],
        out_specs=[pl.BlockSpec((PREP_TILE, width), lambda r: (r, 0)),
                   pl.BlockSpec((PREP_TILE, LANES), lambda r: (r, 0))],
        out_shape=[jax.ShapeDtypeStruct((ta, width), BF16),
                   jax.ShapeDtypeStruct((ta, LANES), F32)],
        scratch_shapes=[pltpu.VMEM(((CONV_K - 1) * PREP_TILE, PREP_TILE), BF16)],
        compiler_params=_params("arbitrary"),
        name="prep",
    )(p, p, p, conv_w8, small, nega, dtb)


def _gdn_decays(d, gb):
    c = GDN_CHUNK
    row = lax.broadcasted_iota(jnp.int32, (c, c), 0)
    col = lax.broadcasted_iota(jnp.int32, (c, c), 1)
    incl = row >= col if d == 0 else row <= col
    lane = lax.broadcasted_iota(jnp.int32, gb.shape, 1)
    g_only = jnp.where(lane >= 2 * N_HEADS, jnp.where(lane < 4 * N_HEADS, gb, 0.0), 0.0)
    tri = jnp.where(incl, 1.0, 0.0).astype(BF16)
    g_hi = g_only.astype(BF16)
    g_r1 = g_only - g_hi.astype(F32)
    g_mid = g_r1.astype(BF16)
    g_lo = (g_r1 - g_mid.astype(F32)).astype(BF16)
    dot = lambda a, b: jnp.dot(a, b, preferred_element_type=F32)
    gcum = dot(tri, g_hi) + dot(tri, g_mid) + dot(tri, g_lo)
    g_end = gcum[c - 1:c, :] if d == 0 else gcum[0:1, :]
    return gcum, gcum.T, jnp.exp(gcum), jnp.exp(g_end - gcum), jnp.exp(g_end)


def _gdn_kernel(qf, kf, vf, gbf, qb, kb, vb, gbb, of_ref, ob_ref, s_ref):
    @pl.when(pl.program_id(1) == 0)
    def _():
        s_ref[...] = jnp.zeros_like(s_ref)

    c = GDN_CHUNK
    n_sub = qf.shape[0] // c
    row = lax.broadcasted_iota(jnp.int32, (c, c), 0)
    col = lax.broadcasted_iota(jnp.int32, (c, c), 1)
    eye = jnp.where(row == col, 1.0, 0.0)
    masks = ((row >= col, row > col), (row <= col, row < col))
    refs = ((qf, kf, vf, gbf, of_ref), (qb, kb, vb, gbb, ob_ref))

    def rows(d, sub):
        k = sub if d == 0 else n_sub - 1 - sub
        return slice(k * c, (k + 1) * c)

    scans = [(d, sub) for sub in range(n_sub) for d in range(2)]
    gbv = {ds: refs[ds[0]][3][rows(*ds), :] for ds in scans}
    dec = {ds: _gdn_decays(ds[0], gbv[ds]) for ds in scans}
    chains = [(d, sub, h) for d, sub in scans for h in range(N_HEADS)]

    st = []
    for d, sub, h in chains:
        q_ref, k_ref, v_ref, _, _ = refs[d]
        rs, hs = rows(d, sub), slice(h * HEAD_DIM, (h + 1) * HEAD_DIM)
        q, k, v = q_ref[rs, hs], k_ref[rs, hs], v_ref[rs, hs]
        both = _mm_nt(jnp.concatenate([q, k], axis=0), k)
        st.append(dict(q=q.astype(F32), k=k.astype(F32), v=v.astype(F32), qk=both[:c], kk=both[c:]))

    for (d, sub, h), e in zip(chains, st):
        cb = d * N_HEADS + h
        cg = 2 * N_HEADS + cb
        gcum, gcum_t, exp_g, exp_rest, exp_end = dec[d, sub]
        incl, strict = masks[d]
        e["beta"] = gbv[d, sub][:, cb:cb + 1]
        diff = gcum[:, cg:cg + 1] - gcum_t[cg:cg + 1, :]
        decay = jnp.where(incl, jnp.exp(jnp.minimum(diff, 0.0)), 0.0)
        e["y"] = jnp.where(strict, -(e["beta"] * e["kk"] * decay), 0.0)
        e["qk"] = e["qk"] * decay
        e["eg"] = exp_g[:, cg:cg + 1]
        e["er"] = exp_rest[:, cg:cg + 1]
        e["ee"] = exp_end[:, cg:cg + 1]

    levels = range(GDN_CHUNK.bit_length() - 1)
    sels = []
    for d in range(2):
        inner, outer = (col, row) if d == 0 else (row, col)
        sels.append([((outer >> lvl) & 1 == 1) & ((inner >> lvl) == (outer >> lvl) - 1) for lvl in levels])
    for (d, sub, h), e in zip(chains, st):
        e["t"] = eye + jnp.where(sels[d][0], e["y"], 0.0)
    def active_rows(d, b):
        return [(2 * p + 1 - d) * b for p in range(c // (2 * b))]

    def pick(x, starts, b):
        return jnp.concatenate([x[s:s + b] for s in starts], axis=0)

    def place(pieces, starts, b, base):
        out = []
        for blk in range(c // b):
            cur = None if base is None else base[blk * b:(blk + 1) * b]
            if blk * b in starts:
                piece = pieces[starts.index(blk * b) * b:(starts.index(blk * b) + 1) * b]
                cur = piece if cur is None else cur + piece
            out.append(jnp.zeros((b, c), F32) if cur is None else cur)
        return jnp.concatenate(out, axis=0)

    for lvl in levels[1:]:
        b = 1 << lvl
        if b % 8 == 0:
            for (d, sub, h), e in zip(chains, st):
                act = active_rows(d, b)
                yd = _mm(pick(jnp.where(sels[d][lvl], e["y"], 0.0), act, b), e["t"])
                e["yd"] = place(yd, act, b, None)
            for (d, sub, h), e in zip(chains, st):
                act = active_rows(d, b)
                e["t"] = place(_mm(pick(e["t"], act, b), e["yd"]), act, b, e["t"])
        else:
            for (d, sub, h), e in zip(chains, st):
                e["yd"] = _mm(jnp.where(sels[d][lvl], e["y"], 0.0), e["t"])
            for e in st:
                e["t"] = e["t"] + _mm(e["t"], e["yd"])

    for e in st:
        kb_ = e["k"] * e["beta"]
        uw = _mm(e["t"], jnp.concatenate([e["v"] * e["beta"], kb_ * e["eg"]], axis=-1))
        e["u"], e["w"] = uw[:, :HEAD_DIM], uw[:, HEAD_DIM:]

    state = {(d, h): s_ref[d, h] for d in range(2) for h in range(N_HEADS)}
    for step in range(n_sub):
        now = [(key, e) for key, e in zip(chains, st) if key[1] == step]
        for (d, sub, h), e in now:
            ws = _mm(jnp.concatenate([e["w"], e["q"] * e["eg"]], axis=0), state[d, h])
            e["v_new"] = e["u"] - ws[:c]
            e["o"] = ws[c:]
        for (d, sub, h), e in now:
            k_dec_t = (e["k"] * e["er"]).T
            out = _mm(jnp.concatenate([e["qk"], k_dec_t], axis=0), e["v_new"])
            refs[d][4][rows(d, sub), h * HEAD_DIM:(h + 1) * HEAD_DIM] = (e["o"] + out[:c]).astype(BF16)
            state[d, h] = state[d, h] * e["ee"] + out[c:]
    for (d, h), s in state.items():
        s_ref[d, h] = s


def _gdn(qkv, gb, batch, ctx_blocks, lat_blocks):
    ta = qkv.shape[0]
    rows = GDN_CHUNK * GDN_STEP_CHUNKS
    n_steps = ctx_blocks + lat_blocks
    lat0 = batch * ctx_blocks

    def fwd_blk(b, s):
        return jnp.where(s < ctx_blocks, b * ctx_blocks + s, lat0 + b * lat_blocks + (s - ctx_blocks))

    def bwd_blk(b, s):
        return jnp.where(s < ctx_blocks, b * ctx_blocks + (ctx_blocks - 1 - s),
                         lat0 + b * lat_blocks + (lat_blocks - 1 - (s - ctx_blocks)))

    def specs(blk):
        col = lambda j: pl.BlockSpec((rows, D_MODEL), lambda b, s: (blk(b, s), j))
        return [col(0), col(1), col(2), pl.BlockSpec((rows, LANES), lambda b, s: (blk(b, s), 0))]

    out = lambda blk: pl.BlockSpec((rows, D_MODEL), lambda b, s: (blk(b, s), 0))
    return pl.pallas_call(
        _gdn_kernel,
        grid=(batch, n_steps),
        in_specs=specs(fwd_blk) + specs(bwd_blk),
        out_specs=[out(fwd_blk), out(bwd_blk)],
        out_shape=[jax.ShapeDtypeStruct((ta, D_MODEL), BF16)] * 2,
        scratch_shapes=[pltpu.VMEM((2, N_HEADS, HEAD_DIM, HEAD_DIM), F32)],
        compiler_params=_params("arbitrary", "arbitrary"),
        name="gdn",
    )(qkv, qkv, qkv, gb, qkv, qkv, qkv, gb)


def _merge_kernel(of_ref, ob_ref, z_ref, su_ref, sv_ref, ga_ref, gb_ref, x_ref, ws_ref, bs_ref,
                  wa_ref, wb_ref, wo_ref, gng_ref, gate_ref, nfg_ref, shift_ref, scale_ref, wr_ref, br_ref,
                  xmid_ref, h2_ref, rgate_ref, ygdn_ref, ysgu_ref):
    tm = x_ref.shape[0]
    parts = [slice(k * tm // MERGE_PARTS, (k + 1) * tm // MERGE_PARTS) for k in range(MERGE_PARTS)]
    dot = lambda a, b: jnp.dot(a, b, preferred_element_type=F32)

    for ch in range(tm // SGU_CHUNK):
        rs = slice(ch * SGU_CHUNK, (ch + 1) * SGU_CHUNK)
        for g in range(SGU_GROUPS):
            cs = slice(g * SGU_GROUP_DIM, (g + 1) * SGU_GROUP_DIM)
            mixed = dot(ws_ref[g], sv_ref[rs, cs]) + bs_ref[:, cs]
            ysgu_ref[rs, cs] = (su_ref[rs, cs].astype(F32) * mixed).astype(BF16)

    for rs in parts:
        o = of_ref[rs, :].astype(F32) + ob_ref[rs, :].astype(F32)
        for h in range(N_HEADS):
            hs = slice(h * HEAD_DIM, (h + 1) * HEAD_DIM)
            oh = o[:, hs]
            inv = lax.rsqrt(jnp.mean(oh * oh, axis=-1, keepdims=True) + EPS)
            ygdn_ref[rs, hs] = (oh * inv * gng_ref[...] * z_ref[rs, hs].astype(F32)).astype(BF16)
    ya = [dot(ygdn_ref[rs, :], wa_ref[...]) for rs in parts]
    yb = [dot(ysgu_ref[rs, :], wb_ref[...]) for rs in parts]
    merged = [(ga_ref[rs, :].astype(F32) * a + gb_ref[rs, :].astype(F32) * b_).astype(BF16)
              for rs, a, b_ in zip(parts, ya, yb)]
    mix = [dot(m, wo_ref[...]) for m in merged]
    h2s = []
    for rs, mx in zip(parts, mix):
        xm = x_ref[rs, :] + gate_ref[0] * mx
        xmid_ref[rs, :] = xm
        hn = xm * lax.rsqrt(jnp.mean(xm * xm, axis=-1, keepdims=True) + EPS) * nfg_ref[...]
        h2 = hn * (1.0 + scale_ref[0]) + shift_ref[0]
        h2_ref[rs, :] = h2.astype(BF16)
        h2s.append(h2)

    wr = wr_ref[...]
    w_hi = wr.astype(BF16)
    w_lo = (wr - w_hi.astype(F32)).astype(BF16)
    logits = []
    for h2 in h2s:
        h_hi = h2.astype(BF16)
        h_lo = (h2 - h_hi.astype(F32)).astype(BF16)
        logits.append(dot(h_hi, w_hi) + dot(h_lo, w_hi) + dot(h_hi, w_lo) + br_ref[...])

    for rs, lg in zip(parts, logits):
        lane = lax.broadcasted_iota(jnp.int32, lg.shape, 1).astype(F32)
        far = float(LANES)
        gl = jnp.where(lane < N_EXPERT_GROUPS, lg, NEG_BIG)
        gmax = jnp.max(gl, axis=-1, keepdims=True)
        p_g = 1.0 / jnp.sum(jnp.exp(gl - gmax), axis=-1, keepdims=True)
        grp = jnp.min(jnp.where(gl == gmax, lane, far), axis=-1, keepdims=True)
        lo = N_EXPERT_GROUPS + EXPERTS_PER_GROUP * grp
        in_grp = jnp.where(lane >= lo, jnp.where(lane < lo + EXPERTS_PER_GROUP, 1.0, 0.0), 0.0)
        el = jnp.where(in_grp > 0.0, lg, NEG_BIG)
        m1 = jnp.max(el, axis=-1, keepdims=True)
        i1 = jnp.min(jnp.where(el == m1, lane, far), axis=-1, keepdims=True)
        el2 = jnp.where(lane == i1, NEG_BIG, el)
        m2 = jnp.max(el2, axis=-1, keepdims=True)
        i2 = jnp.min(jnp.where(el2 == m2, lane, far), axis=-1, keepdims=True)
        t = jnp.exp(m2 - m1)
        w1 = p_g / (1.0 + t)
        w2 = w1 * t
        rgate_ref[rs, :] = jnp.where(lane == 0.0, grp,
                                     jnp.where(lane == i1, w1, 0.0) + jnp.where(lane == i2, w2, 0.0))


def _merge(o_f, o_b, p, x2, w_s, b_s, w_a, w_b, w_o, gng, gate3, nfg, shift3, scale3, w_r, b_r,
           lat_tile0, tiles_per_batch):
    t_lat = x2.shape[0]
    n_tiles = t_lat // ROW_TILE
    lat = lambda c: pl.BlockSpec((ROW_TILE, D_MODEL), lambda i: (i + lat_tile0, c))
    own = lambda: pl.BlockSpec((ROW_TILE, D_MODEL), lambda i: (i, 0))
    mat = lambda: pl.BlockSpec((D_MODEL, D_MODEL), lambda i: (0, 0))
    vec = lambda: pl.BlockSpec((1, D_MODEL), lambda i: (0, 0))
    per_b = lambda: pl.BlockSpec((1, 1, D_MODEL), lambda i: (i // tiles_per_batch, 0, 0))
    return pl.pallas_call(
        _merge_kernel,
        grid=(n_tiles,),
        in_specs=[lat(0), lat(0), lat(PCOL_Z), lat(PCOL_U), lat(PCOL_SV), lat(PCOL_GA), lat(PCOL_GB), own(),
                  pl.BlockSpec((SGU_GROUPS, SGU_CHUNK, SGU_CHUNK), lambda i: (0, 0, 0)),
                  pl.BlockSpec((SGU_CHUNK, D_MODEL), lambda i: (0, 0)),
                  mat(), mat(), mat(), pl.BlockSpec((1, HEAD_DIM), lambda i: (0, 0)),
                  per_b(), vec(), per_b(), per_b(),
                  pl.BlockSpec((D_MODEL, LANES), lambda i: (0, 0)),
                  pl.BlockSpec((1, LANES), lambda i: (0, 0))],
        out_specs=[own(), own(), pl.BlockSpec((ROW_TILE, LANES), lambda i: (i, 0))],
        out_shape=[jax.ShapeDtypeStruct((t_lat, D_MODEL), F32),
                   jax.ShapeDtypeStruct((t_lat, D_MODEL), BF16),
                   jax.ShapeDtypeStruct((t_lat, LANES), F32)],
        scratch_shapes=[pltpu.VMEM((ROW_TILE, D_MODEL), BF16), pltpu.VMEM((ROW_TILE, D_MODEL), BF16)],
        compiler_params=_params("arbitrary"),
        name="merge",
    )(o_f, o_b, p, p, p, p, p, x2, w_s, b_s, w_a, w_b, w_o, gng, gate3, nfg, shift3, scale3, w_r, b_r)


def _split3(x):
    hi = x.astype(BF16)
    r1 = x - hi.astype(F32)
    mid = r1.astype(BF16)
    return hi, mid, (r1 - mid.astype(F32)).astype(BF16)


def _moe_kernel(h_ref, rg_ref, w1_ref, w3_ref, w2_ref, xmid_ref, gate_ref, fng_ref, o_ref,
                before_ref, destc_ref, xs_ref, ys_ref, gs_ref, start_ref):
    i, g, hf = pl.program_id(0), pl.program_id(1), pl.program_id(2)
    tm = h_ref.shape[0]
    dot = lambda a, b: jnp.dot(a, b, preferred_element_type=F32)

    @pl.when((i == 0) & (g == 0) & (hf == 0))
    def _():
        r = lax.broadcasted_iota(jnp.int32, (tm, tm), 0)
        c = lax.broadcasted_iota(jnp.int32, (tm, tm), 1)
        before_ref[...] = jnp.where(r < c, 1.0, 0.0).astype(BF16)

    @pl.when((g == 0) & (hf == 0))
    def _():
        rg = rg_ref[...]
        sub = lax.broadcasted_iota(jnp.int32, (8, tm), 0).astype(F32)
        m_rows = jnp.where(rg.T[0:1, :] == sub, 1.0, 0.0)
        total_r = jnp.sum(m_rows, axis=1, keepdims=True)
        start_r = jnp.zeros_like(total_r)
        for gg in range(N_EXPERT_GROUPS - 1):
            start_r = start_r + jnp.where(sub[:, 0:1] > gg, total_r[gg:gg + 1, :], 0.0)
        rank_r = dot(m_rows.astype(BF16), before_ref[...])
        dest_r = jnp.sum(jnp.where(m_rows > 0.0, rank_r + start_r, 0.0), axis=0, keepdims=True)
        destc_ref[...] = jnp.broadcast_to(dest_r, (LANES, tm)).T
        acc = jnp.int32(0)
        for gg in range(N_EXPERT_GROUPS):
            start_ref[gg] = acc
            acc = acc + total_r[gg, 0].astype(jnp.int32)
        start_ref[N_EXPERT_GROUPS] = acc
        slot = lax.broadcasted_iota(jnp.int32, (tm, tm), 0).astype(F32)
        perm = jnp.where(dest_r == slot, 1.0, 0.0).astype(BF16)
        g_hi, g_mid, _ = _split3(rg)
        moved = dot(perm, jnp.concatenate([h_ref[...], g_hi, g_mid], axis=1))
        xs_ref[...] = moved[:, :D_MODEL].astype(BF16)
        gs_ref[...] = moved[:, D_MODEL:D_MODEL + LANES] + moved[:, D_MODEL + LANES:]
        ys_ref[...] = jnp.zeros_like(ys_ref)

    bs = MOE_BLOCK
    first = start_ref[g] // bs
    n_unit = (start_ref[g + 1] + (bs - 1)) // bs - first

    def experts(row0, size):
        rows = pl.ds(pl.multiple_of(row0, bs), size)
        x = xs_ref[rows, :]
        gs = gs_ref[rows, :]
        lane = lax.broadcasted_iota(jnp.int32, gs.shape, 1)
        y = ys_ref[rows, :]
        ups = [(dot(x, w1_ref[e]), dot(x, w3_ref[e])) for e in range(MOE_EXPERTS_PER_STEP)]
        hids = []
        for e, (a, b) in enumerate(ups):
            col = N_EXPERT_GROUPS + g * EXPERTS_PER_GROUP + hf * MOE_EXPERTS_PER_STEP + e
            gate = jnp.sum(jnp.where(lane == col, gs, 0.0), axis=-1, keepdims=True)
            hids.append((_silu(a) * b * gate).astype(BF16))
        for e, hid in enumerate(hids):
            y = y + dot(hid, w2_ref[e])
        ys_ref[rows, :] = y

    big = MOE_BLOCK_MULTIPLES[0]

    def big_block(k, carry):
        experts((first + k * big) * bs, big * bs)
        return carry

    lax.fori_loop(0, n_unit // big, big_block, 0)
    done = first + (n_unit // big) * big
    for m in MOE_BLOCK_MULTIPLES[1:]:
        take = (n_unit & m) != 0

        @pl.when(take)
        def _(done=done, m=m):
            experts(done * bs, m * bs)

        done = done + jnp.where(take, m, 0)

    @pl.when((g == pl.num_programs(1) - 1) & (hf == pl.num_programs(2) - 1))
    def _():
        slot = lax.broadcasted_iota(jnp.int32, (tm, tm), 1).astype(F32)
        perm_t = jnp.where(destc_ref[:, 0:1] == slot, 1.0, 0.0).astype(BF16)
        xo = xmid_ref[...] + gate_ref[0] * dot(perm_t, ys_ref[...].astype(BF16))
        o_ref[...] = xo * lax.rsqrt(jnp.mean(xo * xo, axis=-1, keepdims=True) + EPS) * fng_ref[...]


def _moe(h2, rgate, w1, w3, w2, xmid, gate3, fng, tiles_per_batch):
    t_lat = h2.shape[0]
    n_tiles = t_lat // MOE_TILE
    eps = MOE_EXPERTS_PER_STEP
    halves = EXPERTS_PER_GROUP // eps
    row = lambda w: pl.BlockSpec((MOE_TILE, w), lambda i, g, hf: (i, 0))
    wspec = lambda a, b: pl.BlockSpec((eps, a, b), lambda i, g, hf: (g * halves + hf, 0, 0))
    return pl.pallas_call(
        _moe_kernel,
        grid=(n_tiles, N_EXPERT_GROUPS, halves),
        in_specs=[row(D_MODEL), row(LANES),
                  wspec(D_MODEL, D_EXPERT), wspec(D_MODEL, D_EXPERT), wspec(D_EXPERT, D_MODEL),
                  row(D_MODEL),
                  pl.BlockSpec((1, 1, D_MODEL), lambda i, g, hf: (i // tiles_per_batch, 0, 0)),
                  pl.BlockSpec((1, D_MODEL), lambda i, g, hf: (0, 0))],
        out_specs=row(D_MODEL),
        out_shape=jax.ShapeDtypeStruct((t_lat, D_MODEL), F32),
        scratch_shapes=[pltpu.VMEM((MOE_TILE, MOE_TILE), BF16),
                        pltpu.VMEM((MOE_TILE, LANES), F32),
                        pltpu.VMEM((MOE_TILE, D_MODEL), BF16),
                        pltpu.VMEM((MOE_TILE, D_MODEL), F32),
                        pltpu.VMEM((MOE_TILE, LANES), F32),
                        pltpu.SMEM((N_EXPERT_GROUPS + 1,), jnp.int32)],
        compiler_params=pltpu.CompilerParams(dimension_semantics=("arbitrary",) * 3,
                                             vmem_limit_bytes=MOE_VMEM_LIMIT),
        name="moe",
    )(h2, rgate, w1, w3, w2, xmid, gate3, fng)


def kernel(x, c, ctx, c_ctx, ada_w, ada_b, norm_mix_g, w_in, conv_w, a_log, dt_bias, gdn_norm_g, sgu_ln_g, sgu_ln_b, sgu_w, sgu_b, w_branch_a, w_branch_b, w_out, norm_ffn_g, router_group_w, router_group_b, router_expert_w, router_expert_b, expert_w1, expert_w3, expert_w2, final_norm_g):
    batch, seq, d = x.shape
    ctx_len = ctx.shape[1]
    assert d == D_MODEL and ada_w.shape[0] == 1, "single-layer block with D_MODEL channels"
    assert batch * ctx_len == ROW_TILE, "context rows of all samples form one row tile"
    assert seq % MOE_TILE == 0 and ctx_len % PREP_TILE == 0 and batch + 1 <= 8
    t_lat = batch * seq
    row = lambda v: v.reshape(1, -1).astype(F32)

    cond = jnp.zeros((8, d), F32).at[:batch].set(c).at[batch].set(c_ctx)
    mod = _adaln(cond, ada_w[0], ada_b[0]).reshape(8, N_MOD, d)
    mod_row = lambda k: mod[:batch + 1, k].reshape(batch + 1, 1, d)

    w_l = w_in[0]
    w_head = w_l[:, :COL_BETA].astype(BF16)
    w_tail = w_l[:, COL_Z:].astype(BF16)
    w_small = jnp.zeros((d, LANES), BF16).at[:, :4 * N_HEADS].set(w_l[:, COL_BETA:COL_Z].astype(BF16))
    x2 = x.reshape(t_lat, d)
    tiles_per_batch = seq // ROW_TILE
    p, small = _inproj(x2, ctx.reshape(batch * ctx_len, d), row(norm_mix_g), mod_row(0), mod_row(1),
                       w_head, w_tail, w_small, row(sgu_ln_g), row(sgu_ln_b), tiles_per_batch)

    ctx_t, lat_t = ctx_len // PREP_TILE, seq // PREP_TILE
    starts = [b * ctx_t for b in range(batch)] + [batch * ctx_t + b * lat_t for b in range(batch)]
    ends = [(b + 1) * ctx_t - 1 for b in range(batch)] + [batch * ctx_t + (b + 1) * lat_t - 1 for b in range(batch)]
    conv_w8 = jnp.zeros((8, 3 * D_GDN), F32).at[:CONV_K].set(conv_w[0])
    pad_lanes = lambda v: jnp.zeros((1, LANES), F32).at[0, 2 * N_HEADS:4 * N_HEADS].set(v.reshape(-1))
    qkv, gb = _prep(p, small, conv_w8, pad_lanes(-jnp.exp(a_log[0])), pad_lanes(dt_bias[0]),
                    tuple(starts), tuple(ends))

    step_rows = GDN_CHUNK * GDN_STEP_CHUNKS
    assert ctx_len % step_rows == 0 and seq % step_rows == 0
    o_f, o_b = _gdn(qkv, gb, batch, ctx_len // step_rows, seq // step_rows)

    b_full = jnp.repeat(sgu_b[0].T, SGU_GROUP_DIM, axis=1).astype(F32)

    w_r = jnp.zeros((d, LANES), F32).at[:, :N_EXPERT_GROUPS].set(router_group_w[0]) \
        .at[:, N_EXPERT_GROUPS:N_EXPERT_GROUPS + N_EXPERTS].set(router_expert_w[0])
    b_r = jnp.zeros((1, LANES), F32).at[0, :N_EXPERT_GROUPS].set(router_group_b[0]) \
        .at[0, N_EXPERT_GROUPS:N_EXPERT_GROUPS + N_EXPERTS].set(router_expert_b[0])
    gng = gdn_norm_g[0].reshape(1, HEAD_DIM).astype(F32)
    xmid, h2, rgate = _merge(o_f, o_b, p, x2, sgu_w[0].astype(BF16), b_full,
                             w_branch_a[0].astype(BF16), w_branch_b[0].astype(BF16),
                             w_out[0].astype(BF16), gng, mod_row(2)[:batch], row(norm_ffn_g),
                             mod_row(3)[:batch], mod_row(4)[:batch], w_r, b_r, 1, tiles_per_batch)

    out = _moe(h2, rgate, expert_w1[0].astype(BF16), expert_w3[0].astype(BF16), expert_w2[0].astype(BF16),
               xmid, mod_row(5)[:batch], row(final_norm_g), seq // MOE_TILE)
    return out.reshape(batch, seq, d)
```

```python
import functools
import math

import jax
import jax.numpy as jnp
from jax import lax
from jax.experimental import pallas as pl
from jax.experimental.pallas import tpu as pltpu

F32 = jnp.float32
BF16 = jnp.bfloat16

D_MODEL = 1024
N_HEADS = 8
HEAD_DIM = 128
D_GDN = N_HEADS * HEAD_DIM
CONV_K = 5
GDN_CHUNK = 64
GDN_STEP_CHUNKS = 4
SGU_GROUPS = 8
SGU_GROUP_DIM = 128
D_SGU = SGU_GROUPS * SGU_GROUP_DIM
SGU_CHUNK = 128
N_EXPERT_GROUPS = 4
EXPERTS_PER_GROUP = 8
N_EXPERTS = N_EXPERT_GROUPS * EXPERTS_PER_GROUP
D_EXPERT = 256
N_MOD = 6
EPS = 1e-6
COL_BETA = 3 * D_GDN
COL_Z = COL_BETA + 4 * N_HEADS

LANES = 128
ROW_TILE = 512
INPROJ_SEGS = 4
MERGE_PARTS = 2
PREP_TILE = 256
MOE_TILE = 1024
MOE_EXPERTS_PER_STEP = 4
MOE_BLOCK = 64
MOE_BLOCK_MULTIPLES = (8, 4, 2, 1)
VMEM_LIMIT = 48 * 1024 * 1024
MOE_VMEM_LIMIT = 56 * 1024 * 1024
NEG_BIG = -1e30

PCOL_Q, PCOL_K, PCOL_V, PCOL_GB, PCOL_Z, PCOL_U, PCOL_SV, PCOL_GA = range(8)


def _mm(a, b):
    return jnp.dot(a.astype(BF16), b.astype(BF16), preferred_element_type=F32)


def _mm_nt(a, b):
    return lax.dot_general(a.astype(BF16), b.astype(BF16), (((1,), (1,)), ((), ())),
                           preferred_element_type=F32)


def _sigmoid(x):
    return 0.5 + 0.5 * jnp.tanh(0.5 * x)


def _silu(x):
    return x * _sigmoid(x)


def _gelu_tanh(x):
    return 0.5 * x * (1.0 + jnp.tanh(math.sqrt(2.0 / math.pi) * (x + 0.044715 * (x * x * x))))


def _params(*sem):
    return pltpu.CompilerParams(dimension_semantics=sem, vmem_limit_bytes=VMEM_LIMIT)


def _adaln_kernel(c_ref, w_ref, b_ref, o_ref):
    o_ref[...] = _mm(_silu(c_ref[...]), w_ref[...]) + b_ref[...]


def _adaln(cond, w, b):
    n = w.shape[1]
    tn = 1536
    return pl.pallas_call(
        _adaln_kernel,
        grid=(n // tn,),
        in_specs=[pl.BlockSpec((8, D_MODEL), lambda j: (0, 0)),
                  pl.BlockSpec((D_MODEL, tn), lambda j: (0, j)),
                  pl.BlockSpec((1, tn), lambda j: (0, j))],
        out_specs=pl.BlockSpec((8, tn), lambda j: (0, j)),
        out_shape=jax.ShapeDtypeStruct((8, n), F32),
        compiler_params=_params("arbitrary"),
        name="adaln",
    )(cond, w, b.reshape(1, n))


def _inproj_kernel(x_ref, ctx_ref, g_ref, shift_ref, scale_ref, wh_ref, wt_ref, ws_ref, lng_ref, lnb_ref,
                   p_ref, small_ref, h_ref):
    i = pl.program_id(0)
    j = pl.program_id(1)

    def norm_mod(xv):
        y = xv * lax.rsqrt(jnp.mean(xv * xv, axis=-1, keepdims=True) + EPS) * g_ref[...]
        return y * (1.0 + scale_ref[0]) + shift_ref[0]

    @pl.when(j == 0)
    def _():
        @pl.when(i == 0)
        def _():
            h_ref[...] = norm_mod(ctx_ref[...]).astype(BF16)

        @pl.when(i > 0)
        def _():
            h_ref[...] = norm_mod(x_ref[...]).astype(BF16)

        small_ref[...] = jnp.dot(h_ref[...], ws_ref[...], preferred_element_type=F32)

    def raw(a):
        return a

    def gelu_ln(a):
        a = _gelu_tanh(a)
        mu = jnp.mean(a, axis=-1, keepdims=True)
        ac = a - mu
        var = jnp.mean(ac * ac, axis=-1, keepdims=True)
        return ac * lax.rsqrt(var + EPS) * lng_ref[...] + lnb_ref[...]

    def project(w_ref, w_seg, p_seg, epilogue):
        a = jnp.dot(h_ref[...], w_ref[:, w_seg * D_MODEL:(w_seg + 1) * D_MODEL], preferred_element_type=F32)
        p_ref[:, p_seg * D_MODEL:(p_seg + 1) * D_MODEL] = epilogue(a).astype(BF16)

    @pl.when(j == 0)
    def _():
        for seg in range(3):
            project(wh_ref, seg, seg, raw)
        project(wt_ref, 0, 3, _sigmoid)

    @pl.when(j == 1)
    def _():
        for seg, epilogue in enumerate((_silu, _gelu_tanh, gelu_ln, _sigmoid)):
            project(wt_ref, seg, seg, epilogue)


def _inproj(x2, ctx2, norm_g, shift3, scale3, w_head, w_tail, w_small, ln_g, ln_b, tiles_per_batch):
    t_lat = x2.shape[0]
    n_lat = t_lat // ROW_TILE
    n_tiles = n_lat + 1
    tn = INPROJ_SEGS * D_MODEL
    assert w_head.shape[1] == tn - D_MODEL and w_tail.shape[1] == tn + D_MODEL
    n_batch = shift3.shape[0] - 1
    sel = lambda i: jnp.where(i == 0, n_batch, (jnp.maximum(i, 1) - 1) // tiles_per_batch)
    vec = lambda: pl.BlockSpec((1, D_MODEL), lambda i, j: (0, 0))
    once = pl.Buffered(1)
    return pl.pallas_call(
        _inproj_kernel,
        grid=(n_tiles, 2),
        in_specs=[pl.BlockSpec((ROW_TILE, D_MODEL), lambda i, j: (jnp.maximum(i, 1) - 1, 0)),
                  pl.BlockSpec((ROW_TILE, D_MODEL), lambda i, j: (0, 0), pipeline_mode=once),
                  vec(),
                  pl.BlockSpec((1, 1, D_MODEL), lambda i, j: (sel(i), 0, 0)),
                  pl.BlockSpec((1, 1, D_MODEL), lambda i, j: (sel(i), 0, 0)),
                  pl.BlockSpec((D_MODEL, tn - D_MODEL), lambda i, j: (0, 0), pipeline_mode=once),
                  pl.BlockSpec((D_MODEL, tn), lambda i, j: (0, 1 - j)),
                  pl.BlockSpec((D_MODEL, LANES), lambda i, j: (0, 0)),
                  vec(), vec()],
        out_specs=[pl.BlockSpec((ROW_TILE, tn), lambda i, j: (i, j)),
                   pl.BlockSpec((ROW_TILE, LANES), lambda i, j: (i, 0))],
        out_shape=[jax.ShapeDtypeStruct((n_tiles * ROW_TILE, 2 * tn), BF16),
                   jax.ShapeDtypeStruct((n_tiles * ROW_TILE, LANES), F32)],
        scratch_shapes=[pltpu.VMEM((ROW_TILE, D_MODEL), BF16)],
        compiler_params=_params("arbitrary", "arbitrary"),
        name="inproj",
    )(x2, ctx2, norm_g, shift3, scale3, w_head, w_tail, w_small, ln_g, ln_b)


def _prep_kernel(pm_ref, pp_ref, pn_ref, cw_ref, small_ref, nega_ref, dtb_ref, qkv_ref, gb_ref, shift_ref,
                 *, first_tiles, last_tiles):
    r = pl.program_id(0)
    tr = pm_ref.shape[0]
    is_first = functools.reduce(jnp.logical_or, [r == t for t in first_tiles])
    is_last = functools.reduce(jnp.logical_or, [r == t for t in last_tiles])
    keep_prev = jnp.where(is_first, 0.0, 1.0)
    keep_next = jnp.where(is_last, 0.0, 1.0)

    half = CONV_K // 2
    offsets = [o for o in range(-half, half + 1) if o != 0]

    @pl.when(r == 0)
    def _():
        i0 = lax.broadcasted_iota(jnp.int32, (tr, tr), 0)
        i1 = lax.broadcasted_iota(jnp.int32, (tr, tr), 1)
        for m, o in enumerate(offsets):
            shift_ref[m * tr:(m + 1) * tr, :] = jnp.where(i1 == i0 + o, 1.0, 0.0).astype(BF16)

    def conv_silu(cs):
        x = pm_ref[:, cs]
        tap = lambda o: cw_ref[half + o:half + o + 1, cs]
        shifted = jnp.dot(shift_ref[...], x, preferred_element_type=F32)
        acc = tap(0) * x.astype(F32)
        for m, o in enumerate(offsets):
            acc = acc + tap(o) * shifted[m * tr:(m + 1) * tr, :]
        prev = pp_ref[:, cs].astype(F32)[8:16, :] * keep_prev
        nxt = pn_ref[:, cs].astype(F32)[0:8, :] * keep_next
        sub = lax.broadcasted_iota(jnp.int32, prev.shape, 0)
        top = jnp.zeros_like(prev)
        bot = jnp.zeros_like(prev)
        for o in range(1, half + 1):
            top = top + tap(-o) * jnp.where(sub < o, pltpu.roll(prev, o, 0), 0.0)
            bot = bot + tap(o) * jnp.where(sub >= 8 - o, pltpu.roll(nxt, 8 - o, 0), 0.0)
        return _silu(jnp.concatenate([acc[0:8] + top, acc[8:tr - 8], acc[tr - 8:tr] + bot], axis=0))

    for j, scale in ((PCOL_Q, HEAD_DIM ** -0.5), (PCOL_K, 1.0)):
        y = conv_silu(slice(j * D_GDN, (j + 1) * D_GDN))
        for h in range(N_HEADS):
            yh = y[:, h * HEAD_DIM:(h + 1) * HEAD_DIM]
            inv = lax.rsqrt(jnp.sum(yh * yh, axis=-1, keepdims=True) + EPS) * scale
            qkv_ref[:, j * D_GDN + h * HEAD_DIM:j * D_GDN + (h + 1) * HEAD_DIM] = (yh * inv).astype(BF16)
    vs = slice(PCOL_V * D_GDN, (PCOL_V + 1) * D_GDN)
    qkv_ref[:, vs] = conv_silu(vs).astype(BF16)

    s = small_ref[...]
    lane = lax.broadcasted_iota(jnp.int32, s.shape, 1)
    beta = _sigmoid(s)
    z = s + dtb_ref[...]
    softplus = jnp.maximum(z, 0.0) + jnp.log(1.0 + jnp.exp(-jnp.abs(z)))
    g = nega_ref[...] * softplus
    gb_ref[...] = jnp.where(lane < 2 * N_HEADS, beta, jnp.where(lane < 4 * N_HEADS, g, 0.0))


def _prep(p, small, conv_w8, nega, dtb, first_tiles, last_tiles):
    ta = p.shape[0]
    n_tiles = ta // PREP_TILE
    sub = PREP_TILE // 16
    n_sub = ta // 16
    width = 3 * D_GDN
    kern = functools.partial(_prep_kernel, first_tiles=first_tiles, last_tiles=last_tiles)
    return pl.pallas_call(
        kern,
        grid=(n_tiles,),
        in_specs=[pl.BlockSpec((PREP_TILE, width), lambda r: (r, 0)),
                  pl.BlockSpec((16, width), lambda r: (jnp.maximum(r * sub - 1, 0), 0)),
                  pl.BlockSpec((16, width), lambda r: (jnp.minimum((r + 1) * sub, n_sub - 1), 0)),
                  pl.BlockSpec((8, width), lambda r: (0, 0)),
                  pl.BlockSpec((PREP_TILE, LANES), lambda r: (r, 0)),
                  pl.BlockSpec((1, LANES), lambda r: (0, 0)),
                  pl.BlockSpec((1, LANES), lambda r: (0, 0))],
        out_specs=[pl.BlockSpec((PREP_TILE, width), lambda r: (r, 0)),
                   pl.BlockSpec((PREP_TILE, LANES), lambda r: (r, 0))],
        out_shape=[jax.ShapeDtypeStruct((ta, width), BF16),
                   jax.ShapeDtypeStruct((ta, LANES), F32)],
        scratch_shapes=[pltpu.VMEM(((CONV_K - 1) * PREP_TILE, PREP_TILE), BF16)],
        compiler_params=_params("arbitrary"),
        name="prep",
    )(p, p, p, conv_w8, small, nega, dtb)


def _gdn_decays(d, gb):
    c = GDN_CHUNK
    row = lax.broadcasted_iota(jnp.int32, (c, c), 0)
    col = lax.broadcasted_iota(jnp.int32, (c, c), 1)
    incl = row >= col if d == 0 else row <= col
    lane = lax.broadcasted_iota(jnp.int32, gb.shape, 1)
    g_only = jnp.where(lane >= 2 * N_HEADS, jnp.where(lane < 4 * N_HEADS, gb, 0.0), 0.0)
    tri = jnp.where(incl, 1.0, 0.0).astype(BF16)
    g_hi = g_only.astype(BF16)
    g_r1 = g_only - g_hi.astype(F32)
    g_mid = g_r1.astype(BF16)
    g_lo = (g_r1 - g_mid.astype(F32)).astype(BF16)
    dot = lambda a, b: jnp.dot(a, b, preferred_element_type=F32)
    gcum = dot(tri, g_hi) + dot(tri, g_mid) + dot(tri, g_lo)
    g_end = gcum[c - 1:c, :] if d == 0 else gcum[0:1, :]
    return gcum, gcum.T, jnp.exp(gcum), jnp.exp(g_end - gcum), jnp.exp(g_end)


def _gdn_kernel(qf, kf, vf, gbf, qb, kb, vb, gbb, of_ref, ob_ref, s_ref):
    @pl.when(pl.program_id(1) == 0)
    def _():
        s_ref[...] = jnp.zeros_like(s_ref)

    c = GDN_CHUNK
    n_sub = qf.shape[0] // c
    row = lax.broadcasted_iota(jnp.int32, (c, c), 0)
    col = lax.broadcasted_iota(jnp.int32, (c, c), 1)
    eye = jnp.where(row == col, 1.0, 0.0)
    masks = ((row >= col, row > col), (row <= col, row < col))
    refs = ((qf, kf, vf, gbf, of_ref), (qb, kb, vb, gbb, ob_ref))

    def rows(d, sub):
        k = sub if d == 0 else n_sub - 1 - sub
        return slice(k * c, (k + 1) * c)

    scans = [(d, sub) for sub in range(n_sub) for d in range(2)]
    gbv = {ds: refs[ds[0]][3][rows(*ds), :] for ds in scans}
    dec = {ds: _gdn_decays(ds[0], gbv[ds]) for ds in scans}
    chains = [(d, sub, h) for d, sub in scans for h in range(N_HEADS)]

    st = []
    for d, sub, h in chains:
        q_ref, k_ref, v_ref, _, _ = refs[d]
        rs, hs = rows(d, sub), slice(h * HEAD_DIM, (h + 1) * HEAD_DIM)
        q, k, v = q_ref[rs, hs], k_ref[rs, hs], v_ref[rs, hs]
        both = _mm_nt(jnp.concatenate([q, k], axis=0), k)
        st.append(dict(q=q.astype(F32), k=k.astype(F32), v=v.astype(F32), qk=both[:c], kk=both[c:]))

    for (d, sub, h), e in zip(chains, st):
        cb = d * N_HEADS + h
        cg = 2 * N_HEADS + cb
        gcum, gcum_t, exp_g, exp_rest, exp_end = dec[d, sub]
        incl, strict = masks[d]
        e["beta"] = gbv[d, sub][:, cb:cb + 1]
        diff = gcum[:, cg:cg + 1] - gcum_t[cg:cg + 1, :]
        decay = jnp.where(incl, jnp.exp(jnp.minimum(diff, 0.0)), 0.0)
        e["y"] = jnp.where(strict, -(e["beta"] * e["kk"] * decay), 0.0)
        e["qk"] = e["qk"] * decay
        e["eg"] = exp_g[:, cg:cg + 1]
        e["er"] = exp_rest[:, cg:cg + 1]
        e["ee"] = exp_end[:, cg:cg + 1]

    levels = range(GDN_CHUNK.bit_length() - 1)
    sels = []
    for d in range(2):
        inner, outer = (col, row) if d == 0 else (row, col)
        sels.append([((outer >> lvl) & 1 == 1) & ((inner >> lvl) == (outer >> lvl) - 1) for lvl in levels])
    for (d, sub, h), e in zip(chains, st):
        e["t"] = eye + jnp.where(sels[d][0], e["y"], 0.0)
    def active_rows(d, b):
        return [(2 * p + 1 - d) * b for p in range(c // (2 * b))]

    def pick(x, starts, b):
        return jnp.concatenate([x[s:s + b] for s in starts], axis=0)

    def place(pieces, starts, b, base):
        out = []
        for blk in range(c // b):
            cur = None if base is None else base[blk * b:(blk + 1) * b]
            if blk * b in starts:
                piece = pieces[starts.index(blk * b) * b:(starts.index(blk * b) + 1) * b]
                cur = piece if cur is None else cur + piece
            out.append(jnp.zeros((b, c), F32) if cur is None else cur)
        return jnp.concatenate(out, axis=0)

    for lvl in levels[1:]:
        b = 1 << lvl
        if b % 8 == 0:
            for (d, sub, h), e in zip(chains, st):
                act = active_rows(d, b)
                yd = _mm(pick(jnp.where(sels[d][lvl], e["y"], 0.0), act, b), e["t"])
                e["yd"] = place(yd, act, b, None)
            for (d, sub, h), e in zip(chains, st):
                act = active_rows(d, b)
                e["t"] = place(_mm(pick(e["t"], act, b), e["yd"]), act, b, e["t"])
        else:
            for (d, sub, h), e in zip(chains, st):
                e["yd"] = _mm(jnp.where(sels[d][lvl], e["y"], 0.0), e["t"])
            for e in st:
                e["t"] = e["t"] + _mm(e["t"], e["yd"])

    for e in st:
        kb_ = e["k"] * e["beta"]
        uw = _mm(e["t"], jnp.concatenate([e["v"] * e["beta"], kb_ * e["eg"]], axis=-1))
        e["u"], e["w"] = uw[:, :HEAD_DIM], uw[:, HEAD_DIM:]

    state = {(d, h): s_ref[d, h] for d in range(2) for h in range(N_HEADS)}
    for step in range(n_sub):
        now = [(key, e) for key, e in zip(chains, st) if key[1] == step]
        for (d, sub, h), e in now:
            ws = _mm(jnp.concatenate([e["w"], e["q"] * e["eg"]], axis=0), state[d, h])
            e["v_new"] = e["u"] - ws[:c]
            e["o"] = ws[c:]
        for (d, sub, h), e in now:
            k_dec_t = (e["k"] * e["er"]).T
            out = _mm(jnp.concatenate([e["qk"], k_dec_t], axis=0), e["v_new"])
            refs[d][4][rows(d, sub), h * HEAD_DIM:(h + 1) * HEAD_DIM] = (e["o"] + out[:c]).astype(BF16)
            state[d, h] = state[d, h] * e["ee"] + out[c:]
    for (d, h), s in state.items():
        s_ref[d, h] = s


def _gdn(qkv, gb, batch, ctx_blocks, lat_blocks):
    ta = qkv.shape[0]
    rows = GDN_CHUNK * GDN_STEP_CHUNKS
    n_steps = ctx_blocks + lat_blocks
    lat0 = batch * ctx_blocks

    def fwd_blk(b, s):
        return jnp.where(s < ctx_blocks, b * ctx_blocks + s, lat0 + b * lat_blocks + (s - ctx_blocks))

    def bwd_blk(b, s):
        return jnp.where(s < ctx_blocks, b * ctx_blocks + (ctx_blocks - 1 - s),
                         lat0 + b * lat_blocks + (lat_blocks - 1 - (s - ctx_blocks)))

    def specs(blk):
        col = lambda j: pl.BlockSpec((rows, D_MODEL), lambda b, s: (blk(b, s), j))
        return [col(0), col(1), col(2), pl.BlockSpec((rows, LANES), lambda b, s: (blk(b, s), 0))]

    out = lambda blk: pl.BlockSpec((rows, D_MODEL), lambda b, s: (blk(b, s), 0))
    return pl.pallas_call(
        _gdn_kernel,
        grid=(batch, n_steps),
        in_specs=specs(fwd_blk) + specs(bwd_blk),
        out_specs=[out(fwd_blk), out(bwd_blk)],
        out_shape=[jax.ShapeDtypeStruct((ta, D_MODEL), BF16)] * 2,
        scratch_shapes=[pltpu.VMEM((2, N_HEADS, HEAD_DIM, HEAD_DIM), F32)],
        compiler_params=_params("arbitrary", "arbitrary"),
        name="gdn",
    )(qkv, qkv, qkv, gb, qkv, qkv, qkv, gb)


def _merge_kernel(of_ref, ob_ref, z_ref, su_ref, sv_ref, ga_ref, gb_ref, x_ref, ws_ref, bs_ref,
                  wa_ref, wb_ref, wo_ref, gng_ref, gate_ref, nfg_ref, shift_ref, scale_ref, wr_ref, br_ref,
                  xmid_ref, h2_ref, rgate_ref, ygdn_ref, ysgu_ref):
    tm = x_ref.shape[0]
    parts = [slice(k * tm // MERGE_PARTS, (k + 1) * tm // MERGE_PARTS) for k in range(MERGE_PARTS)]
    dot = lambda a, b: jnp.dot(a, b, preferred_element_type=F32)

    for ch in range(tm // SGU_CHUNK):
        rs = slice(ch * SGU_CHUNK, (ch + 1) * SGU_CHUNK)
        for g in range(SGU_GROUPS):
            cs = slice(g * SGU_GROUP_DIM, (g + 1) * SGU_GROUP_DIM)
            mixed = dot(ws_ref[g], sv_ref[rs, cs]) + bs_ref[:, cs]
            ysgu_ref[rs, cs] = (su_ref[rs, cs].astype(F32) * mixed).astype(BF16)

    for rs in parts:
        o = of_ref[rs, :].astype(F32) + ob_ref[rs, :].astype(F32)
        for h in range(N_HEADS):
            hs = slice(h * HEAD_DIM, (h + 1) * HEAD_DIM)
            oh = o[:, hs]
            inv = lax.rsqrt(jnp.mean(oh * oh, axis=-1, keepdims=True) + EPS)
            ygdn_ref[rs, hs] = (oh * inv * gng_ref[...] * z_ref[rs, hs].astype(F32)).astype(BF16)
    ya = [dot(ygdn_ref[rs, :], wa_ref[...]) for rs in parts]
    yb = [dot(ysgu_ref[rs, :], wb_ref[...]) for rs in parts]
    merged = [(ga_ref[rs, :].astype(F32) * a + gb_ref[rs, :].astype(F32) * b_).astype(BF16)
              for rs, a, b_ in zip(parts, ya, yb)]
    mix = [dot(m, wo_ref[...]) for m in merged]
    h2s = []
    for rs, mx in zip(parts, mix):
        xm = x_ref[rs, :] + gate_ref[0] * mx
        xmid_ref[rs, :] = xm
        hn = xm * lax.rsqrt(jnp.mean(xm * xm, axis=-1, keepdims=True) + EPS) * nfg_ref[...]
        h2 = hn * (1.0 + scale_ref[0]) + shift_ref[0]
        h2_ref[rs, :] = h2.astype(BF16)
        h2s.append(h2)

    wr = wr_ref[...]
    w_hi = wr.astype(BF16)
    w_lo = (wr - w_hi.astype(F32)).astype(BF16)
    logits = []
    for h2 in h2s:
        h_hi = h2.astype(BF16)
        h_lo = (h2 - h_hi.astype(F32)).astype(BF16)
        logits.append(dot(h_hi, w_hi) + dot(h_lo, w_hi) + dot(h_hi, w_lo) + br_ref[...])

    for rs, lg in zip(parts, logits):
        lane = lax.broadcasted_iota(jnp.int32, lg.shape, 1).astype(F32)
        far = float(LANES)
        gl = jnp.where(lane < N_EXPERT_GROUPS, lg, NEG_BIG)
        gmax = jnp.max(gl, axis=-1, keepdims=True)
        p_g = 1.0 / jnp.sum(jnp.exp(gl - gmax), axis=-1, keepdims=True)
        grp = jnp.min(jnp.where(gl == gmax, lane, far), axis=-1, keepdims=True)
        lo = N_EXPERT_GROUPS + EXPERTS_PER_GROUP * grp
        in_grp = jnp.where(lane >= lo, jnp.where(lane < lo + EXPERTS_PER_GROUP, 1.0, 0.0), 0.0)
        el = jnp.where(in_grp > 0.0, lg, NEG_BIG)
        m1 = jnp.max(el, axis=-1, keepdims=True)
        i1 = jnp.min(jnp.where(el == m1, lane, far), axis=-1, keepdims=True)
        el2 = jnp.where(lane == i1, NEG_BIG, el)
        m2 = jnp.max(el2, axis=-1, keepdims=True)
        i2 = jnp.min(jnp.where(el2 == m2, lane, far), axis=-1, keepdims=True)
        t = jnp.exp(m2 - m1)
        w1 = p_g / (1.0 + t)
        w2 = w1 * t
        rgate_ref[rs, :] = jnp.where(lane == 0.0, grp,
                                     jnp.where(lane == i1, w1, 0.0) + jnp.where(lane == i2, w2, 0.0))


def _merge(o_f, o_b, p, x2, w_s, b_s, w_a, w_b, w_o, gng, gate3, nfg, shift3, scale3, w_r, b_r,
           lat_tile0, tiles_per_batch):
    t_lat = x2.shape[0]
    n_tiles = t_lat // ROW_TILE
    lat = lambda c: pl.BlockSpec((ROW_TILE, D_MODEL), lambda i: (i + lat_tile0, c))
    own = lambda: pl.BlockSpec((ROW_TILE, D_MODEL), lambda i: (i, 0))
    mat = lambda: pl.BlockSpec((D_MODEL, D_MODEL), lambda i: (0, 0))
    vec = lambda: pl.BlockSpec((1, D_MODEL), lambda i: (0, 0))
    per_b = lambda: pl.BlockSpec((1, 1, D_MODEL), lambda i: (i // tiles_per_batch, 0, 0))
    return pl.pallas_call(
        _merge_kernel,
        grid=(n_tiles,),
        in_specs=[lat(0), lat(0), lat(PCOL_Z), lat(PCOL_U), lat(PCOL_SV), lat(PCOL_GA), lat(PCOL_GB), own(),
                  pl.BlockSpec((SGU_GROUPS, SGU_CHUNK, SGU_CHUNK), lambda i: (0, 0, 0)),
                  pl.BlockSpec((SGU_CHUNK, D_MODEL), lambda i: (0, 0)),
                  mat(), mat(), mat(), pl.BlockSpec((1, HEAD_DIM), lambda i: (0, 0)),
                  per_b(), vec(), per_b(), per_b(),
                  pl.BlockSpec((D_MODEL, LANES), lambda i: (0, 0)),
                  pl.BlockSpec((1, LANES), lambda i: (0, 0))],
        out_specs=[own(), own(), pl.BlockSpec((ROW_TILE, LANES), lambda i: (i, 0))],
        out_shape=[jax.ShapeDtypeStruct((t_lat, D_MODEL), F32),
                   jax.ShapeDtypeStruct((t_lat, D_MODEL), BF16),
                   jax.ShapeDtypeStruct((t_lat, LANES), F32)],
        scratch_shapes=[pltpu.VMEM((ROW_TILE, D_MODEL), BF16), pltpu.VMEM((ROW_TILE, D_MODEL), BF16)],
        compiler_params=_params("arbitrary"),
        name="merge",
    )(o_f, o_b, p, p, p, p, p, x2, w_s, b_s, w_a, w_b, w_o, gng, gate3, nfg, shift3, scale3, w_r, b_r)


def _split3(x):
    hi = x.astype(BF16)
    r1 = x - hi.astype(F32)
    mid = r1.astype(BF16)
    return hi, mid, (r1 - mid.astype(F32)).astype(BF16)


def _moe_kernel(h_ref, rg_ref, w1_ref, w3_ref, w2_ref, xmid_ref, gate_ref, fng_ref, o_ref,
                before_ref, destc_ref, xs_ref, ys_ref, gs_ref, start_ref):
    i, g, hf = pl.program_id(0), pl.program_id(1), pl.program_id(2)
    tm = h_ref.shape[0]
    dot = lambda a, b: jnp.dot(a, b, preferred_element_type=F32)

    @pl.when((i == 0) & (g == 0) & (hf == 0))
    def _():
        r = lax.broadcasted_iota(jnp.int32, (tm, tm), 0)
        c = lax.broadcasted_iota(jnp.int32, (tm, tm), 1)
        before_ref[...] = jnp.where(r < c, 1.0, 0.0).astype(BF16)

    @pl.when((g == 0) & (hf == 0))
    def _():
        rg = rg_ref[...]
        sub = lax.broadcasted_iota(jnp.int32, (8, tm), 0).astype(F32)
        m_rows = jnp.where(rg.T[0:1, :] == sub, 1.0, 0.0)
        total_r = jnp.sum(m_rows, axis=1, keepdims=True)
        start_r = jnp.zeros_like(total_r)
        for gg in range(N_EXPERT_GROUPS - 1):
            start_r = start_r + jnp.where(sub[:, 0:1] > gg, total_r[gg:gg + 1, :], 0.0)
        rank_r = dot(m_rows.astype(BF16), before_ref[...])
        dest_r = jnp.sum(jnp.where(m_rows > 0.0, rank_r + start_r, 0.0), axis=0, keepdims=True)
        destc_ref[...] = jnp.broadcast_to(dest_r, (LANES, tm)).T
        acc = jnp.int32(0)
        for gg in range(N_EXPERT_GROUPS):
            start_ref[gg] = acc
            acc = acc + total_r[gg, 0].astype(jnp.int32)
        start_ref[N_EXPERT_GROUPS] = acc
        slot = lax.broadcasted_iota(jnp.int32, (tm, tm), 0).astype(F32)
        perm = jnp.where(dest_r == slot, 1.0, 0.0).astype(BF16)
        g_hi, g_mid, _ = _split3(rg)
        moved = dot(perm, jnp.concatenate([h_ref[...], g_hi, g_mid], axis=1))
        xs_ref[...] = moved[:, :D_MODEL].astype(BF16)
        gs_ref[...] = moved[:, D_MODEL:D_MODEL + LANES] + moved[:, D_MODEL + LANES:]
        ys_ref[...] = jnp.zeros_like(ys_ref)

    bs = MOE_BLOCK
    first = start_ref[g] // bs
    n_unit = (start_ref[g + 1] + (bs - 1)) // bs - first

    def experts(row0, size):
        rows = pl.ds(pl.multiple_of(row0, bs), size)
        x = xs_ref[rows, :]
        gs = gs_ref[rows, :]
        lane = lax.broadcasted_iota(jnp.int32, gs.shape, 1)
        y = ys_ref[rows, :]
        ups = [(dot(x, w1_ref[e]), dot(x, w3_ref[e])) for e in range(MOE_EXPERTS_PER_STEP)]
        hids = []
        for e, (a, b) in enumerate(ups):
            col = N_EXPERT_GROUPS + g * EXPERTS_PER_GROUP + hf * MOE_EXPERTS_PER_STEP + e
            gate = jnp.sum(jnp.where(lane == col, gs, 0.0), axis=-1, keepdims=True)
            hids.append((_silu(a) * b * gate).astype(BF16))
        for e, hid in enumerate(hids):
            y = y + dot(hid, w2_ref[e])
        ys_ref[rows, :] = y

    big = MOE_BLOCK_MULTIPLES[0]

    def big_block(k, carry):
        experts((first + k * big) * bs, big * bs)
        return carry

    lax.fori_loop(0, n_unit // big, big_block, 0)
    done = first + (n_unit // big) * big
    for m in MOE_BLOCK_MULTIPLES[1:]:
        take = (n_unit & m) != 0

        @pl.when(take)
        def _(done=done, m=m):
            experts(done * bs, m * bs)

        done = done + jnp.where(take, m, 0)

    @pl.when((g == pl.num_programs(1) - 1) & (hf == pl.num_programs(2) - 1))
    def _():
        slot = lax.broadcasted_iota(jnp.int32, (tm, tm), 1).astype(F32)
        perm_t = jnp.where(destc_ref[:, 0:1] == slot, 1.0, 0.0).astype(BF16)
        xo = xmid_ref[...] + gate_ref[0] * dot(perm_t, ys_ref[...].astype(BF16))
        o_ref[...] = xo * lax.rsqrt(jnp.mean(xo * xo, axis=-1, keepdims=True) + EPS) * fng_ref[...]


def _moe(h2, rgate, w1, w3, w2, xmid, gate3, fng, tiles_per_batch):
    t_lat = h2.shape[0]
    n_tiles = t_lat // MOE_TILE
    eps = MOE_EXPERTS_PER_STEP
    halves = EXPERTS_PER_GROUP // eps
    row = lambda w: pl.BlockSpec((MOE_TILE, w), lambda i, g, hf: (i, 0))
    wspec = lambda a, b: pl.BlockSpec((eps, a, b), lambda i, g, hf: (g * halves + hf, 0, 0))
    return pl.pallas_call(
        _moe_kernel,
        grid=(n_tiles, N_EXPERT_GROUPS, halves),
        in_specs=[row(D_MODEL), row(LANES),
                  wspec(D_MODEL, D_EXPERT), wspec(D_MODEL, D_EXPERT), wspec(D_EXPERT, D_MODEL),
                  row(D_MODEL),
                  pl.BlockSpec((1, 1, D_MODEL), lambda i, g, hf: (i // tiles_per_batch, 0, 0)),
                  pl.BlockSpec((1, D_MODEL), lambda i, g, hf: (0, 0))],
        out_specs=row(D_MODEL),
        out_shape=jax.ShapeDtypeStruct((t_lat, D_MODEL), F32),
        scratch_shapes=[pltpu.VMEM((MOE_TILE, MOE_TILE), BF16),
                        pltpu.VMEM((MOE_TILE, LANES), F32),
                        pltpu.VMEM((MOE_TILE, D_MODEL), BF16),
                        pltpu.VMEM((MOE_TILE, D_MODEL), F32),
                        pltpu.VMEM((MOE_TILE, LANES), F32),
                        pltpu.SMEM((N_EXPERT_GROUPS + 1,), jnp.int32)],
        compiler_params=pltpu.CompilerParams(dimension_semantics=("arbitrary",) * 3,
                                             vmem_limit_bytes=MOE_VMEM_LIMIT),
        name="moe",
    )(h2, rgate, w1, w3, w2, xmid, gate3, fng)


def kernel(x, c, ctx, c_ctx, ada_w, ada_b, norm_mix_g, w_in, conv_w, a_log, dt_bias, gdn_norm_g, sgu_ln_g, sgu_ln_b, sgu_w, sgu_b, w_branch_a, w_branch_b, w_out, norm_ffn_g, router_group_w, router_group_b, router_expert_w, router_expert_b, expert_w1, expert_w3, expert_w2, final_norm_g):
    batch, seq, d = x.shape
    ctx_len = ctx.shape[1]
    assert d == D_MODEL and ada_w.shape[0] == 1, "single-layer block with D_MODEL channels"
    assert batch * ctx_len == ROW_TILE, "context rows of all samples form one row tile"
    assert seq % MOE_TILE == 0 and ctx_len % PREP_TILE == 0 and batch + 1 <= 8
    t_lat = batch * seq
    row = lambda v: v.reshape(1, -1).astype(F32)

    cond = jnp.zeros((8, d), F32).at[:batch].set(c).at[batch].set(c_ctx)
    mod = _adaln(cond, ada_w[0], ada_b[0]).reshape(8, N_MOD, d)
    mod_row = lambda k: mod[:batch + 1, k].reshape(batch + 1, 1, d)

    w_l = w_in[0]
    w_head = w_l[:, :COL_BETA].astype(BF16)
    w_tail = w_l[:, COL_Z:].astype(BF16)
    w_small = jnp.zeros((d, LANES), BF16).at[:, :4 * N_HEADS].set(w_l[:, COL_BETA:COL_Z].astype(BF16))
    x2 = x.reshape(t_lat, d)
    tiles_per_batch = seq // ROW_TILE
    p, small = _inproj(x2, ctx.reshape(batch * ctx_len, d), row(norm_mix_g), mod_row(0), mod_row(1),
                       w_head, w_tail, w_small, row(sgu_ln_g), row(sgu_ln_b), tiles_per_batch)

    ctx_t, lat_t = ctx_len // PREP_TILE, seq // PREP_TILE
    starts = [b * ctx_t for b in range(batch)] + [batch * ctx_t + b * lat_t for b in range(batch)]
    ends = [(b + 1) * ctx_t - 1 for b in range(batch)] + [batch * ctx_t + (b + 1) * lat_t - 1 for b in range(batch)]
    conv_w8 = jnp.zeros((8, 3 * D_GDN), F32).at[:CONV_K].set(conv_w[0])
    pad_lanes = lambda v: jnp.zeros((1, LANES), F32).at[0, 2 * N_HEADS:4 * N_HEADS].set(v.reshape(-1))
    qkv, gb = _prep(p, small, conv_w8, pad_lanes(-jnp.exp(a_log[0])), pad_lanes(dt_bias[0]),
                    tuple(starts), tuple(ends))

    step_rows = GDN_CHUNK * GDN_STEP_CHUNKS
    assert ctx_len % step_rows == 0 and seq % step_rows == 0
    o_f, o_b = _gdn(qkv, gb, batch, ctx_len // step_rows, seq // step_rows)

    b_full = jnp.repeat(sgu_b[0].T, SGU_GROUP_DIM, axis=1).astype(F32)

    w_r = jnp.zeros((d, LANES), F32).at[:, :N_EXPERT_GROUPS].set(router_group_w[0]) \
        .at[:, N_EXPERT_GROUPS:N_EXPERT_GROUPS + N_EXPERTS].set(router_expert_w[0])
    b_r = jnp.zeros((1, LANES), F32).at[0, :N_EXPERT_GROUPS].set(router_group_b[0]) \
        .at[0, N_EXPERT_GROUPS:N_EXPERT_GROUPS + N_EXPERTS].set(router_expert_b[0])
    gng = gdn_norm_g[0].reshape(1, HEAD_DIM).astype(F32)
    xmid, h2, rgate = _merge(o_f, o_b, p, x2, sgu_w[0].astype(BF16), b_full,
                             w_branch_a[0].astype(BF16), w_branch_b[0].astype(BF16),
                             w_out[0].astype(BF16), gng, mod_row(2)[:batch], row(norm_ffn_g),
                             mod_row(3)[:batch], mod_row(4)[:batch], w_r, b_r, 1, tiles_per_batch)

    out = _moe(h2, rgate, expert_w1[0].astype(BF16), expert_w3[0].astype(BF16), expert_w2[0].astype(BF16),
               xmid, mod_row(5)[:batch], row(final_norm_g), seq // MOE_TILE)
    return out.reshape(batch, seq, d)
```

```python
import functools
import math

import jax
import jax.numpy as jnp
from jax import lax
from jax.experimental import pallas as pl
from jax.experimental.pallas import tpu as pltpu

F32 = jnp.float32
BF16 = jnp.bfloat16

D_MODEL = 1024
N_HEADS = 8
HEAD_DIM = 128
D_GDN = N_HEADS * HEAD_DIM
CONV_K = 5
GDN_CHUNK = 64
GDN_STEP_CHUNKS = 4
SGU_GROUPS = 8
SGU_GROUP_DIM = 128
D_SGU = SGU_GROUPS * SGU_GROUP_DIM
SGU_CHUNK = 128
N_EXPERT_GROUPS = 4
EXPERTS_PER_GROUP = 8
N_EXPERTS = N_EXPERT_GROUPS * EXPERTS_PER_GROUP
D_EXPERT = 256
N_MOD = 6
EPS = 1e-6
COL_BETA = 3 * D_GDN
COL_Z = COL_BETA + 4 * N_HEADS

LANES = 128
ROW_TILE = 512
INPROJ_SEGS = 4
MERGE_PARTS = 2
PREP_TILE = 256
MOE_TILE = 1024
MOE_EXPERTS_PER_STEP = 4
MOE_BLOCK = 64
MOE_BLOCK_MULTIPLES = (8, 4, 2, 1)
VMEM_LIMIT = 48 * 1024 * 1024
MOE_WEIGHT_BUFFERS = 3
MOE_VMEM_LIMIT = 56 * 1024 * 1024
NEG_BIG = -1e30

PCOL_Q, PCOL_K, PCOL_V, PCOL_GB, PCOL_Z, PCOL_U, PCOL_SV, PCOL_GA = range(8)


def _mm(a, b):
    return jnp.dot(a.astype(BF16), b.astype(BF16), preferred_element_type=F32)


def _mm_nt(a, b):
    return lax.dot_general(a.astype(BF16), b.astype(BF16), (((1,), (1,)), ((), ())),
                           preferred_element_type=F32)


def _sigmoid(x):
    return 0.5 + 0.5 * jnp.tanh(0.5 * x)


def _silu(x):
    return x * _sigmoid(x)


def _gelu_tanh(x):
    return 0.5 * x * (1.0 + jnp.tanh(math.sqrt(2.0 / math.pi) * (x + 0.044715 * (x * x * x))))


def _params(*sem):
    return pltpu.CompilerParams(dimension_semantics=sem, vmem_limit_bytes=VMEM_LIMIT)


def _adaln_kernel(c_ref, w_ref, b_ref, o_ref):
    o_ref[...] = _mm(_silu(c_ref[...]), w_ref[...]) + b_ref[...]


def _adaln(cond, w, b):
    n = w.shape[1]
    tn = 1536
    return pl.pallas_call(
        _adaln_kernel,
        grid=(n // tn,),
        in_specs=[pl.BlockSpec((8, D_MODEL), lambda j: (0, 0)),
                  pl.BlockSpec((D_MODEL, tn), lambda j: (0, j)),
                  pl.BlockSpec((1, tn), lambda j: (0, j))],
        out_specs=pl.BlockSpec((8, tn), lambda j: (0, j)),
        out_shape=jax.ShapeDtypeStruct((8, n), F32),
        compiler_params=_params("arbitrary"),
        name="adaln",
    )(cond, w, b.reshape(1, n))


def _inproj_kernel(x_ref, ctx_ref, g_ref, shift_ref, scale_ref, wh_ref, wt_ref, ws_ref, lng_ref, lnb_ref,
                   p_ref, small_ref, h_ref):
    i = pl.program_id(0)
    j = pl.program_id(1)

    def norm_mod(xv):
        y = xv * lax.rsqrt(jnp.mean(xv * xv, axis=-1, keepdims=True) + EPS) * g_ref[...]
        return y * (1.0 + scale_ref[0]) + shift_ref[0]

    @pl.when(j == 0)
    def _():
        @pl.when(i == 0)
        def _():
            h_ref[...] = norm_mod(ctx_ref[...]).astype(BF16)

        @pl.when(i > 0)
        def _():
            h_ref[...] = norm_mod(x_ref[...]).astype(BF16)

        small_ref[...] = jnp.dot(h_ref[...], ws_ref[...], preferred_element_type=F32)

    def raw(a):
        return a

    def gelu_ln(a):
        a = _gelu_tanh(a)
        mu = jnp.mean(a, axis=-1, keepdims=True)
        ac = a - mu
        var = jnp.mean(ac * ac, axis=-1, keepdims=True)
        return ac * lax.rsqrt(var + EPS) * lng_ref[...] + lnb_ref[...]

    def project(w_ref, w_seg, p_seg, epilogue):
        a = jnp.dot(h_ref[...], w_ref[:, w_seg * D_MODEL:(w_seg + 1) * D_MODEL], preferred_element_type=F32)
        p_ref[:, p_seg * D_MODEL:(p_seg + 1) * D_MODEL] = epilogue(a).astype(BF16)

    @pl.when(j == 0)
    def _():
        for seg in range(3):
            project(wh_ref, seg, seg, raw)
        project(wt_ref, 0, 3, _sigmoid)

    @pl.when(j == 1)
    def _():
        for seg, epilogue in enumerate((_silu, _gelu_tanh, gelu_ln, _sigmoid)):
            project(wt_ref, seg, seg, epilogue)


def _inproj(x2, ctx2, norm_g, shift3, scale3, w_head, w_tail, w_small, ln_g, ln_b, tiles_per_batch):
    t_lat = x2.shape[0]
    n_lat = t_lat // ROW_TILE
    n_tiles = n_lat + 1
    tn = INPROJ_SEGS * D_MODEL
    assert w_head.shape[1] == tn - D_MODEL and w_tail.shape[1] == tn + D_MODEL
    n_batch = shift3.shape[0] - 1
    sel = lambda i: jnp.where(i == 0, n_batch, (jnp.maximum(i, 1) - 1) // tiles_per_batch)
    vec = lambda: pl.BlockSpec((1, D_MODEL), lambda i, j: (0, 0))
    once = pl.Buffered(1)
    return pl.pallas_call(
        _inproj_kernel,
        grid=(n_tiles, 2),
        in_specs=[pl.BlockSpec((ROW_TILE, D_MODEL), lambda i, j: (jnp.maximum(i, 1) - 1, 0)),
                  pl.BlockSpec((ROW_TILE, D_MODEL), lambda i, j: (0, 0), pipeline_mode=once),
                  vec(),
                  pl.BlockSpec((1, 1, D_MODEL), lambda i, j: (sel(i), 0, 0)),
                  pl.BlockSpec((1, 1, D_MODEL), lambda i, j: (sel(i), 0, 0)),
                  pl.BlockSpec((D_MODEL, tn - D_MODEL), lambda i, j: (0, 0), pipeline_mode=once),
                  pl.BlockSpec((D_MODEL, tn), lambda i, j: (0, 1 - j)),
                  pl.BlockSpec((D_MODEL, LANES), lambda i, j: (0, 0)),
                  vec(), vec()],
        out_specs=[pl.BlockSpec((ROW_TILE, tn), lambda i, j: (i, j)),
                   pl.BlockSpec((ROW_TILE, LANES), lambda i, j: (i, 0))],
        out_shape=[jax.ShapeDtypeStruct((n_tiles * ROW_TILE, 2 * tn), BF16),
                   jax.ShapeDtypeStruct((n_tiles * ROW_TILE, LANES), F32)],
        scratch_shapes=[pltpu.VMEM((ROW_TILE, D_MODEL), BF16)],
        compiler_params=_params("arbitrary", "arbitrary"),
        name="inproj",
    )(x2, ctx2, norm_g, shift3, scale3, w_head, w_tail, w_small, ln_g, ln_b)


def _prep_kernel(pm_ref, pp_ref, pn_ref, cw_ref, small_ref, nega_ref, dtb_ref, qkv_ref, gb_ref, shift_ref,
                 *, first_tiles, last_tiles):
    r = pl.program_id(0)
    tr = pm_ref.shape[0]
    is_first = functools.reduce(jnp.logical_or, [r == t for t in first_tiles])
    is_last = functools.reduce(jnp.logical_or, [r == t for t in last_tiles])
    keep_prev = jnp.where(is_first, 0.0, 1.0)
    keep_next = jnp.where(is_last, 0.0, 1.0)

    half = CONV_K // 2
    offsets = [o for o in range(-half, half + 1) if o != 0]

    @pl.when(r == 0)
    def _():
        i0 = lax.broadcasted_iota(jnp.int32, (tr, tr), 0)
        i1 = lax.broadcasted_iota(jnp.int32, (tr, tr), 1)
        for m, o in enumerate(offsets):
            shift_ref[m * tr:(m + 1) * tr, :] = jnp.where(i1 == i0 + o, 1.0, 0.0).astype(BF16)

    def conv_silu(cs):
        x = pm_ref[:, cs]
        tap = lambda o: cw_ref[half + o:half + o + 1, cs]
        shifted = jnp.dot(shift_ref[...], x, preferred_element_type=F32)
        acc = tap(0) * x.astype(F32)
        for m, o in enumerate(offsets):
            acc = acc + tap(o) * shifted[m * tr:(m + 1) * tr, :]
        prev = pp_ref[:, cs].astype(F32)[8:16, :] * keep_prev
        nxt = pn_ref[:, cs].astype(F32)[0:8, :] * keep_next
        sub = lax.broadcasted_iota(jnp.int32, prev.shape, 0)
        top = jnp.zeros_like(prev)
        bot = jnp.zeros_like(prev)
        for o in range(1, half + 1):
            top = top + tap(-o) * jnp.where(sub < o, pltpu.roll(prev, o, 0), 0.0)
            bot = bot + tap(o) * jnp.where(sub >= 8 - o, pltpu.roll(nxt, 8 - o, 0), 0.0)
        return _silu(jnp.concatenate([acc[0:8] + top, acc[8:tr - 8], acc[tr - 8:tr] + bot], axis=0))

    for j, scale in ((PCOL_Q, HEAD_DIM ** -0.5), (PCOL_K, 1.0)):
        y = conv_silu(slice(j * D_GDN, (j + 1) * D_GDN))
        for h in range(N_HEADS):
            yh = y[:, h * HEAD_DIM:(h + 1) * HEAD_DIM]
            inv = lax.rsqrt(jnp.sum(yh * yh, axis=-1, keepdims=True) + EPS) * scale
            qkv_ref[:, j * D_GDN + h * HEAD_DIM:j * D_GDN + (h + 1) * HEAD_DIM] = (yh * inv).astype(BF16)
    vs = slice(PCOL_V * D_GDN, (PCOL_V + 1) * D_GDN)
    qkv_ref[:, vs] = conv_silu(vs).astype(BF16)

    s = small_ref[...]
    lane = lax.broadcasted_iota(jnp.int32, s.shape, 1)
    beta = _sigmoid(s)
    z = s + dtb_ref[...]
    softplus = jnp.maximum(z, 0.0) + jnp.log(1.0 + jnp.exp(-jnp.abs(z)))
    g = nega_ref[...] * softplus
    gb_ref[...] = jnp.where(lane < 2 * N_HEADS, beta, jnp.where(lane < 4 * N_HEADS, g, 0.0))


def _prep(p, small, conv_w8, nega, dtb, first_tiles, last_tiles):
    ta = p.shape[0]
    n_tiles = ta // PREP_TILE
    sub = PREP_TILE // 16
    n_sub = ta // 16
    width = 3 * D_GDN
    kern = functools.partial(_prep_kernel, first_tiles=first_tiles, last_tiles=last_tiles)
    return pl.pallas_call(
        kern,
        grid=(n_tiles,),
        in_specs=[pl.BlockSpec((PREP_TILE, width), lambda r: (r, 0)),
                  pl.BlockSpec((16, width), lambda r: (jnp.maximum(r * sub - 1, 0), 0)),
                  pl.BlockSpec((16, width), lambda r: (jnp.minimum((r + 1) * sub, n_sub - 1), 0)),
                  pl.BlockSpec((8, width), lambda r: (0, 0)),
                  pl.BlockSpec((PREP_TILE, LANES), lambda r: (r, 0)),
                  pl.BlockSpec((1, LANES), lambda r: (0, 0)),
                  pl.BlockSpec((1, LANES), lambda r: (0, 0))],
        out_specs=[pl.BlockSpec((PREP_TILE, width), lambda r: (r, 0)),
                   pl.BlockSpec((PREP_TILE, LANES), lambda r: (r, 0))],
        out_shape=[jax.ShapeDtypeStruct((ta, width), BF16),
                   jax.ShapeDtypeStruct((ta, LANES), F32)],
        scratch_shapes=[pltpu.VMEM(((CONV_K - 1) * PREP_TILE, PREP_TILE), BF16)],
        compiler_params=_params("arbitrary"),
        name="prep",
    )(p, p, p, conv_w8, small, nega, dtb)


def _gdn_decays(d, gb):
    c = GDN_CHUNK
    row = lax.broadcasted_iota(jnp.int32, (c, c), 0)
    col = lax.broadcasted_iota(jnp.int32, (c, c), 1)
    incl = row >= col if d == 0 else row <= col
    lane = lax.broadcasted_iota(jnp.int32, gb.shape, 1)
    g_only = jnp.where(lane >= 2 * N_HEADS, jnp.where(lane < 4 * N_HEADS, gb, 0.0), 0.0)
    tri = jnp.where(incl, 1.0, 0.0).astype(BF16)
    g_hi = g_only.astype(BF16)
    g_r1 = g_only - g_hi.astype(F32)
    g_mid = g_r1.astype(BF16)
    g_lo = (g_r1 - g_mid.astype(F32)).astype(BF16)
    dot = lambda a, b: jnp.dot(a, b, preferred_element_type=F32)
    gcum = dot(tri, g_hi) + dot(tri, g_mid) + dot(tri, g_lo)
    g_end = gcum[c - 1:c, :] if d == 0 else gcum[0:1, :]
    return gcum, gcum.T, jnp.exp(gcum), jnp.exp(g_end - gcum), jnp.exp(g_end)


def _gdn_kernel(qf, kf, vf, gbf, qb, kb, vb, gbb, of_ref, ob_ref, s_ref):
    @pl.when(pl.program_id(1) == 0)
    def _():
        s_ref[...] = jnp.zeros_like(s_ref)

    c = GDN_CHUNK
    n_sub = qf.shape[0] // c
    row = lax.broadcasted_iota(jnp.int32, (c, c), 0)
    col = lax.broadcasted_iota(jnp.int32, (c, c), 1)
    eye = jnp.where(row == col, 1.0, 0.0)
    masks = ((row >= col, row > col), (row <= col, row < col))
    refs = ((qf, kf, vf, gbf, of_ref), (qb, kb, vb, gbb, ob_ref))

    def rows(d, sub):
        k = sub if d == 0 else n_sub - 1 - sub
        return slice(k * c, (k + 1) * c)

    scans = [(d, sub) for sub in range(n_sub) for d in range(2)]
    gbv = {ds: refs[ds[0]][3][rows(*ds), :] for ds in scans}
    dec = {ds: _gdn_decays(ds[0], gbv[ds]) for ds in scans}
    chains = [(d, sub, h) for d, sub in scans for h in range(N_HEADS)]

    st = []
    for d, sub, h in chains:
        q_ref, k_ref, v_ref, _, _ = refs[d]
        rs, hs = rows(d, sub), slice(h * HEAD_DIM, (h + 1) * HEAD_DIM)
        q, k, v = q_ref[rs, hs], k_ref[rs, hs], v_ref[rs, hs]
        both = _mm_nt(jnp.concatenate([q, k], axis=0), k)
        st.append(dict(q=q.astype(F32), k=k.astype(F32), v=v.astype(F32), qk=both[:c], kk=both[c:]))

    for (d, sub, h), e in zip(chains, st):
        cb = d * N_HEADS + h
        cg = 2 * N_HEADS + cb
        gcum, gcum_t, exp_g, exp_rest, exp_end = dec[d, sub]
        incl, strict = masks[d]
        e["beta"] = gbv[d, sub][:, cb:cb + 1]
        diff = gcum[:, cg:cg + 1] - gcum_t[cg:cg + 1, :]
        decay = jnp.where(incl, jnp.exp(jnp.minimum(diff, 0.0)), 0.0)
        e["y"] = jnp.where(strict, -(e["beta"] * e["kk"] * decay), 0.0)
        e["qk"] = e["qk"] * decay
        e["eg"] = exp_g[:, cg:cg + 1]
        e["er"] = exp_rest[:, cg:cg + 1]
        e["ee"] = exp_end[:, cg:cg + 1]

    levels = range(GDN_CHUNK.bit_length() - 1)
    sels = []
    for d in range(2):
        inner, outer = (col, row) if d == 0 else (row, col)
        sels.append([((outer >> lvl) & 1 == 1) & ((inner >> lvl) == (outer >> lvl) - 1) for lvl in levels])
    for (d, sub, h), e in zip(chains, st):
        e["t"] = eye + jnp.where(sels[d][0], e["y"], 0.0)
    def active_rows(d, b):
        return [(2 * p + 1 - d) * b for p in range(c // (2 * b))]

    def pick(x, starts, b):
        return jnp.concatenate([x[s:s + b] for s in starts], axis=0)

    def place(pieces, starts, b, base):
        out = []
        for blk in range(c // b):
            cur = None if base is None else base[blk * b:(blk + 1) * b]
            if blk * b in starts:
                piece = pieces[starts.index(blk * b) * b:(starts.index(blk * b) + 1) * b]
                cur = piece if cur is None else cur + piece
            out.append(jnp.zeros((b, c), F32) if cur is None else cur)
        return jnp.concatenate(out, axis=0)

    for lvl in levels[1:]:
        b = 1 << lvl
        if b % 8 == 0:
            for (d, sub, h), e in zip(chains, st):
                act = active_rows(d, b)
                yd = _mm(pick(jnp.where(sels[d][lvl], e["y"], 0.0), act, b), e["t"])
                e["yd"] = place(yd, act, b, None)
            for (d, sub, h), e in zip(chains, st):
                act = active_rows(d, b)
                e["t"] = place(_mm(pick(e["t"], act, b), e["yd"]), act, b, e["t"])
        else:
            for (d, sub, h), e in zip(chains, st):
                e["yd"] = _mm(jnp.where(sels[d][lvl], e["y"], 0.0), e["t"])
            for e in st:
                e["t"] = e["t"] + _mm(e["t"], e["yd"])

    for e in st:
        kb_ = e["k"] * e["beta"]
        uw = _mm(e["t"], jnp.concatenate([e["v"] * e["beta"], kb_ * e["eg"]], axis=-1))
        e["u"], e["w"] = uw[:, :HEAD_DIM], uw[:, HEAD_DIM:]

    state = {(d, h): s_ref[d, h] for d in range(2) for h in range(N_HEADS)}
    for step in range(n_sub):
        now = [(key, e) for key, e in zip(chains, st) if key[1] == step]
        for (d, sub, h), e in now:
            ws = _mm(jnp.concatenate([e["w"], e["q"] * e["eg"]], axis=0), state[d, h])
            e["v_new"] = e["u"] - ws[:c]
            e["o"] = ws[c:]
        for (d, sub, h), e in now:
            k_dec_t = (e["k"] * e["er"]).T
            out = _mm(jnp.concatenate([e["qk"], k_dec_t], axis=0), e["v_new"])
            refs[d][4][rows(d, sub), h * HEAD_DIM:(h + 1) * HEAD_DIM] = (e["o"] + out[:c]).astype(BF16)
            state[d, h] = state[d, h] * e["ee"] + out[c:]
    for (d, h), s in state.items():
        s_ref[d, h] = s


def _gdn(qkv, gb, batch, ctx_blocks, lat_blocks):
    ta = qkv.shape[0]
    rows = GDN_CHUNK * GDN_STEP_CHUNKS
    n_steps = ctx_blocks + lat_blocks
    lat0 = batch * ctx_blocks

    def fwd_blk(b, s):
        return jnp.where(s < ctx_blocks, b * ctx_blocks + s, lat0 + b * lat_blocks + (s - ctx_blocks))

    def bwd_blk(b, s):
        return jnp.where(s < ctx_blocks, b * ctx_blocks + (ctx_blocks - 1 - s),
                         lat0 + b * lat_blocks + (lat_blocks - 1 - (s - ctx_blocks)))

    def specs(blk):
        col = lambda j: pl.BlockSpec((rows, D_MODEL), lambda b, s: (blk(b, s), j))
        return [col(0), col(1), col(2), pl.BlockSpec((rows, LANES), lambda b, s: (blk(b, s), 0))]

    out = lambda blk: pl.BlockSpec((rows, D_MODEL), lambda b, s: (blk(b, s), 0))
    return pl.pallas_call(
        _gdn_kernel,
        grid=(batch, n_steps),
        in_specs=specs(fwd_blk) + specs(bwd_blk),
        out_specs=[out(fwd_blk), out(bwd_blk)],
        out_shape=[jax.ShapeDtypeStruct((ta, D_MODEL), BF16)] * 2,
        scratch_shapes=[pltpu.VMEM((2, N_HEADS, HEAD_DIM, HEAD_DIM), F32)],
        compiler_params=_params("arbitrary", "arbitrary"),
        name="gdn",
    )(qkv, qkv, qkv, gb, qkv, qkv, qkv, gb)


def _merge_kernel(of_ref, ob_ref, z_ref, su_ref, sv_ref, ga_ref, gb_ref, x_ref, ws_ref, bs_ref,
                  wa_ref, wb_ref, wo_ref, gng_ref, gate_ref, nfg_ref, shift_ref, scale_ref, wr_ref, br_ref,
                  xmid_ref, h2_ref, rgate_ref, ygdn_ref, ysgu_ref):
    tm = x_ref.shape[0]
    parts = [slice(k * tm // MERGE_PARTS, (k + 1) * tm // MERGE_PARTS) for k in range(MERGE_PARTS)]
    dot = lambda a, b: jnp.dot(a, b, preferred_element_type=F32)

    for ch in range(tm // SGU_CHUNK):
        rs = slice(ch * SGU_CHUNK, (ch + 1) * SGU_CHUNK)
        for g in range(SGU_GROUPS):
            cs = slice(g * SGU_GROUP_DIM, (g + 1) * SGU_GROUP_DIM)
            mixed = dot(ws_ref[g], sv_ref[rs, cs]) + bs_ref[:, cs]
            ysgu_ref[rs, cs] = (su_ref[rs, cs].astype(F32) * mixed).astype(BF16)

    for rs in parts:
        o = of_ref[rs, :].astype(F32) + ob_ref[rs, :].astype(F32)
        for h in range(N_HEADS):
            hs = slice(h * HEAD_DIM, (h + 1) * HEAD_DIM)
            oh = o[:, hs]
            inv = lax.rsqrt(jnp.mean(oh * oh, axis=-1, keepdims=True) + EPS)
            ygdn_ref[rs, hs] = (oh * inv * gng_ref[...] * z_ref[rs, hs].astype(F32)).astype(BF16)
    ya = [dot(ygdn_ref[rs, :], wa_ref[...]) for rs in parts]
    yb = [dot(ysgu_ref[rs, :], wb_ref[...]) for rs in parts]
    merged = [(ga_ref[rs, :].astype(F32) * a + gb_ref[rs, :].astype(F32) * b_).astype(BF16)
              for rs, a, b_ in zip(parts, ya, yb)]
    mix = [dot(m, wo_ref[...]) for m in merged]
    h2s = []
    for rs, mx in zip(parts, mix):
        xm = x_ref[rs, :] + gate_ref[0] * mx
        xmid_ref[rs, :] = xm
        hn = xm * lax.rsqrt(jnp.mean(xm * xm, axis=-1, keepdims=True) + EPS) * nfg_ref[...]
        h2 = hn * (1.0 + scale_ref[0]) + shift_ref[0]
        h2_ref[rs, :] = h2.astype(BF16)
        h2s.append(h2)

    wr = wr_ref[...]
    w_hi = wr.astype(BF16)
    w_lo = (wr - w_hi.astype(F32)).astype(BF16)
    logits = []
    for h2 in h2s:
        h_hi = h2.astype(BF16)
        h_lo = (h2 - h_hi.astype(F32)).astype(BF16)
        logits.append(dot(h_hi, w_hi) + dot(h_lo, w_hi) + dot(h_hi, w_lo) + br_ref[...])

    for rs, lg in zip(parts, logits):
        lane = lax.broadcasted_iota(jnp.int32, lg.shape, 1).astype(F32)
        far = float(LANES)
        gl = jnp.where(lane < N_EXPERT_GROUPS, lg, NEG_BIG)
        gmax = jnp.max(gl, axis=-1, keepdims=True)
        p_g = 1.0 / jnp.sum(jnp.exp(gl - gmax), axis=-1, keepdims=True)
        grp = jnp.min(jnp.where(gl == gmax, lane, far), axis=-1, keepdims=True)
        lo = N_EXPERT_GROUPS + EXPERTS_PER_GROUP * grp
        in_grp = jnp.where(lane >= lo, jnp.where(lane < lo + EXPERTS_PER_GROUP, 1.0, 0.0), 0.0)
        el = jnp.where(in_grp > 0.0, lg, NEG_BIG)
        m1 = jnp.max(el, axis=-1, keepdims=True)
        i1 = jnp.min(jnp.where(el == m1, lane, far), axis=-1, keepdims=True)
        el2 = jnp.where(lane == i1, NEG_BIG, el)
        m2 = jnp.max(el2, axis=-1, keepdims=True)
        i2 = jnp.min(jnp.where(el2 == m2, lane, far), axis=-1, keepdims=True)
        t = jnp.exp(m2 - m1)
        w1 = p_g / (1.0 + t)
        w2 = w1 * t
        rgate_ref[rs, :] = jnp.where(lane == 0.0, grp,
                                     jnp.where(lane == i1, w1, 0.0) + jnp.where(lane == i2, w2, 0.0))


def _merge(o_f, o_b, p, x2, w_s, b_s, w_a, w_b, w_o, gng, gate3, nfg, shift3, scale3, w_r, b_r,
           lat_tile0, tiles_per_batch):
    t_lat = x2.shape[0]
    n_tiles = t_lat // ROW_TILE
    lat = lambda c: pl.BlockSpec((ROW_TILE, D_MODEL), lambda i: (i + lat_tile0, c))
    own = lambda: pl.BlockSpec((ROW_TILE, D_MODEL), lambda i: (i, 0))
    mat = lambda: pl.BlockSpec((D_MODEL, D_MODEL), lambda i: (0, 0))
    vec = lambda: pl.BlockSpec((1, D_MODEL), lambda i: (0, 0))
    per_b = lambda: pl.BlockSpec((1, 1, D_MODEL), lambda i: (i // tiles_per_batch, 0, 0))
    return pl.pallas_call(
        _merge_kernel,
        grid=(n_tiles,),
        in_specs=[lat(0), lat(0), lat(PCOL_Z), lat(PCOL_U), lat(PCOL_SV), lat(PCOL_GA), lat(PCOL_GB), own(),
                  pl.BlockSpec((SGU_GROUPS, SGU_CHUNK, SGU_CHUNK), lambda i: (0, 0, 0)),
                  pl.BlockSpec((SGU_CHUNK, D_MODEL), lambda i: (0, 0)),
                  mat(), mat(), mat(), pl.BlockSpec((1, HEAD_DIM), lambda i: (0, 0)),
                  per_b(), vec(), per_b(), per_b(),
                  pl.BlockSpec((D_MODEL, LANES), lambda i: (0, 0)),
                  pl.BlockSpec((1, LANES), lambda i: (0, 0))],
        out_specs=[own(), own(), pl.BlockSpec((ROW_TILE, LANES), lambda i: (i, 0))],
        out_shape=[jax.ShapeDtypeStruct((t_lat, D_MODEL), F32),
                   jax.ShapeDtypeStruct((t_lat, D_MODEL), BF16),
                   jax.ShapeDtypeStruct((t_lat, LANES), F32)],
        scratch_shapes=[pltpu.VMEM((ROW_TILE, D_MODEL), BF16), pltpu.VMEM((ROW_TILE, D_MODEL), BF16)],
        compiler_params=_params("arbitrary"),
        name="merge",
    )(o_f, o_b, p, p, p, p, p, x2, w_s, b_s, w_a, w_b, w_o, gng, gate3, nfg, shift3, scale3, w_r, b_r)


def _split3(x):
    hi = x.astype(BF16)
    r1 = x - hi.astype(F32)
    mid = r1.astype(BF16)
    return hi, mid, (r1 - mid.astype(F32)).astype(BF16)


def _moe_kernel(h_ref, rg_ref, w1_hbm, w3_hbm, w2_hbm, xmid_ref, gate_ref, fng_ref, o_ref,
                before_ref, destc_ref, xs_ref, ys_ref, gs_ref, start_ref, w1_buf, w3_buf, w2_buf, w_sem):
    i, g, hf = pl.program_id(0), pl.program_id(1), pl.program_id(2)
    tm = h_ref.shape[0]
    dot = lambda a, b: jnp.dot(a, b, preferred_element_type=F32)

    eps = MOE_EXPERTS_PER_STEP
    per_tile = pl.num_programs(1) * pl.num_programs(2)
    step = (i * pl.num_programs(1) + g) * pl.num_programs(2) + hf
    n_steps = pl.num_programs(0) * per_tile
    ahead = MOE_WEIGHT_BUFFERS - 1

    def weight_copies(s):
        slot = s % MOE_WEIGHT_BUFFERS
        src = pl.ds((s % per_tile) * eps, eps)
        return [pltpu.make_async_copy(hbm.at[src], buf.at[slot], w_sem.at[k, slot])
                for k, (hbm, buf) in enumerate(((w1_hbm, w1_buf), (w3_hbm, w3_buf), (w2_hbm, w2_buf)))]

    @pl.when(step == 0)
    def _():
        for s in range(ahead):
            for cp in weight_copies(s):
                cp.start()

    @pl.when(step + ahead < n_steps)
    def _():
        for cp in weight_copies(step + ahead):
            cp.start()

    for cp in weight_copies(step):
        cp.wait()
    slot = step % MOE_WEIGHT_BUFFERS
    w1_ref, w3_ref, w2_ref = w1_buf.at[slot], w3_buf.at[slot], w2_buf.at[slot]

    @pl.when((i == 0) & (g == 0) & (hf == 0))
    def _():
        r = lax.broadcasted_iota(jnp.int32, (tm, tm), 0)
        c = lax.broadcasted_iota(jnp.int32, (tm, tm), 1)
        before_ref[...] = jnp.where(r < c, 1.0, 0.0).astype(BF16)

    @pl.when((g == 0) & (hf == 0))
    def _():
        rg = rg_ref[...]
        sub = lax.broadcasted_iota(jnp.int32, (8, tm), 0).astype(F32)
        m_rows = jnp.where(rg.T[0:1, :] == sub, 1.0, 0.0)
        total_r = jnp.sum(m_rows, axis=1, keepdims=True)
        start_r = jnp.zeros_like(total_r)
        for gg in range(N_EXPERT_GROUPS - 1):
            start_r = start_r + jnp.where(sub[:, 0:1] > gg, total_r[gg:gg + 1, :], 0.0)
        rank_r = dot(m_rows.astype(BF16), before_ref[...])
        dest_r = jnp.sum(jnp.where(m_rows > 0.0, rank_r + start_r, 0.0), axis=0, keepdims=True)
        destc_ref[...] = jnp.broadcast_to(dest_r, (LANES, tm)).T
        acc = jnp.int32(0)
        for gg in range(N_EXPERT_GROUPS):
            start_ref[gg] = acc
            acc = acc + total_r[gg, 0].astype(jnp.int32)
        start_ref[N_EXPERT_GROUPS] = acc
        slot = lax.broadcasted_iota(jnp.int32, (tm, tm), 0).astype(F32)
        perm = jnp.where(dest_r == slot, 1.0, 0.0).astype(BF16)
        g_hi, g_mid, _ = _split3(rg)
        moved = dot(perm, jnp.concatenate([h_ref[...], g_hi, g_mid], axis=1))
        xs_ref[...] = moved[:, :D_MODEL].astype(BF16)
        gs_ref[...] = moved[:, D_MODEL:D_MODEL + LANES] + moved[:, D_MODEL + LANES:]
        ys_ref[...] = jnp.zeros_like(ys_ref)

    bs = MOE_BLOCK
    first = start_ref[g] // bs
    n_unit = (start_ref[g + 1] + (bs - 1)) // bs - first

    def experts(row0, size):
        rows = pl.ds(pl.multiple_of(row0, bs), size)
        x = xs_ref[rows, :]
        gs = gs_ref[rows, :]
        lane = lax.broadcasted_iota(jnp.int32, gs.shape, 1)
        y = ys_ref[rows, :]
        ups = [(dot(x, w1_ref[e]), dot(x, w3_ref[e])) for e in range(MOE_EXPERTS_PER_STEP)]
        hids = []
        for e, (a, b) in enumerate(ups):
            col = N_EXPERT_GROUPS + g * EXPERTS_PER_GROUP + hf * MOE_EXPERTS_PER_STEP + e
            gate = jnp.sum(jnp.where(lane == col, gs, 0.0), axis=-1, keepdims=True)
            hids.append((_silu(a) * b * gate).astype(BF16))
        for e, hid in enumerate(hids):
            y = y + dot(hid, w2_ref[e])
        ys_ref[rows, :] = y

    big = MOE_BLOCK_MULTIPLES[0]

    def big_block(k, carry):
        experts((first + k * big) * bs, big * bs)
        return carry

    lax.fori_loop(0, n_unit // big, big_block, 0)
    done = first + (n_unit // big) * big
    for m in MOE_BLOCK_MULTIPLES[1:]:
        take = (n_unit & m) != 0

        @pl.when(take)
        def _(done=done, m=m):
            experts(done * bs, m * bs)

        done = done + jnp.where(take, m, 0)

    @pl.when((g == pl.num_programs(1) - 1) & (hf == pl.num_programs(2) - 1))
    def _():
        slot = lax.broadcasted_iota(jnp.int32, (tm, tm), 1).astype(F32)
        perm_t = jnp.where(destc_ref[:, 0:1] == slot, 1.0, 0.0).astype(BF16)
        xo = xmid_ref[...] + gate_ref[0] * dot(perm_t, ys_ref[...].astype(BF16))
        o_ref[...] = xo * lax.rsqrt(jnp.mean(xo * xo, axis=-1, keepdims=True) + EPS) * fng_ref[...]


def _moe(h2, rgate, w1, w3, w2, xmid, gate3, fng, tiles_per_batch):
    t_lat = h2.shape[0]
    n_tiles = t_lat // MOE_TILE
    eps = MOE_EXPERTS_PER_STEP
    halves = EXPERTS_PER_GROUP // eps
    row = lambda w: pl.BlockSpec((MOE_TILE, w), lambda i, g, hf: (i, 0))
    assert halves * eps == EXPERTS_PER_GROUP
    wspec = lambda a, b: pl.BlockSpec(memory_space=pl.ANY)
    wbuf = lambda a, b: pltpu.VMEM((MOE_WEIGHT_BUFFERS, eps, a, b), BF16)
    return pl.pallas_call(
        _moe_kernel,
        grid=(n_tiles, N_EXPERT_GROUPS, halves),
        in_specs=[row(D_MODEL), row(LANES),
                  wspec(D_MODEL, D_EXPERT), wspec(D_MODEL, D_EXPERT), wspec(D_EXPERT, D_MODEL),
                  row(D_MODEL),
                  pl.BlockSpec((1, 1, D_MODEL), lambda i, g, hf: (i // tiles_per_batch, 0, 0)),
                  pl.BlockSpec((1, D_MODEL), lambda i, g, hf: (0, 0))],
        out_specs=row(D_MODEL),
        out_shape=jax.ShapeDtypeStruct((t_lat, D_MODEL), F32),
        scratch_shapes=[pltpu.VMEM((MOE_TILE, MOE_TILE), BF16),
                        pltpu.VMEM((MOE_TILE, LANES), F32),
                        pltpu.VMEM((MOE_TILE, D_MODEL), BF16),
                        pltpu.VMEM((MOE_TILE, D_MODEL), F32),
                        pltpu.VMEM((MOE_TILE, LANES), F32),
                        pltpu.SMEM((N_EXPERT_GROUPS + 1,), jnp.int32),
                        wbuf(D_MODEL, D_EXPERT), wbuf(D_MODEL, D_EXPERT), wbuf(D_EXPERT, D_MODEL),
                        pltpu.SemaphoreType.DMA((3, MOE_WEIGHT_BUFFERS))],
        compiler_params=pltpu.CompilerParams(dimension_semantics=("arbitrary",) * 3,
                                             vmem_limit_bytes=MOE_VMEM_LIMIT),
        name="moe",
    )(h2, rgate, w1, w3, w2, xmid, gate3, fng)


def kernel(x, c, ctx, c_ctx, ada_w, ada_b, norm_mix_g, w_in, conv_w, a_log, dt_bias, gdn_norm_g, sgu_ln_g, sgu_ln_b, sgu_w, sgu_b, w_branch_a, w_branch_b, w_out, norm_ffn_g, router_group_w, router_group_b, router_expert_w, router_expert_b, expert_w1, expert_w3, expert_w2, final_norm_g):
    batch, seq, d = x.shape
    ctx_len = ctx.shape[1]
    assert d == D_MODEL and ada_w.shape[0] == 1, "single-layer block with D_MODEL channels"
    assert batch * ctx_len == ROW_TILE, "context rows of all samples form one row tile"
    assert seq % MOE_TILE == 0 and ctx_len % PREP_TILE == 0 and batch + 1 <= 8
    t_lat = batch * seq
    row = lambda v: v.reshape(1, -1).astype(F32)

    cond = jnp.zeros((8, d), F32).at[:batch].set(c).at[batch].set(c_ctx)
    mod = _adaln(cond, ada_w[0], ada_b[0]).reshape(8, N_MOD, d)
    mod_row = lambda k: mod[:batch + 1, k].reshape(batch + 1, 1, d)

    w_l = w_in[0]
    w_head = w_l[:, :COL_BETA].astype(BF16)
    w_tail = w_l[:, COL_Z:].astype(BF16)
    w_small = jnp.zeros((d, LANES), BF16).at[:, :4 * N_HEADS].set(w_l[:, COL_BETA:COL_Z].astype(BF16))
    x2 = x.reshape(t_lat, d)
    tiles_per_batch = seq // ROW_TILE
    p, small = _inproj(x2, ctx.reshape(batch * ctx_len, d), row(norm_mix_g), mod_row(0), mod_row(1),
                       w_head, w_tail, w_small, row(sgu_ln_g), row(sgu_ln_b), tiles_per_batch)

    ctx_t, lat_t = ctx_len // PREP_TILE, seq // PREP_TILE
    starts = [b * ctx_t for b in range(batch)] + [batch * ctx_t + b * lat_t for b in range(batch)]
    ends = [(b + 1) * ctx_t - 1 for b in range(batch)] + [batch * ctx_t + (b + 1) * lat_t - 1 for b in range(batch)]
    conv_w8 = jnp.zeros((8, 3 * D_GDN), F32).at[:CONV_K].set(conv_w[0])
    pad_lanes = lambda v: jnp.zeros((1, LANES), F32).at[0, 2 * N_HEADS:4 * N_HEADS].set(v.reshape(-1))
    qkv, gb = _prep(p, small, conv_w8, pad_lanes(-jnp.exp(a_log[0])), pad_lanes(dt_bias[0]),
                    tuple(starts), tuple(ends))

    step_rows = GDN_CHUNK * GDN_STEP_CHUNKS
    assert ctx_len % step_rows == 0 and seq % step_rows == 0
    o_f, o_b = _gdn(qkv, gb, batch, ctx_len // step_rows, seq // step_rows)

    b_full = jnp.repeat(sgu_b[0].T, SGU_GROUP_DIM, axis=1).astype(F32)

    w_r = jnp.zeros((d, LANES), F32).at[:, :N_EXPERT_GROUPS].set(router_group_w[0]) \
        .at[:, N_EXPERT_GROUPS:N_EXPERT_GROUPS + N_EXPERTS].set(router_expert_w[0])
    b_r = jnp.zeros((1, LANES), F32).at[0, :N_EXPERT_GROUPS].set(router_group_b[0]) \
        .at[0, N_EXPERT_GROUPS:N_EXPERT_GROUPS + N_EXPERTS].set(router_expert_b[0])
    gng = gdn_norm_g[0].reshape(1, HEAD_DIM).astype(F32)
    xmid, h2, rgate = _merge(o_f, o_b, p, x2, sgu_w[0].astype(BF16), b_full,
                             w_branch_a[0].astype(BF16), w_branch_b[0].astype(BF16),
                             w_out[0].astype(BF16), gng, mod_row(2)[:batch], row(norm_ffn_g),
                             mod_row(3)[:batch], mod_row(4)[:batch], w_r, b_r, 1, tiles_per_batch)

    out = _moe(h2, rgate, expert_w1[0].astype(BF16), expert_w3[0].astype(BF16), expert_w2[0].astype(BF16),
               xmid, mod_row(5)[:batch], row(final_norm_g), seq // MOE_TILE)
    return out.reshape(batch, seq, d)
```
